```python
import math
import jax
import jax.numpy as jnp
from jax import lax
import numpy as np

D_MODEL = 1024
BATCH = 4
SEQ = 4096
DEPTH = 2
DEC_BATCH = 128
DEC_SEQ = 1
PAST_LEN = 2048
PAGE_SIZE = 128

N_MIXERS = 2
N_GDN_LAYERS = (DEPTH + N_MIXERS - 1) // N_MIXERS
N_FOX_LAYERS = DEPTH // N_MIXERS
GDN_K_HEADS = D_MODEL // 128
GDN_V_HEADS = 2 * GDN_K_HEADS
GDN_HEAD_K = 128
GDN_HEAD_V = 128
GDN_CONV = 4
GDN_CHUNK = 64
GDN_QK = GDN_K_HEADS * GDN_HEAD_K
GDN_VW = GDN_V_HEADS * GDN_HEAD_V
GDN_CONV_DIM = 2 * GDN_QK + GDN_VW
GDN_IN = GDN_CONV_DIM + GDN_VW + 2 * GDN_V_HEADS
FOX_HEADS = D_MODEL // 128
FOX_HEAD_DIM = 128
FOX_W = FOX_HEADS * FOX_HEAD_DIM
FOX_IN = 4 * FOX_W + FOX_HEADS
FOX_SCALE = FOX_HEAD_DIM ** -0.5
Q_BLOCK = 128
FORGET_BIAS_INIT = 3.0
D_FF = ((8 * D_MODEL // 3 + 127) // 128) * 128
FFN_CONV = 3
EPS = 1e-6

kernel_name = "hybrid_gdn_fox_convffn_step"


def rmsnorm(x, w):
    xf = x.astype(jnp.float32)
    y = xf * lax.rsqrt(jnp.mean(xf * xf, axis=-1, keepdims=True) + EPS)
    return (y * w.astype(jnp.float32)).astype(x.dtype)


def l2norm(x):
    xf = x.astype(jnp.float32)
    return xf * lax.rsqrt(jnp.sum(xf * xf, axis=-1, keepdims=True) + EPS)


def causal_dwconv(u, buf, w):
    width = w.shape[0]
    L = u.shape[1]
    up = jnp.concatenate([buf.astype(u.dtype), u], axis=1)
    y = sum(up[:, j:j + L] * w[j] for j in range(width))
    return y, up[:, L:]


def gated_delta_rule(q, k, v, g, beta, s0):
    bsz, L, H, dk = q.shape
    dv = v.shape[-1]
    c = min(GDN_CHUNK, L)
    pad = (-L) % c
    n = (L + pad) // c
    q = q * dk ** -0.5

    def chunks(t):
        t = jnp.pad(t, [(0, 0), (0, pad)] + [(0, 0)] * (t.ndim - 2))
        t = t.reshape((bsz, n, c) + t.shape[2:])
        return jnp.swapaxes(jnp.moveaxis(t, 1, 0), 2, 3)

    lower = jnp.tril(jnp.ones((c, c), bool))
    strict = jnp.tril(jnp.ones((c, c), bool), -1)
    eye = jnp.eye(c, dtype=jnp.float32)

    def step(S, xs):
        qc, kc, vc, gc, bc = xs
        gcum = jnp.cumsum(gc, axis=-1)
        decay = jnp.exp(jnp.where(lower, gcum[..., :, None] - gcum[..., None, :], -jnp.inf))
        kb = kc * bc[..., None]
        a = jnp.where(strict, jnp.einsum('bhid,bhjd->bhij', kb, kc) * decay, 0.0) + eye
        rhs = jnp.concatenate([vc * bc[..., None], kb * jnp.exp(gcum)[..., None]], axis=-1)
        sol = lax.linalg.triangular_solve(a, rhs, left_side=True, lower=True, unit_diagonal=True)
        u, w = sol[..., :dv], sol[..., dv:]
        v_new = u - jnp.einsum('bhik,bhkv->bhiv', w, S)
        attn = jnp.where(lower, jnp.einsum('bhid,bhjd->bhij', qc, kc) * decay, 0.0)
        o = (jnp.einsum('bhik,bhkv->bhiv', qc * jnp.exp(gcum)[..., None], S)
             + jnp.einsum('bhij,bhjv->bhiv', attn, v_new))
        g_last = gcum[..., -1]
        S = (S * jnp.exp(g_last)[..., None, None]
             + jnp.einsum('bhik,bhiv->bhkv', kc * jnp.exp(g_last[..., None] - gcum)[..., None], v_new))
        return S, o

    S, o = lax.scan(step, s0, (chunks(q), chunks(k), chunks(v), chunks(g), chunks(beta)))
    o = jnp.moveaxis(jnp.swapaxes(o, 2, 3), 0, 1).reshape(bsz, n * c, H, dv)[:, :L]
    return o, S


def gdn_mixer(h, conv_buf, s0, w_in, conv_w, a_log, dt_bias, norm_w, w_out):
    bsz, L, _ = h.shape
    f32 = jnp.float32
    proj = h @ w_in
    qkv, z, b, a = jnp.split(proj, [GDN_CONV_DIM, GDN_CONV_DIM + GDN_VW, GDN_CONV_DIM + GDN_VW + GDN_V_HEADS], axis=-1)
    qkv_c, new_buf = causal_dwconv(qkv, conv_buf, conv_w)
    qkv_c = jax.nn.silu(qkv_c)
    q, k, v = jnp.split(qkv_c, [GDN_QK, 2 * GDN_QK], axis=-1)
    rep = GDN_V_HEADS // GDN_K_HEADS
    q = jnp.repeat(l2norm(q.reshape(bsz, L, GDN_K_HEADS, GDN_HEAD_K)), rep, axis=2)
    k = jnp.repeat(l2norm(k.reshape(bsz, L, GDN_K_HEADS, GDN_HEAD_K)), rep, axis=2)
    v = v.reshape(bsz, L, GDN_V_HEADS, GDN_HEAD_V).astype(f32)
    beta = jax.nn.sigmoid(b.astype(f32))
    g = -jnp.exp(a_log.astype(f32)) * jax.nn.softplus(a.astype(f32) + dt_bias.astype(f32))
    o, s_new = gated_delta_rule(q, k, v, g, beta, s0.astype(f32))
    o = rmsnorm(o, norm_w) * jax.nn.silu(z.reshape(bsz, L, GDN_V_HEADS, GDN_HEAD_V).astype(f32))
    y = o.reshape(bsz, L, GDN_VW).astype(h.dtype) @ w_out
    return y, new_buf, s_new.astype(s0.dtype)


def fox_logits(q, cq, q_pos, k, ck, k_pos):
    s = jnp.einsum('bqhd,bkhd->bhqk', q, k, preferred_element_type=jnp.float32) * FOX_SCALE
    s = s + jnp.swapaxes(cq, 1, 2)[..., :, None] - jnp.swapaxes(ck, 1, 2)[..., None, :]
    return jnp.where(k_pos[None, :] <= q_pos[:, None], s, -jnp.inf)


def fox_mixer(h, past, w_in, b_f, q_norm, k_norm, w_out):
    bsz, L, _ = h.shape
    f32 = jnp.float32
    shp = (bsz, L, FOX_HEADS, FOX_HEAD_DIM)
    proj = h @ w_in
    q, k, v, og, fl = jnp.split(proj, [FOX_W, 2 * FOX_W, 3 * FOX_W, 4 * FOX_W], axis=-1)
    q = rmsnorm(q.reshape(shp), q_norm)
    k = rmsnorm(k.reshape(shp), k_norm)
    v = v.reshape(shp)
    logf = jax.nn.log_sigmoid(fl.astype(f32) + b_f.astype(f32))
    if past is None:
        c = jnp.cumsum(logf, axis=1)
        nb = L // Q_BLOCK
        pos = jnp.arange(L)

        def block(args):
            qb, cqb, pb = args
            p = jax.nn.softmax(fox_logits(qb, cqb, pb, k, c, pos), axis=-1).astype(v.dtype)
            return jnp.einsum('bhqk,bkhd->bqhd', p, v)

        qb = jnp.swapaxes(q.reshape(bsz, nb, Q_BLOCK, FOX_HEADS, FOX_HEAD_DIM), 0, 1)
        cqb = jnp.swapaxes(c.reshape(bsz, nb, Q_BLOCK, FOX_HEADS), 0, 1)
        o = lax.map(block, (qb, cqb, pos.reshape(nb, Q_BLOCK)))
        o = jnp.swapaxes(o, 0, 1).reshape(shp)
    else:
        k_past, v_past, lf_past = past
        P = k_past.shape[1]
        c_past = jnp.cumsum(lf_past.astype(f32), axis=1)
        c_new = c_past[:, -1:] + jnp.cumsum(logf, axis=1)
        q_pos = P + jnp.arange(L)
        s = jnp.concatenate([fox_logits(q, c_new, q_pos, k_past, c_past, jnp.arange(P)),
                             fox_logits(q, c_new, q_pos, k, c_new, q_pos)], axis=-1)
        p = jax.nn.softmax(s, axis=-1).astype(v.dtype)
        o = (jnp.einsum('bhqk,bkhd->bqhd', p[..., :P], v_past)
             + jnp.einsum('bhqk,bkhd->bqhd', p[..., P:], v))
    o = o.astype(f32) * jax.nn.sigmoid(og.astype(f32)).reshape(shp)
    y = o.reshape(bsz, L, FOX_W).astype(h.dtype) @ w_out
    return y, k, v, logf.astype(h.dtype)


def conv_ffn(h, buf, w_in, conv_w, conv_b, w_out):
    u = h @ w_in
    uc, new_buf = causal_dwconv(u, buf, conv_w)
    gate, up = jnp.split(uc + conv_b, 2, axis=-1)
    return (jax.nn.silu(gate) * up) @ w_out, new_buf


def run_trunk(x, gdn_s0, gdn_cb0, ffn_cb0, fox_cache, norm_mix, norm_ffn,
              gdn_w_in, gdn_conv_w, gdn_a_log, gdn_dt_bias, gdn_norm, gdn_w_out,
              fox_w_in, fox_b_f, fox_q_norm, fox_k_norm, fox_w_out,
              ffn_w_in, ffn_conv_w, ffn_conv_b, ffn_w_out):
    gdn_s, gdn_cb, ks, vs, lfs, ffn_cb = [], [], [], [], [], []
    for i in range(DEPTH):
        j = i // N_MIXERS
        h = rmsnorm(x, norm_mix[i])
        if i % N_MIXERS == 0:
            y, cb, s = gdn_mixer(h, gdn_cb0[j], gdn_s0[j], gdn_w_in[j], gdn_conv_w[j], gdn_a_log[j],
                                 gdn_dt_bias[j], gdn_norm[j], gdn_w_out[j])
            gdn_cb.append(cb)
            gdn_s.append(s)
        else:
            past = None
            if fox_cache is not None:
                ck, cv, clf, pt = fox_cache
                nseq = pt.shape[0]
                past = (ck[j, pt].reshape(nseq, -1, FOX_HEADS, FOX_HEAD_DIM),
                        cv[j, pt].reshape(nseq, -1, FOX_HEADS, FOX_HEAD_DIM),
                        clf[j, pt].reshape(nseq, -1, FOX_HEADS))
            y, k, v, lf = fox_mixer(h, past, fox_w_in[j], fox_b_f[j], fox_q_norm[j], fox_k_norm[j], fox_w_out[j])
            ks.append(k)
            vs.append(v)
            lfs.append(lf)
        x = x + y
        f, fcb = conv_ffn(rmsnorm(x, norm_ffn[i]), ffn_cb0[i], ffn_w_in[i], ffn_conv_w[i], ffn_conv_b[i], ffn_w_out[i])
        ffn_cb.append(fcb)
        x = x + f
    return x, jnp.stack(gdn_s), jnp.stack(gdn_cb), jnp.stack(ks), jnp.stack(vs), jnp.stack(lfs), jnp.stack(ffn_cb)


def setup_inputs(seed: int = 0) -> dict:
    key = jax.random.key(seed)
    keys = iter(jax.random.split(key, 40))

    def nrm(shape, scale=1.0):
        return scale * jax.random.normal(next(keys), shape, jnp.float32)

    n_pages = PAST_LEN // PAGE_SIZE
    n_used = DEC_BATCH * n_pages
    n_pool = n_used + max(1, n_used // 4)
    page_table = jax.random.permutation(next(keys), n_pool)[:n_used].reshape(DEC_BATCH, n_pages).astype(jnp.int32)
    dt = jnp.exp(jax.random.uniform(next(keys), (N_GDN_LAYERS, GDN_V_HEADS), jnp.float32,
                                    math.log(1e-3), math.log(1e-1)))
    a_log = jnp.log(jax.random.uniform(next(keys), (N_GDN_LAYERS, GDN_V_HEADS), jnp.float32, 1.0, 16.0))
    return {
        "x_prompt": nrm((BATCH, SEQ, D_MODEL)),
        "x_sample": nrm((DEC_BATCH, DEC_SEQ, D_MODEL)),
        "state_gdn": nrm((N_GDN_LAYERS, DEC_BATCH, GDN_V_HEADS, GDN_HEAD_K, GDN_HEAD_V), 0.1),
        "state_gdn_conv": nrm((N_GDN_LAYERS, DEC_BATCH, GDN_CONV - 1, GDN_CONV_DIM)),
        "cache_k": nrm((N_FOX_LAYERS, n_pool, PAGE_SIZE, FOX_HEADS, FOX_HEAD_DIM)),
        "cache_v": nrm((N_FOX_LAYERS, n_pool, PAGE_SIZE, FOX_HEADS, FOX_HEAD_DIM)),
        "cache_logf": jax.nn.log_sigmoid(FORGET_BIAS_INIT + nrm((N_FOX_LAYERS, n_pool, PAGE_SIZE, FOX_HEADS))),
        "state_ffn_conv": nrm((DEPTH, DEC_BATCH, FFN_CONV - 1, 2 * D_FF)),
        "page_table": page_table,
        "norm_mix": 1.0 + nrm((DEPTH, D_MODEL), 0.02),
        "norm_ffn": 1.0 + nrm((DEPTH, D_MODEL), 0.02),
        "gdn_w_in": nrm((N_GDN_LAYERS, D_MODEL, GDN_IN), D_MODEL ** -0.5),
        "gdn_conv_w": nrm((N_GDN_LAYERS, GDN_CONV, GDN_CONV_DIM), GDN_CONV ** -0.5),
        "gdn_a_log": a_log,
        "gdn_dt_bias": dt + jnp.log(-jnp.expm1(-dt)),
        "gdn_norm": 1.0 + nrm((N_GDN_LAYERS, GDN_HEAD_V), 0.02),
        "gdn_w_out": nrm((N_GDN_LAYERS, GDN_VW, D_MODEL), GDN_VW ** -0.5),
        "fox_w_in": nrm((N_FOX_LAYERS, D_MODEL, FOX_IN), D_MODEL ** -0.5),
        "fox_b_f": FORGET_BIAS_INIT + nrm((N_FOX_LAYERS, FOX_HEADS), 0.1),
        "fox_q_norm": 1.0 + nrm((N_FOX_LAYERS, FOX_HEAD_DIM), 0.02),
        "fox_k_norm": 1.0 + nrm((N_FOX_LAYERS, FOX_HEAD_DIM), 0.02),
        "fox_w_out": nrm((N_FOX_LAYERS, FOX_W, D_MODEL), FOX_W ** -0.5),
        "ffn_w_in": nrm((DEPTH, D_MODEL, 2 * D_FF), D_MODEL ** -0.5),
        "ffn_conv_w": nrm((DEPTH, FFN_CONV, 2 * D_FF), FFN_CONV ** -0.5),
        "ffn_conv_b": nrm((DEPTH, 2 * D_FF), 0.01),
        "ffn_w_out": nrm((DEPTH, D_FF, D_MODEL), D_FF ** -0.5),
    }


def reference(x_prompt, x_sample, state_gdn, state_gdn_conv, cache_k, cache_v, cache_logf, state_ffn_conv,
              page_table, norm_mix, norm_ffn, gdn_w_in, gdn_conv_w, gdn_a_log, gdn_dt_bias, gdn_norm, gdn_w_out,
              fox_w_in, fox_b_f, fox_q_norm, fox_k_norm, fox_w_out, ffn_w_in, ffn_conv_w, ffn_conv_b, ffn_w_out):
    weights = (norm_mix, norm_ffn, gdn_w_in, gdn_conv_w, gdn_a_log, gdn_dt_bias, gdn_norm, gdn_w_out,
               fox_w_in, fox_b_f, fox_q_norm, fox_k_norm, fox_w_out, ffn_w_in, ffn_conv_w, ffn_conv_b, ffn_w_out)
    bp = x_prompt.shape[0]
    dt = x_prompt.dtype
    p_s0 = jnp.zeros((N_GDN_LAYERS, bp, GDN_V_HEADS, GDN_HEAD_K, GDN_HEAD_V), dt)
    p_cb0 = jnp.zeros((N_GDN_LAYERS, bp, GDN_CONV - 1, GDN_CONV_DIM), dt)
    p_fcb0 = jnp.zeros((DEPTH, bp, FFN_CONV - 1, 2 * D_FF), dt)
    y_prompt, p_s, p_cb, p_k, p_v, p_lf, p_fcb = run_trunk(x_prompt, p_s0, p_cb0, p_fcb0, None, *weights)
    y_sample, s_s, s_cb, s_k, s_v, s_lf, s_fcb = run_trunk(
        x_sample, state_gdn, state_gdn_conv, state_ffn_conv, (cache_k, cache_v, cache_logf, page_table), *weights)
    return (y_prompt, y_sample, p_s, p_cb, p_k, p_v, p_lf, p_fcb, s_s, s_cb, s_k, s_v, s_lf, s_fcb)
```

```python
import functools

import jax
import jax.numpy as jnp
from jax import lax
from jax.experimental import pallas as pl
from jax.experimental.pallas import tpu as pltpu

F32 = jnp.float32
BF16 = jnp.bfloat16

HEAD_DIM = 128
GDN_CHUNK = 64
EPS = 1e-6
V7X_VMEM_LIMIT_BYTES = 56 * 1024 * 1024
ROW_TILE = 512
ATTN_TILE = 512
CONV_COLS = 512
FFN_COLS = 256


def _cparams(*semantics):
    return pltpu.CompilerParams(dimension_semantics=semantics, vmem_limit_bytes=V7X_VMEM_LIMIT_BYTES)


def _resident(shape):
    return pl.BlockSpec(shape, lambda *_: (0,) * len(shape), pipeline_mode=pl.Buffered(1))


def _sigmoid(x):
    return 1.0 / (1.0 + jnp.exp(-x))


def _silu(x):
    return x * _sigmoid(x)


def _softplus(x):
    return jnp.maximum(x, 0.0) + jnp.log(1.0 + jnp.exp(-jnp.abs(x)))


def _rmsnorm_rows(x, w):
    return x * lax.rsqrt(jnp.mean(x * x, axis=-1, keepdims=True) + EPS) * w


def _dot(a, b):
    return jnp.dot(a, b, preferred_element_type=F32)


def _dot_nt(a, b):
    return lax.dot_general(a, b, (((1,), (1,)), ((), ())), preferred_element_type=F32)


def _dot_tn(a, b):
    return lax.dot_general(a, b, (((0,), (0,)), ((), ())), preferred_element_type=F32)


def _split3(x):
    hi = x.astype(BF16)
    r = x - hi.astype(F32)
    mid = r.astype(BF16)
    lo = (r - mid.astype(F32)).astype(BF16)
    return hi, mid, lo


def _dot_sel_left(sel, x):
    hi, mid, lo = _split3(x)
    return _dot(sel, hi) + _dot(sel, mid) + _dot(sel, lo)


def _dot_sel_right(x, sel):
    hi, mid, lo = _split3(x)
    return _dot(hi, sel) + _dot(mid, sel) + _dot(lo, sel)


def _dot_f32(a, b):
    return jnp.dot(a, b, preferred_element_type=F32, precision=lax.Precision.HIGHEST)


def _iota2(shape, dim):
    return lax.broadcasted_iota(jnp.int32, shape, dim)


def _gdn_in_kernel(x_ref, nw_ref, wqkv_ref, wz_ref, wba_ref, cw_ref, alog_ref, dtb_ref, buf_ref,
                   q_ref, k_ref, v_ref, z_ref, beta_ref, g_ref, cs_ref, ubuf, carry,
                   *, decode, tm, qk_width, conv_dim, n_vheads):
    hb = _rmsnorm_rows(x_ref[...], nw_ref[...]).astype(BF16)

    if not decode:
        @pl.when(pl.program_id(1) == 0)
        def _():
            carry[...] = jnp.zeros_like(carry)
            carry[5:8, :] = buf_ref[...]

    wc = CONV_COLS
    for j in range(conv_dim // wc):
        c0 = j * wc
        cols = slice(c0, c0 + wc)
        u = _dot(hb, wqkv_ref[:, cols])
        if decode:
            b0 = buf_ref[:, c0:c0 + wc]
            b1 = buf_ref[:, conv_dim + c0:conv_dim + c0 + wc]
            b2 = buf_ref[:, 2 * conv_dim + c0:2 * conv_dim + c0 + wc]
            y = cw_ref[0:1, cols] * b0 + cw_ref[1:2, cols] * b1 + cw_ref[2:3, cols] * b2 + cw_ref[3:4, cols] * u
            cs_ref[:, c0:c0 + wc] = b1
            cs_ref[:, conv_dim + c0:conv_dim + c0 + wc] = b2
            cs_ref[:, 2 * conv_dim + c0:2 * conv_dim + c0 + wc] = u
        else:
            ubuf[0:8, :] = carry[:, cols]
            ubuf[8:8 + tm, :] = u
            y = (cw_ref[3:4, cols] * u + cw_ref[2:3, cols] * ubuf[7:7 + tm, :]
                 + cw_ref[1:2, cols] * ubuf[6:6 + tm, :] + cw_ref[0:1, cols] * ubuf[5:5 + tm, :])
            carry[:, cols] = ubuf[tm:tm + 8, :]
        y = _silu(y)
        for hh in range(wc // HEAD_DIM):
            yh = y[:, hh * HEAD_DIM:(hh + 1) * HEAD_DIM]
            col = c0 + hh * HEAD_DIM
            if col < 2 * qk_width:
                yh = yh * lax.rsqrt(jnp.sum(yh * yh, axis=-1, keepdims=True) + EPS)
            if col < qk_width:
                q_ref[:, col:col + HEAD_DIM] = yh
            elif col < 2 * qk_width:
                k_ref[:, col - qk_width:col - qk_width + HEAD_DIM] = yh
            else:
                v_ref[:, col - 2 * qk_width:col - 2 * qk_width + HEAD_DIM] = yh
    if not decode:
        cs_ref[...] = carry[5:8, :]

    z_ref[...] = _dot(hb, wz_ref[...])
    ba = _dot(hb, wba_ref[...])
    beta_ref[...] = _sigmoid(ba[:, :n_vheads])
    g_ref[...] = -jnp.exp(alog_ref[...]) * _softplus(ba[:, n_vheads:] + dtb_ref[...])


def _gdn_in(x, buf, nw, wqkv, wz, wba, cw, alog, dtb, *, decode):
    bsz, seq, dm = x.shape
    conv_dim = wqkv.shape[1]
    vw = wz.shape[1]
    qk_width = (conv_dim - vw) // 2
    n_vheads = vw // HEAD_DIM
    tm = min(ROW_TILE, seq)
    grid = (bsz, seq // tm)
    row = lambda width: pl.BlockSpec((None, tm, width), lambda b, i: (b, i, 0))
    if decode:
        buf_spec = pl.BlockSpec((None, tm, 3 * conv_dim), lambda b, i: (b, i, 0))
        cs_shape, cs_spec = (bsz, seq, 3 * conv_dim), pl.BlockSpec((None, tm, 3 * conv_dim), lambda b, i: (b, i, 0))
    else:
        buf_spec = pl.BlockSpec((None, 3, conv_dim), lambda b, i: (b, 0, 0))
        cs_shape, cs_spec = (bsz, 3, conv_dim), pl.BlockSpec((None, 3, conv_dim), lambda b, i: (b, 0, 0))
    kern = functools.partial(_gdn_in_kernel, decode=decode, tm=tm, qk_width=qk_width, conv_dim=conv_dim,
                             n_vheads=n_vheads)
    return pl.pallas_call(
        kern,
        grid=grid,
        in_specs=[row(dm), _resident((1, dm)), _resident(wqkv.shape), _resident(wz.shape), _resident(wba.shape),
                  _resident(cw.shape), _resident((1, n_vheads)), _resident((1, n_vheads)), buf_spec],
        out_specs=[row(qk_width), row(qk_width), row(vw), row(vw), row(n_vheads), row(n_vheads), cs_spec],
        out_shape=[jax.ShapeDtypeStruct((bsz, seq, qk_width), F32), jax.ShapeDtypeStruct((bsz, seq, qk_width), F32),
                   jax.ShapeDtypeStruct((bsz, seq, vw), F32), jax.ShapeDtypeStruct((bsz, seq, vw), F32),
                   jax.ShapeDtypeStruct((bsz, seq, n_vheads), F32), jax.ShapeDtypeStruct((bsz, seq, n_vheads), F32),
                   jax.ShapeDtypeStruct(cs_shape, F32)],
        scratch_shapes=[pltpu.VMEM((tm + 8, CONV_COLS), F32), pltpu.VMEM((8, conv_dim), F32)],
        compiler_params=_cparams("arbitrary", "arbitrary"),
        name="gdn_in_decode" if decode else "gdn_in_prompt",
    )(x, nw, wqkv, wz, wba, cw, alog, dtb, buf)


def _unit_lower_inverse(n, row, col):
    inv = jnp.where(row == col, 1.0, 0.0) - jnp.where((row >> 1) == (col >> 1), n, 0.0)
    size = n.shape[0]
    bits = 2
    while (1 << bits) <= size:
        off_diag = jnp.where((row >> bits) == (col >> bits),
                             jnp.where((row >> (bits - 1)) == (col >> (bits - 1)), 0.0, n), 0.0)
        inv = inv - _dot_f32(inv, _dot_f32(off_diag, inv))
        bits += 1
    return inv


def _gdn_core_kernel(q_ref, k_ref, v_ref, z_ref, beta_ref, g_ref, nw_ref, o_ref, s_ref, *, seq):
    c = GDN_CHUNK
    h = pl.program_id(1)
    row = _iota2((c, c), 0)
    col = _iota2((c, c), 1)
    lower = row >= col
    strict = row > col
    tri_lower = jnp.where(lower, 1.0, 0.0).astype(BF16)
    ones_upper = jnp.where(row <= col, 1.0, 0.0)
    all_ones = jnp.ones((c, c), BF16)
    head_lane = _iota2((c, beta_ref.shape[-1]), 1) == h
    q_scale = HEAD_DIM ** -0.5
    s_ref[...] = jnp.zeros_like(s_ref)

    def chunk(i, _):
        rows = pl.ds(pl.multiple_of(i * c, c), c)
        qc = q_ref[rows, :] * q_scale
        kc = k_ref[rows, :]
        vc = v_ref[rows, :]
        bcol = jnp.sum(jnp.where(head_lane, beta_ref[rows, :], 0.0), axis=-1, keepdims=True)
        gcol = jnp.sum(jnp.where(head_lane, g_ref[rows, :], 0.0), axis=-1, keepdims=True)
        gcum = _dot_sel_left(tri_lower, jnp.broadcast_to(gcol, (c, HEAD_DIM)))
        gcum_cols = _dot_sel_left(all_ones, jnp.broadcast_to(gcol, (c, c)) * ones_upper)
        decay = jnp.where(lower, jnp.exp(jnp.minimum(gcum[:, :c] - gcum_cols, 0.0)), 0.0)
        kb = kc * bcol
        kcb = kc.astype(BF16)
        n = jnp.where(strict, _dot_nt(kb.astype(BF16), kcb) * decay, 0.0)
        inv = _unit_lower_inverse(n, row, col)
        egc = jnp.exp(gcum)
        rhs = jnp.concatenate([vc * bcol, kb * egc], axis=-1)
        sol = _dot(inv.astype(BF16), rhs.astype(BF16))
        u, w = sol[:, :HEAD_DIM], sol[:, HEAD_DIM:]
        state = s_ref[...]
        sb = state.astype(BF16)
        v_new = u - _dot(w.astype(BF16), sb)
        attn = jnp.where(lower, _dot_nt(qc.astype(BF16), kcb) * decay, 0.0)
        o = _dot((qc * egc).astype(BF16), sb) + _dot(attn.astype(BF16), v_new.astype(BF16))
        g_last = gcum[c - 1:c, :]
        s_ref[...] = state * jnp.exp(g_last) + _dot_tn((kc * jnp.exp(g_last - gcum)).astype(BF16),
                                                         v_new.astype(BF16))
        gated = _rmsnorm_rows(o, nw_ref[...]) * _silu(z_ref[rows, :])
        o_ref[rows, :] = gated.astype(o_ref.dtype)
        return 0

    lax.fori_loop(0, seq // c, chunk, 0)


def _gdn_core_prompt(q, k, v, z, beta, g, nw):
    bsz, seq, vw = v.shape
    n_vheads = vw // HEAD_DIM
    rep = vw // q.shape[-1]
    head = lambda div: pl.BlockSpec((None, seq, HEAD_DIM), lambda b, h: (b, 0, h // div))
    gates = pl.BlockSpec((None, seq, n_vheads), lambda b, h: (b, 0, 0))
    return pl.pallas_call(
        functools.partial(_gdn_core_kernel, seq=seq),
        grid=(bsz, n_vheads),
        in_specs=[head(rep), head(rep), head(1), head(1), gates, gates, _resident((1, HEAD_DIM))],
        out_specs=[head(1), pl.BlockSpec((None, None, HEAD_DIM, HEAD_DIM), lambda b, h: (b, h, 0, 0))],
        out_shape=[jax.ShapeDtypeStruct((bsz, seq, vw), BF16),
                   jax.ShapeDtypeStruct((bsz, n_vheads, HEAD_DIM, HEAD_DIM), F32)],
        compiler_params=_cparams("arbitrary", "arbitrary"),
        name="gdn_core_prompt",
    )(q, k, v, z, beta, g, nw)


def _gdn_core_decode_kernel(q_ref, k_ref, v_ref, z_ref, beta_ref, g_ref, nw_ref, s_ref, o_ref, so_ref,
                            *, n_vheads, rep):
    q_scale = HEAD_DIM ** -0.5
    row = _iota2((8, HEAD_DIM), 0)

    def rows3(a, b, c):
        return jnp.where(row == 0, a, jnp.where(row == 1, b, jnp.where(row == 2, c, 0.0)))

    for h in range(n_vheads):
        hk = h // rep
        lanes = slice(h * HEAD_DIM, (h + 1) * HEAD_DIM)
        klanes = slice(hk * HEAD_DIM, (hk + 1) * HEAD_DIM)
        qh = q_ref[:, klanes] * q_scale
        kh = k_ref[:, klanes]
        vh = v_ref[:, lanes]
        beta = beta_ref[:, h:h + 1]
        eg = jnp.exp(g_ref[:, h:h + 1])
        state = s_ref[h]
        w = kh * beta * eg
        lhs = rows3(w, qh * eg, 0.0)
        res = _dot(lhs.astype(BF16), state.astype(BF16))
        v_new = vh * beta - res[0:1, :]
        attn = jnp.sum(qh * kh, axis=-1, keepdims=True)
        o = res[1:2, :] + attn * v_new
        k_hi = kh.astype(BF16).astype(F32)
        v_hi = v_new.astype(BF16).astype(F32)
        k_rows = rows3(k_hi, kh - k_hi, k_hi)
        v_rows = rows3(v_hi, v_hi, v_new - v_hi)
        so_ref[h] = state * eg + _dot_tn(k_rows.astype(BF16), v_rows.astype(BF16))
        gated = _rmsnorm_rows(o, nw_ref[...]) * _silu(z_ref[:, lanes])
        o_ref[:, lanes] = gated.astype(o_ref.dtype)


def _gdn_core_decode(q, k, v, z, beta, g, nw, state):
    nseq, _, vw = v.shape
    n_vheads = vw // HEAD_DIM
    rep = vw // q.shape[-1]
    row = lambda width: pl.BlockSpec((None, 1, width), lambda s: (s, 0, 0))
    st = pl.BlockSpec((None, n_vheads, HEAD_DIM, HEAD_DIM), lambda s: (s, 0, 0, 0))
    return pl.pallas_call(
        functools.partial(_gdn_core_decode_kernel, n_vheads=n_vheads, rep=rep),
        grid=(nseq,),
        in_specs=[row(q.shape[-1]), row(q.shape[-1]), row(vw), row(vw), row(n_vheads), row(n_vheads),
                  _resident((1, HEAD_DIM)), st],
        out_specs=[row(vw), st],
        out_shape=[jax.ShapeDtypeStruct((nseq, 1, vw), BF16), jax.ShapeDtypeStruct(state.shape, F32)],
        compiler_params=_cparams("arbitrary"),
        name="gdn_core_decode",
    )(q, k, v, z, beta, g, nw, state)


def _out_ffn_kernel(x_ref, o_ref, wo_ref, nw_ref, w1_ref, cw_ref, cb_ref, w2_ref, buf_ref,
                    y_ref, cs_ref, gbuf, ubuf, carry, acc, *, decode, tm, d_ff):
    x1 = x_ref[...] + _dot(o_ref[...], wo_ref[...])
    hb = _rmsnorm_rows(x1, nw_ref[...]).astype(BF16)

    if not decode:
        @pl.when(pl.program_id(1) == 0)
        def _():
            carry[...] = jnp.zeros_like(carry)
            carry[6:8, :] = buf_ref[...]

    def conv(u, scratch, c0, wc):
        cols = slice(c0, c0 + wc)
        if decode:
            width = 2 * d_ff
            b0 = buf_ref[:, c0:c0 + wc]
            b1 = buf_ref[:, width + c0:width + c0 + wc]
            cs_ref[:, c0:c0 + wc] = b1
            cs_ref[:, width + c0:width + c0 + wc] = u
            y = cw_ref[0:1, cols] * b0 + cw_ref[1:2, cols] * b1 + cw_ref[2:3, cols] * u
        else:
            scratch[0:8, :] = carry[:, cols]
            scratch[8:8 + tm, :] = u
            y = (cw_ref[2:3, cols] * u + cw_ref[1:2, cols] * scratch[7:7 + tm, :]
                 + cw_ref[0:1, cols] * scratch[6:6 + tm, :])
            carry[:, cols] = scratch[tm:tm + 8, :]
        return y + cb_ref[:, cols]

    wc = FFN_COLS
    for j in range(d_ff // wc):
        c0 = j * wc
        gate = conv(_dot(hb, w1_ref[:, c0:c0 + wc]), gbuf, c0, wc)
        up = conv(_dot(hb, w1_ref[:, d_ff + c0:d_ff + c0 + wc]), ubuf, d_ff + c0, wc)
        act = (_silu(gate) * up).astype(BF16)
        part = _dot(act, w2_ref[c0:c0 + wc, :])
        if j == 0:
            acc[...] = part
        else:
            acc[...] += part
    if not decode:
        cs_ref[...] = carry[6:8, :]
    y_ref[...] = x1 + acc[...]


def _out_ffn(x, o, wo, nw, w1, cw, cb, w2, buf, *, decode):
    bsz, seq, dm = x.shape
    d_ff = w2.shape[0]
    tm = min(ROW_TILE, seq)
    row = lambda width: pl.BlockSpec((None, tm, width), lambda b, i: (b, i, 0))
    if decode:
        buf_spec = row(4 * d_ff)
        cs_shape, cs_spec = (bsz, seq, 4 * d_ff), row(4 * d_ff)
    else:
        buf_spec = pl.BlockSpec((None, 2, 2 * d_ff), lambda b, i: (b, 0, 0))
        cs_shape, cs_spec = (bsz, 2, 2 * d_ff), pl.BlockSpec((None, 2, 2 * d_ff), lambda b, i: (b, 0, 0))
    return pl.pallas_call(
        functools.partial(_out_ffn_kernel, decode=decode, tm=tm, d_ff=d_ff),
        grid=(bsz, seq // tm),
        in_specs=[row(dm), row(o.shape[-1]), _resident(wo.shape), _resident((1, dm)), _resident(w1.shape),
                  _resident(cw.shape), _resident((1, 2 * d_ff)), _resident(w2.shape), buf_spec],
        out_specs=[row(dm), cs_spec],
        out_shape=[jax.ShapeDtypeStruct((bsz, seq, dm), F32), jax.ShapeDtypeStruct(cs_shape, F32)],
        scratch_shapes=[pltpu.VMEM((tm + 8, FFN_COLS), F32), pltpu.VMEM((tm + 8, FFN_COLS), F32),
                        pltpu.VMEM((8, 2 * d_ff), F32), pltpu.VMEM((tm, dm), F32)],
        compiler_params=_cparams("arbitrary", "arbitrary"),
        name="out_ffn_decode" if decode else "out_ffn_prompt",
    )(x, o, wo, nw, w1, cw, cb, w2, buf)


def _fox_in_kernel(x_ref, nw_ref, wq_ref, wk_ref, wv_ref, wg_ref, wf_ref, wft_ref, bf_ref, bft_ref, qn_ref, kn_ref,
                   q_ref, k_ref, v_ref, og_ref, lf_ref, cum_ref, carry, *, tm, n_heads):
    hb = _rmsnorm_rows(x_ref[...], nw_ref[...]).astype(BF16)
    q = _dot(hb, wq_ref[...])
    k = _dot(hb, wk_ref[...])
    for h in range(n_heads):
        lanes = slice(h * HEAD_DIM, (h + 1) * HEAD_DIM)
        q_ref[:, lanes] = _rmsnorm_rows(q[:, lanes], qn_ref[...])
        k_ref[:, lanes] = _rmsnorm_rows(k[:, lanes], kn_ref[...])
    v_ref[...] = _dot(hb, wv_ref[...])
    og_ref[...] = _dot(hb, wg_ref[...])
    lf_ref[...] = -_softplus(-(_dot(hb, wf_ref[...]) + bf_ref[...]))

    @pl.when(pl.program_id(1) == 0)
    def _():
        carry[...] = jnp.zeros_like(carry)

    lf_t = -_softplus(-(_dot_nt(wft_ref[...], hb) + bft_ref[...]))
    upper = jnp.where(_iota2((tm, tm), 0) <= _iota2((tm, tm), 1), 1.0, 0.0).astype(BF16)
    cum = _dot_sel_right(lf_t, upper) + carry[:, 0:1]
    cum_ref[...] = cum
    carry[...] = jnp.broadcast_to(cum[:, tm - 1:tm], carry.shape)


def _fox_in(x, nw, wq, wk, wv, wg, wf, wft, bf, bft, qn, kn):
    bsz, seq, dm = x.shape
    width = wq.shape[1]
    n_heads = width // HEAD_DIM
    tm = min(ROW_TILE, seq)
    row = lambda w: pl.BlockSpec((None, tm, w), lambda b, i: (b, i, 0))
    wide = jax.ShapeDtypeStruct((bsz, seq, width), F32)
    return pl.pallas_call(
        functools.partial(_fox_in_kernel, tm=tm, n_heads=n_heads),
        grid=(bsz, seq // tm),
        in_specs=[row(dm), _resident((1, dm)), _resident(wq.shape), _resident(wk.shape), _resident(wv.shape),
                  _resident(wg.shape), _resident(wf.shape), _resident(wft.shape), _resident((1, n_heads)),
                  _resident((n_heads, 1)), _resident((1, HEAD_DIM)), _resident((1, HEAD_DIM))],
        out_specs=[row(width), row(width), row(width), row(width), row(n_heads),
                   pl.BlockSpec((None, n_heads, tm), lambda b, i: (b, 0, i))],
        out_shape=[wide, wide, wide, wide, jax.ShapeDtypeStruct((bsz, seq, n_heads), F32),
                   jax.ShapeDtypeStruct((bsz, n_heads, seq), F32)],
        scratch_shapes=[pltpu.VMEM((n_heads, HEAD_DIM), F32)],
        compiler_params=_cparams("arbitrary", "arbitrary"),
        name="fox_in",
    )(x, nw, wq, wk, wv, wg, wf, wft, bf, bft, qn, kn)


def _fox_attn_kernel(q_ref, k_ref, v_ref, cum_ref, og_ref, o_ref, m_scr, l_scr, acc_scr, *, tq, tk):
    h = pl.program_id(1)
    qi = pl.program_id(2)
    ki = pl.program_id(3)

    @pl.when(ki == 0)
    def _():
        m_scr[...] = jnp.full_like(m_scr, -jnp.inf)
        l_scr[...] = jnp.zeros_like(l_scr)
        acc_scr[...] = jnp.zeros_like(acc_scr)

    @pl.when(ki <= qi)
    def _():
        s = _dot_nt(q_ref[...].astype(BF16), k_ref[...].astype(BF16)) * (HEAD_DIM ** -0.5)
        s = s - cum_ref[pl.ds(h, 1), :]
        q_pos = qi * tq + _iota2((tq, tk), 0)
        k_pos = ki * tk + _iota2((tq, tk), 1)
        s = jnp.where(k_pos <= q_pos, s, -jnp.inf)
        m_prev = m_scr[:, 0:1]
        m_new = jnp.maximum(m_prev, jnp.max(s, axis=-1, keepdims=True))
        alpha = jnp.exp(m_prev - m_new)
        p = jnp.exp(s - m_new)
        l_scr[...] = jnp.broadcast_to(alpha * l_scr[:, 0:1] + jnp.sum(p, axis=-1, keepdims=True), l_scr.shape)
        acc_scr[...] = alpha * acc_scr[...] + _dot(p.astype(BF16), v_ref[...].astype(BF16))
        m_scr[...] = jnp.broadcast_to(m_new, m_scr.shape)

    @pl.when(ki == pl.num_programs(3) - 1)
    def _():
        o = acc_scr[...] / l_scr[:, 0:1] * _sigmoid(og_ref[...])
        o_ref[...] = o.astype(o_ref.dtype)


def _fox_attn_prompt(q, k, v, cum, og):
    bsz, seq, width = q.shape
    n_heads = width // HEAD_DIM
    tq = tk = min(ATTN_TILE, seq)
    q_spec = pl.BlockSpec((None, tq, HEAD_DIM), lambda b, h, i, j: (b, i, h))
    kv_spec = pl.BlockSpec((None, tk, HEAD_DIM), lambda b, h, i, j: (b, jnp.minimum(i, j), h))
    cum_spec = pl.BlockSpec((None, n_heads, tk), lambda b, h, i, j: (b, 0, jnp.minimum(i, j)))
    return pl.pallas_call(
        functools.partial(_fox_attn_kernel, tq=tq, tk=tk),
        grid=(bsz, n_heads, seq // tq, seq // tk),
        in_specs=[q_spec, kv_spec, kv_spec, cum_spec, q_spec],
        out_specs=q_spec,
        out_shape=jax.ShapeDtypeStruct((bsz, seq, width), BF16),
        scratch_shapes=[pltpu.VMEM((tq, HEAD_DIM), F32), pltpu.VMEM((tq, HEAD_DIM), F32),
                        pltpu.VMEM((tq, HEAD_DIM), F32)],
        compiler_params=_cparams("arbitrary", "arbitrary", "arbitrary", "arbitrary"),
        name="fox_attn_prompt",
    )(q, k, v, cum, og)


def _fox_attn_decode_kernel(pt_ref, q_ref, kn_ref, vn_ref, lfn_ref, og_ref, kp_ref, vp_ref, lfp_ref, o_ref,
                            m_scr, l_scr, acc_scr, cum_scr, *, n_heads, page):
    del pt_ref
    pi = pl.program_id(1)
    flat = n_heads * page
    scale = HEAD_DIM ** -0.5
    lane = _iota2((n_heads, flat), 1)
    own_head = (lane & (n_heads - 1)) == _iota2((n_heads, flat), 0)

    @pl.when(pi == 0)
    def _():
        m_scr[...] = jnp.full_like(m_scr, -jnp.inf)
        l_scr[...] = jnp.zeros_like(l_scr)
        acc_scr[...] = jnp.zeros_like(acc_scr)
        cum_scr[...] = jnp.zeros_like(cum_scr)

    lf_page = jnp.broadcast_to(lfp_ref[...], (n_heads, flat))
    cum = lf_page
    shift = n_heads
    while shift < flat:
        cum = cum + jnp.where(lane >= shift, pltpu.roll(cum, shift, axis=1), 0.0)
        shift *= 2
    cum = cum + cum_scr[:, 0:1]
    cum_total = cum_scr[:, 0:1] + jnp.sum(jnp.where(own_head, lf_page, 0.0), axis=-1, keepdims=True)
    cum_scr[...] = jnp.broadcast_to(cum_total, cum_scr.shape)

    q = q_ref[...]
    s = _dot_nt(q.astype(BF16), kp_ref[...].astype(BF16)) * scale - cum
    s = jnp.where(own_head, s, -jnp.inf)
    m_prev = m_scr[:, 0:1]
    m_new = jnp.maximum(m_prev, jnp.max(s, axis=-1, keepdims=True))
    alpha = jnp.exp(m_prev - m_new)
    p = jnp.exp(s - m_new)
    l_new = alpha * l_scr[:, 0:1] + jnp.sum(p, axis=-1, keepdims=True)
    acc = alpha * acc_scr[...] + _dot(p.astype(BF16), vp_ref[...].astype(BF16))
    m_scr[...] = jnp.broadcast_to(m_new, m_scr.shape)
    l_scr[...] = jnp.broadcast_to(l_new, l_scr.shape)
    acc_scr[...] = acc

    @pl.when(pi == pl.num_programs(1) - 1)
    def _():
        eye = _iota2((n_heads, n_heads), 0) == _iota2((n_heads, n_heads), 1)
        lf_new = jnp.sum(jnp.where(eye, jnp.broadcast_to(lfn_ref[...], (n_heads, n_heads)), 0.0),
                         axis=-1, keepdims=True)
        s_new = jnp.sum(q * kn_ref[...], axis=-1, keepdims=True) * scale - (cum_total + lf_new)
        m_fin = jnp.maximum(m_new, s_new)
        a = jnp.exp(m_new - m_fin)
        p_new = jnp.exp(s_new - m_fin)
        o = (a * acc + p_new * vn_ref[...]) / (a * l_new + p_new)
        o_ref[...] = (o * _sigmoid(og_ref[...])).astype(o_ref.dtype)


def _fox_attn_decode(page_table, q, k_new, v_new, lf_new, og, k_pool, v_pool, lf_pool):
    nseq, n_heads, _ = q.shape
    n_pages = page_table.shape[1]
    flat = k_pool.shape[1]
    page = flat // n_heads
    tok = pl.BlockSpec((None, n_heads, HEAD_DIM), lambda s, p, pt: (s, 0, 0))
    pool = pl.BlockSpec((None, flat, HEAD_DIM), lambda s, p, pt: (pt[s, p], 0, 0))
    grid_spec = pltpu.PrefetchScalarGridSpec(
        num_scalar_prefetch=1,
        grid=(nseq, n_pages),
        in_specs=[tok, tok, tok, pl.BlockSpec((None, 1, n_heads), lambda s, p, pt: (s, 0, 0)), tok, pool, pool,
                  pl.BlockSpec((None, 1, flat), lambda s, p, pt: (pt[s, p], 0, 0))],
        out_specs=tok,
        scratch_shapes=[pltpu.VMEM((n_heads, HEAD_DIM), F32), pltpu.VMEM((n_heads, HEAD_DIM), F32),
                        pltpu.VMEM((n_heads, HEAD_DIM), F32), pltpu.VMEM((n_heads, HEAD_DIM), F32)],
    )
    return pl.pallas_call(
        functools.partial(_fox_attn_decode_kernel, n_heads=n_heads, page=page),
        grid_spec=grid_spec,
        out_shape=jax.ShapeDtypeStruct((nseq, n_heads, HEAD_DIM), BF16),
        compiler_params=_cparams("arbitrary", "arbitrary"),
        name="fox_attn_decode",
    )(page_table, q, k_new, v_new, lf_new, og, k_pool, v_pool, lf_pool)


def _prepare_weights(norm_mix, norm_ffn, gdn_w_in, gdn_conv_w, gdn_a_log, gdn_dt_bias, gdn_norm, gdn_w_out,
                     fox_w_in, fox_b_f, fox_q_norm, fox_k_norm, fox_w_out, ffn_w_in, ffn_conv_w, ffn_conv_b,
                     ffn_w_out):
    conv_dim = gdn_conv_w.shape[-1]
    vw = gdn_w_out.shape[1]
    fw = fox_w_out.shape[1]
    layers = []
    for i in range(norm_mix.shape[0]):
        j = i // 2
        ffn = dict(nw=norm_ffn[i][None], w1=ffn_w_in[i].astype(BF16), cw=ffn_conv_w[i], cb=ffn_conv_b[i][None],
                   w2=ffn_w_out[i].astype(BF16))
        if i % 2 == 0:
            w = gdn_w_in[j].astype(BF16)
            mixer = dict(nw=norm_mix[i][None], wqkv=w[:, :conv_dim], wz=w[:, conv_dim:conv_dim + vw],
                         wba=w[:, conv_dim + vw:], cw=gdn_conv_w[j], alog=gdn_a_log[j][None],
                         dtb=gdn_dt_bias[j][None], onw=gdn_norm[j][None], wo=gdn_w_out[j].astype(BF16))
        else:
            w = fox_w_in[j].astype(BF16)
            mixer = dict(nw=norm_mix[i][None], wq=w[:, :fw], wk=w[:, fw:2 * fw], wv=w[:, 2 * fw:3 * fw],
                         wg=w[:, 3 * fw:4 * fw], wf=w[:, 4 * fw:], wft=w[:, 4 * fw:].T, bf=fox_b_f[j][None],
                         bft=fox_b_f[j][:, None], qn=fox_q_norm[j][None], kn=fox_k_norm[j][None],
                         wo=fox_w_out[j].astype(BF16))
        layers.append((mixer, ffn))
    return layers


def _trunk_prompt(x, layers):
    bsz, seq, _ = x.shape
    gdn_s, gdn_cb, ks, vs, lfs, ffn_cb = [], [], [], [], [], []
    for i, (m, f) in enumerate(layers):
        if i % 2 == 0:
            conv_dim = m["wqkv"].shape[1]
            q, k, v, z, beta, g, cb = _gdn_in(x, jnp.zeros((bsz, 3, conv_dim), F32), m["nw"], m["wqkv"], m["wz"],
                                              m["wba"], m["cw"], m["alog"], m["dtb"], decode=False)
            o, s = _gdn_core_prompt(q, k, v, z, beta, g, m["onw"])
            gdn_cb.append(cb)
            gdn_s.append(s)
        else:
            q, k, v, og, lf, cum = _fox_in(x, m["nw"], m["wq"], m["wk"], m["wv"], m["wg"], m["wf"], m["wft"],
                                           m["bf"], m["bft"], m["qn"], m["kn"])
            o = _fox_attn_prompt(q, k, v, cum, og)
            n_heads = lf.shape[-1]
            ks.append(k.reshape(bsz, seq, n_heads, HEAD_DIM))
            vs.append(v.reshape(bsz, seq, n_heads, HEAD_DIM))
            lfs.append(lf)
        d_ff = f["w2"].shape[0]
        x, fcb = _out_ffn(x, o, m["wo"], f["nw"], f["w1"], f["cw"], f["cb"], f["w2"],
                          jnp.zeros((bsz, 2, 2 * d_ff), F32), decode=False)
        ffn_cb.append(fcb)
    return x, jnp.stack(gdn_s), jnp.stack(gdn_cb), jnp.stack(ks), jnp.stack(vs), jnp.stack(lfs), jnp.stack(ffn_cb)


def _trunk_decode(x, state_gdn, state_gdn_conv, cache_k, cache_v, cache_logf, state_ffn_conv, page_table, layers):
    nseq = x.shape[0]
    x = x.reshape(1, nseq, x.shape[-1])
    gdn_s, gdn_cb, ks, vs, lfs, ffn_cb = [], [], [], [], [], []
    for i, (m, f) in enumerate(layers):
        j = i // 2
        if i % 2 == 0:
            q, k, v, z, beta, g, cb = _gdn_in(x, state_gdn_conv[j].reshape(1, nseq, -1), m["nw"], m["wqkv"], m["wz"],
                                              m["wba"], m["cw"], m["alog"], m["dtb"], decode=True)
            per_seq = lambda a: a.reshape(nseq, 1, a.shape[-1])
            o, s = _gdn_core_decode(per_seq(q), per_seq(k), per_seq(v), per_seq(z), per_seq(beta), per_seq(g),
                                    m["onw"], state_gdn[j])
            o = o.reshape(1, nseq, -1)
            gdn_cb.append(cb.reshape(state_gdn_conv[j].shape))
            gdn_s.append(s)
        else:
            q, k, v, og, lf, _ = _fox_in(x, m["nw"], m["wq"], m["wk"], m["wv"], m["wg"], m["wf"], m["wft"],
                                         m["bf"], m["bft"], m["qn"], m["kn"])
            n_heads = lf.shape[-1]
            n_pool, page = cache_k.shape[1], cache_k.shape[2]
            heads = lambda a: a.reshape(nseq, n_heads, HEAD_DIM)
            o = _fox_attn_decode(page_table, heads(q), heads(k), heads(v), lf.reshape(nseq, 1, n_heads), heads(og),
                                 cache_k[j].reshape(n_pool, page * n_heads, HEAD_DIM),
                                 cache_v[j].reshape(n_pool, page * n_heads, HEAD_DIM),
                                 cache_logf[j].reshape(n_pool, 1, page * n_heads))
            o = o.reshape(1, nseq, -1)
            ks.append(k.reshape(nseq, 1, n_heads, HEAD_DIM))
            vs.append(v.reshape(nseq, 1, n_heads, HEAD_DIM))
            lfs.append(lf.reshape(nseq, 1, n_heads))
        x, fcb = _out_ffn(x, o, m["wo"], f["nw"], f["w1"], f["cw"], f["cb"], f["w2"],
                          state_ffn_conv[i].reshape(1, nseq, -1), decode=True)
        ffn_cb.append(fcb.reshape(state_ffn_conv[i].shape))
    return (x.reshape(nseq, 1, -1), jnp.stack(gdn_s), jnp.stack(gdn_cb), jnp.stack(ks), jnp.stack(vs),
            jnp.stack(lfs), jnp.stack(ffn_cb))


def kernel(x_prompt, x_sample, state_gdn, state_gdn_conv, cache_k, cache_v, cache_logf, state_ffn_conv, page_table,
           norm_mix, norm_ffn, gdn_w_in, gdn_conv_w, gdn_a_log, gdn_dt_bias, gdn_norm, gdn_w_out, fox_w_in, fox_b_f,
           fox_q_norm, fox_k_norm, fox_w_out, ffn_w_in, ffn_conv_w, ffn_conv_b, ffn_w_out):
    layers = _prepare_weights(norm_mix, norm_ffn, gdn_w_in, gdn_conv_w, gdn_a_log, gdn_dt_bias, gdn_norm, gdn_w_out,
                              fox_w_in, fox_b_f, fox_q_norm, fox_k_norm, fox_w_out, ffn_w_in, ffn_conv_w, ffn_conv_b,
                              ffn_w_out)
    prompt = _trunk_prompt(x_prompt, layers)
    sample = _trunk_decode(x_sample, state_gdn, state_gdn_conv, cache_k, cache_v, cache_logf, state_ffn_conv,
                           page_table, layers)
    return (prompt[0], sample[0]) + prompt[1:] + sample[1:]
```

```python
import functools

import jax
import jax.numpy as jnp
from jax import lax
from jax.experimental import pallas as pl
from jax.experimental.pallas import tpu as pltpu

F32 = jnp.float32
BF16 = jnp.bfloat16

HEAD_DIM = 128
GDN_CHUNK = 64
EPS = 1e-6
V7X_VMEM_LIMIT_BYTES = 56 * 1024 * 1024
ROW_TILE = 512
ATTN_TILE = 512
CONV_COLS = 512
FFN_COLS = 256
GDN_HEADS_PER_STEP = 4
GDN_SEQ_TILE = 1024
GDN_SOLVE_BATCH = 8


def _cparams(*semantics):
    return pltpu.CompilerParams(dimension_semantics=semantics, vmem_limit_bytes=V7X_VMEM_LIMIT_BYTES)


def _resident(shape):
    return pl.BlockSpec(shape, lambda *_: (0,) * len(shape), pipeline_mode=pl.Buffered(1))


def _sigmoid(x):
    return 1.0 / (1.0 + jnp.exp(-x))


def _silu(x):
    return x * _sigmoid(x)


def _softplus(x):
    return jnp.maximum(x, 0.0) + jnp.log(1.0 + jnp.exp(-jnp.abs(x)))


def _rmsnorm_rows(x, w):
    return x * lax.rsqrt(jnp.mean(x * x, axis=-1, keepdims=True) + EPS) * w


def _dot(a, b):
    return jnp.dot(a, b, preferred_element_type=F32)


def _dot_nt(a, b):
    return lax.dot_general(a, b, (((1,), (1,)), ((), ())), preferred_element_type=F32)


def _dot_tn(a, b):
    return lax.dot_general(a, b, (((0,), (0,)), ((), ())), preferred_element_type=F32)


def _split3(x):
    hi = x.astype(BF16)
    r = x - hi.astype(F32)
    mid = r.astype(BF16)
    lo = (r - mid.astype(F32)).astype(BF16)
    return hi, mid, lo


def _dot_sel_left(sel, x):
    hi, mid, lo = _split3(x)
    return _dot(sel, hi) + _dot(sel, mid) + _dot(sel, lo)


def _dot_sel_right(x, sel):
    hi, mid, lo = _split3(x)
    return _dot(hi, sel) + _dot(mid, sel) + _dot(lo, sel)


def _dot_f32(a, b):
    return jnp.dot(a, b, preferred_element_type=F32, precision=lax.Precision.HIGHEST)


def _iota2(shape, dim):
    return lax.broadcasted_iota(jnp.int32, shape, dim)


def _gdn_in_kernel(x_ref, nw_ref, wqkv_ref, wz_ref, wba_ref, cw_ref, alog_ref, dtb_ref, buf_ref,
                   q_ref, k_ref, v_ref, z_ref, beta_ref, g_ref, cs_ref, ubuf, carry,
                   *, decode, tm, qk_width, conv_dim, n_vheads):
    hb = _rmsnorm_rows(x_ref[...], nw_ref[...]).astype(BF16)

    if not decode:
        @pl.when(pl.program_id(1) == 0)
        def _():
            carry[...] = jnp.zeros_like(carry)
            carry[5:8, :] = buf_ref[...]

    wc = CONV_COLS
    for j in range(conv_dim // wc):
        c0 = j * wc
        cols = slice(c0, c0 + wc)
        u = _dot(hb, wqkv_ref[:, cols])
        if decode:
            b0 = buf_ref[:, c0:c0 + wc]
            b1 = buf_ref[:, conv_dim + c0:conv_dim + c0 + wc]
            b2 = buf_ref[:, 2 * conv_dim + c0:2 * conv_dim + c0 + wc]
            y = cw_ref[0:1, cols] * b0 + cw_ref[1:2, cols] * b1 + cw_ref[2:3, cols] * b2 + cw_ref[3:4, cols] * u
            cs_ref[:, c0:c0 + wc] = b1
            cs_ref[:, conv_dim + c0:conv_dim + c0 + wc] = b2
            cs_ref[:, 2 * conv_dim + c0:2 * conv_dim + c0 + wc] = u
        else:
            ubuf[0:8, :] = carry[:, cols]
            ubuf[8:8 + tm, :] = u
            y = (cw_ref[3:4, cols] * u + cw_ref[2:3, cols] * ubuf[7:7 + tm, :]
                 + cw_ref[1:2, cols] * ubuf[6:6 + tm, :] + cw_ref[0:1, cols] * ubuf[5:5 + tm, :])
            carry[:, cols] = ubuf[tm:tm + 8, :]
        y = _silu(y)
        for hh in range(wc // HEAD_DIM):
            yh = y[:, hh * HEAD_DIM:(hh + 1) * HEAD_DIM]
            col = c0 + hh * HEAD_DIM
            if col < 2 * qk_width:
                yh = yh * lax.rsqrt(jnp.sum(yh * yh, axis=-1, keepdims=True) + EPS)
            dst, col = ((q_ref, col) if col < qk_width else
                        (k_ref, col - qk_width) if col < 2 * qk_width else (v_ref, col - 2 * qk_width))
            if decode:
                dst[:, col:col + HEAD_DIM] = yh
            else:
                dst[col // HEAD_DIM] = yh
    if not decode:
        cs_ref[...] = carry[5:8, :]

    z = _dot(hb, wz_ref[...])
    if decode:
        z_ref[...] = z
    else:
        for h in range(n_vheads):
            z_ref[h] = z[:, h * HEAD_DIM:(h + 1) * HEAD_DIM]
    ba = _dot(hb, wba_ref[...])
    beta_ref[...] = _sigmoid(ba[:, :n_vheads])
    g = -jnp.exp(alog_ref[...]) * _softplus(ba[:, n_vheads:] + dtb_ref[...])
    if decode:
        g_ref[...] = g
    else:
        r, c = _iota2((tm, tm), 0), _iota2((tm, tm), 1)
        same_chunk_lower = jnp.where((r // GDN_CHUNK == c // GDN_CHUNK) & (c <= r), 1.0, 0.0).astype(BF16)
        g_ref[...] = _dot_sel_left(same_chunk_lower, g)


def _gdn_in(x, buf, nw, wqkv, wz, wba, cw, alog, dtb, *, decode):
    bsz, seq, dm = x.shape
    conv_dim = wqkv.shape[1]
    vw = wz.shape[1]
    qk_width = (conv_dim - vw) // 2
    n_vheads = vw // HEAD_DIM
    tm = min(ROW_TILE, seq)
    grid = (bsz, seq // tm)
    row = lambda width: pl.BlockSpec((None, tm, width), lambda b, i: (b, i, 0))
    if decode:
        buf_spec = pl.BlockSpec((None, tm, 3 * conv_dim), lambda b, i: (b, i, 0))
        cs_shape, cs_spec = (bsz, seq, 3 * conv_dim), pl.BlockSpec((None, tm, 3 * conv_dim), lambda b, i: (b, i, 0))
    else:
        buf_spec = pl.BlockSpec((None, 3, conv_dim), lambda b, i: (b, 0, 0))
        cs_shape, cs_spec = (bsz, 3, conv_dim), pl.BlockSpec((None, 3, conv_dim), lambda b, i: (b, 0, 0))
    kern = functools.partial(_gdn_in_kernel, decode=decode, tm=tm, qk_width=qk_width, conv_dim=conv_dim,
                             n_vheads=n_vheads)
    if decode:
        wide = lambda width: (row(width), jax.ShapeDtypeStruct((bsz, seq, width), F32))
    else:
        wide = lambda width: (pl.BlockSpec((None, width // HEAD_DIM, tm, HEAD_DIM), lambda b, i: (b, 0, i, 0)),
                              jax.ShapeDtypeStruct((bsz, width // HEAD_DIM, seq, HEAD_DIM), F32))
    (q_spec, q_shape), (v_spec, v_shape) = wide(qk_width), wide(vw)
    gate_shape = jax.ShapeDtypeStruct((bsz, seq, n_vheads), F32)
    return pl.pallas_call(
        kern,
        grid=grid,
        in_specs=[row(dm), _resident((1, dm)), _resident(wqkv.shape), _resident(wz.shape), _resident(wba.shape),
                  _resident(cw.shape), _resident((1, n_vheads)), _resident((1, n_vheads)), buf_spec],
        out_specs=[q_spec, q_spec, v_spec, v_spec, row(n_vheads), row(n_vheads), cs_spec],
        out_shape=[q_shape, q_shape, v_shape, v_shape, gate_shape, gate_shape, jax.ShapeDtypeStruct(cs_shape, F32)],
        scratch_shapes=[pltpu.VMEM((tm + 8, CONV_COLS), F32), pltpu.VMEM((8, conv_dim), F32)],
        compiler_params=_cparams("arbitrary", "arbitrary"),
        name="gdn_in_decode" if decode else "gdn_in_prompt",
    )(x, nw, wqkv, wz, wba, cw, alog, dtb, buf)


def _bdot(a, b):
    return lax.dot_general(a, b, (((2,), (1,)), ((0,), (0,))), preferred_element_type=F32)


def _bdot_nt(a, b):
    return lax.dot_general(a, b, (((2,), (2,)), ((0,), (0,))), preferred_element_type=F32)


def _unit_lower_inverse(n, row, col):
    eye = jnp.where(row == col, 1.0, 0.0)
    inv = eye - jnp.where((row >> 1) == (col >> 1), n, 0.0)
    size = n.shape[-1]
    bits = 2
    while (1 << bits) <= size:
        off_diag = jnp.where((row >> bits) == (col >> bits),
                             jnp.where((row >> (bits - 1)) == (col >> (bits - 1)), 0.0, n), 0.0)
        inv_b = inv.astype(BF16)
        inv = inv - _bdot(inv_b, _bdot(off_diag.astype(BF16), inv_b).astype(BF16))
        bits += 1
    n_hi, n_mid, _ = _split3(n)
    x_hi, x_mid, _ = _split3(inv)
    resid = eye - inv - (_bdot(n_hi, x_hi) + _bdot(n_mid, x_hi) + _bdot(n_hi, x_mid))
    return inv + _bdot(x_hi, resid.astype(BF16))


def _gdn_core_kernel(q_ref, k_ref, v_ref, z_ref, beta_ref, gcum_ref, nw_ref, o_ref, s_out_ref,
                     s_scr, u_scr, w_scr, att_scr, qg_scr, kd_scr, egl_scr, gb_scr, *, lt, hb, rep, nb):
    c = GDN_CHUNK
    groups = lt // c // nb
    t = pl.program_id(2)
    head0 = pl.program_id(1) * hb
    q_scale = HEAD_DIM ** -0.5

    @pl.when(t == 0)
    def _():
        s_scr[...] = jnp.zeros_like(s_scr)

    row = lax.broadcasted_iota(jnp.int32, (nb, c, c), 1)
    col = lax.broadcasted_iota(jnp.int32, (nb, c, c), 2)
    lower = row >= col
    strict = row > col
    lane = _iota2((nb * c, HEAD_DIM), 1)
    gate_lane = _iota2((nb * c, beta_ref.shape[-1]), 1)
    pick = jnp.where(lane < 3, 1.0, 0.0).astype(BF16).reshape(nb, c, HEAD_DIM)

    def solve(idx, _):
        hh = idx // groups
        rows = pl.ds(pl.multiple_of((idx - hh * groups) * (nb * c), nb * c), nb * c)
        kh = hh // rep
        own = gate_lane == head0 + hh
        bcol = jnp.sum(jnp.where(own, beta_ref[rows, :], 0.0), axis=-1, keepdims=True)
        gcol = jnp.sum(jnp.where(own, gcum_ref[rows, :], 0.0), axis=-1, keepdims=True)
        g_hi = gcol.astype(BF16).astype(F32)
        g_mid = (gcol - g_hi).astype(BF16).astype(F32)
        g_lo = gcol - g_hi - g_mid
        pieces = jnp.where(lane == 0, g_hi, jnp.where(lane == 1, g_mid, jnp.where(lane == 2, g_lo, 0.0)))
        gcum_t = _bdot_nt(pick, pieces.astype(BF16).reshape(nb, c, HEAD_DIM))
        g3 = gcol.reshape(nb, c, 1)
        b3 = bcol.reshape(nb, c, 1)
        decay = jnp.where(lower, jnp.exp(jnp.minimum(g3 - gcum_t, 0.0)), 0.0)
        q3 = (q_ref[kh, rows, :] * q_scale).reshape(nb, c, HEAD_DIM)
        k3 = k_ref[kh, rows, :].reshape(nb, c, HEAD_DIM)
        v3 = v_ref[hh, rows, :].reshape(nb, c, HEAD_DIM)
        kb = k3 * b3
        k_bf = k3.astype(BF16)
        n = jnp.where(strict, _bdot_nt(kb.astype(BF16), k_bf) * decay, 0.0)
        inv = _unit_lower_inverse(n, row, col)
        eg = jnp.exp(g3)
        rhs = jnp.concatenate([v3 * b3, kb * eg], axis=-1).astype(BF16)
        sol = _bdot(inv.astype(BF16), rhs)
        attn = jnp.where(lower, _bdot_nt(q3.astype(BF16), k_bf) * decay, 0.0)
        g_last = g3[:, c - 1:c, :]
        u_scr[hh, rows, :] = sol[:, :, :HEAD_DIM].reshape(nb * c, HEAD_DIM)
        w_scr[hh, rows, :] = sol[:, :, HEAD_DIM:].reshape(nb * c, HEAD_DIM).astype(BF16)
        att_scr[hh, rows, :] = attn.reshape(nb * c, c).astype(BF16)
        qg_scr[hh, rows, :] = (q3 * eg).reshape(nb * c, HEAD_DIM).astype(BF16)
        kd_scr[hh, rows, :] = (k3 * jnp.exp(g_last - g3)).reshape(nb * c, HEAD_DIM).astype(BF16)
        gb_scr[...] = jnp.broadcast_to(gcol, gb_scr.shape)
        chunk0 = pl.multiple_of((idx - hh * groups) * nb, nb)
        egl_scr[hh, pl.ds(chunk0, nb), :] = jnp.exp(gb_scr[pl.ds(c - 1, nb, stride=c), :])
        return 0

    lax.fori_loop(0, hb * groups, solve, 0)

    def step(i, _):
        rows = pl.ds(pl.multiple_of(i * c, c), c)
        for hh in range(hb):
            state = s_scr[hh]
            res = _dot(jnp.concatenate([w_scr[hh, rows, :], qg_scr[hh, rows, :]], axis=0), state.astype(BF16))
            v_new = (u_scr[hh, rows, :] - res[:c]).astype(BF16)
            o = res[c:] + _dot(att_scr[hh, rows, :], v_new)
            s_scr[hh] = state * egl_scr[hh, pl.ds(i, 1), :] + _dot_tn(kd_scr[hh, rows, :], v_new)
            gated = _rmsnorm_rows(o, nw_ref[...]) * _silu(z_ref[hh, rows, :])
            o_ref[rows, hh * HEAD_DIM:(hh + 1) * HEAD_DIM] = gated.astype(o_ref.dtype)
        return 0

    lax.fori_loop(0, lt // c, step, 0)

    @pl.when(t == pl.num_programs(2) - 1)
    def _():
        s_out_ref[...] = s_scr[...]


def _gdn_core_prompt(q, k, v, z, beta, gcum, nw):
    bsz, n_vheads, seq, _ = v.shape
    rep = n_vheads // q.shape[1]
    hb = min(GDN_HEADS_PER_STEP, n_vheads)
    lt = min(GDN_SEQ_TILE, seq)
    nb = min(GDN_SOLVE_BATCH, lt // GDN_CHUNK)
    heads = lambda n: pl.BlockSpec((None, n, lt, HEAD_DIM), lambda b, h, t: (b, h, t, 0))
    gates = pl.BlockSpec((None, lt, n_vheads), lambda b, h, t: (b, t, 0))
    per_head = lambda width, dtype: pltpu.VMEM((hb, lt, width), dtype)
    return pl.pallas_call(
        functools.partial(_gdn_core_kernel, lt=lt, hb=hb, rep=rep, nb=nb),
        grid=(bsz, n_vheads // hb, seq // lt),
        in_specs=[heads(hb // rep), heads(hb // rep), heads(hb), heads(hb), gates, gates, _resident((1, HEAD_DIM))],
        out_specs=[pl.BlockSpec((None, lt, hb * HEAD_DIM), lambda b, h, t: (b, t, h)),
                   pl.BlockSpec((None, hb, HEAD_DIM, HEAD_DIM), lambda b, h, t: (b, h, 0, 0))],
        out_shape=[jax.ShapeDtypeStruct((bsz, seq, n_vheads * HEAD_DIM), BF16),
                   jax.ShapeDtypeStruct((bsz, n_vheads, HEAD_DIM, HEAD_DIM), F32)],
        scratch_shapes=[pltpu.VMEM((hb, HEAD_DIM, HEAD_DIM), F32), per_head(HEAD_DIM, F32), per_head(HEAD_DIM, BF16),
                        per_head(GDN_CHUNK, BF16), per_head(HEAD_DIM, BF16), per_head(HEAD_DIM, BF16),
                        pltpu.VMEM((hb, lt // GDN_CHUNK, HEAD_DIM), F32), pltpu.VMEM((nb * GDN_CHUNK, HEAD_DIM), F32)],
        compiler_params=_cparams("arbitrary", "arbitrary", "arbitrary"),
        name="gdn_core_prompt",
    )(q, k, v, z, beta, gcum, nw)


def _gdn_core_decode_kernel(q_ref, k_ref, v_ref, z_ref, beta_ref, g_ref, nw_ref, s_ref, o_ref, so_ref,
                            *, n_vheads, rep):
    q_scale = HEAD_DIM ** -0.5
    row = _iota2((8, HEAD_DIM), 0)

    def rows3(a, b, c):
        return jnp.where(row == 0, a, jnp.where(row == 1, b, jnp.where(row == 2, c, 0.0)))

    for h in range(n_vheads):
        hk = h // rep
        lanes = slice(h * HEAD_DIM, (h + 1) * HEAD_DIM)
        klanes = slice(hk * HEAD_DIM, (hk + 1) * HEAD_DIM)
        qh = q_ref[:, klanes] * q_scale
        kh = k_ref[:, klanes]
        vh = v_ref[:, lanes]
        beta = beta_ref[:, h:h + 1]
        eg = jnp.exp(g_ref[:, h:h + 1])
        state = s_ref[h]
        w = kh * beta * eg
        lhs = rows3(w, qh * eg, 0.0)
        res = _dot(lhs.astype(BF16), state.astype(BF16))
        v_new = vh * beta - res[0:1, :]
        attn = jnp.sum(qh * kh, axis=-1, keepdims=True)
        o = res[1:2, :] + attn * v_new
        k_hi = kh.astype(BF16).astype(F32)
        v_hi = v_new.astype(BF16).astype(F32)
        k_rows = rows3(k_hi, kh - k_hi, k_hi)
        v_rows = rows3(v_hi, v_hi, v_new - v_hi)
        so_ref[h] = state * eg + _dot_tn(k_rows.astype(BF16), v_rows.astype(BF16))
        gated = _rmsnorm_rows(o, nw_ref[...]) * _silu(z_ref[:, lanes])
        o_ref[:, lanes] = gated.astype(o_ref.dtype)


def _gdn_core_decode(q, k, v, z, beta, g, nw, state):
    nseq, _, vw = v.shape
    n_vheads = vw // HEAD_DIM
    rep = vw // q.shape[-1]
    row = lambda width: pl.BlockSpec((None, 1, width), lambda s: (s, 0, 0))
    st = pl.BlockSpec((None, n_vheads, HEAD_DIM, HEAD_DIM), lambda s: (s, 0, 0, 0))
    return pl.pallas_call(
        functools.partial(_gdn_core_decode_kernel, n_vheads=n_vheads, rep=rep),
        grid=(nseq,),
        in_specs=[row(q.shape[-1]), row(q.shape[-1]), row(vw), row(vw), row(n_vheads), row(n_vheads),
                  _resident((1, HEAD_DIM)), st],
        out_specs=[row(vw), st],
        out_shape=[jax.ShapeDtypeStruct((nseq, 1, vw), BF16), jax.ShapeDtypeStruct(state.shape, F32)],
        compiler_params=_cparams("arbitrary"),
        name="gdn_core_decode",
    )(q, k, v, z, beta, g, nw, state)


def _out_ffn_kernel(x_ref, o_ref, wo_ref, nw_ref, w1_ref, cw_ref, cb_ref, w2_ref, buf_ref,
                    y_ref, cs_ref, gbuf, ubuf, carry, acc, *, decode, tm, d_ff):
    x1 = x_ref[...] + _dot(o_ref[...], wo_ref[...])
    hb = _rmsnorm_rows(x1, nw_ref[...]).astype(BF16)

    if not decode:
        @pl.when(pl.program_id(1) == 0)
        def _():
            carry[...] = jnp.zeros_like(carry)
            carry[6:8, :] = buf_ref[...]

    def conv(u, scratch, c0, wc):
        cols = slice(c0, c0 + wc)
        if decode:
            width = 2 * d_ff
            b0 = buf_ref[:, c0:c0 + wc]
            b1 = buf_ref[:, width + c0:width + c0 + wc]
            cs_ref[:, c0:c0 + wc] = b1
            cs_ref[:, width + c0:width + c0 + wc] = u
            y = cw_ref[0:1, cols] * b0 + cw_ref[1:2, cols] * b1 + cw_ref[2:3, cols] * u
        else:
            scratch[0:8, :] = carry[:, cols]
            scratch[8:8 + tm, :] = u
            y = (cw_ref[2:3, cols] * u + cw_ref[1:2, cols] * scratch[7:7 + tm, :]
                 + cw_ref[0:1, cols] * scratch[6:6 + tm, :])
            carry[:, cols] = scratch[tm:tm + 8, :]
        return y + cb_ref[:, cols]

    wc = FFN_COLS
    for j in range(d_ff // wc):
        c0 = j * wc
        gate = conv(_dot(hb, w1_ref[:, c0:c0 + wc]), gbuf, c0, wc)
        up = conv(_dot(hb, w1_ref[:, d_ff + c0:d_ff + c0 + wc]), ubuf, d_ff + c0, wc)
        act = (_silu(gate) * up).astype(BF16)
        part = _dot(act, w2_ref[c0:c0 + wc, :])
        if j == 0:
            acc[...] = part
        else:
            acc[...] += part
    if not decode:
        cs_ref[...] = carry[6:8, :]
    y_ref[...] = x1 + acc[...]


def _out_ffn(x, o, wo, nw, w1, cw, cb, w2, buf, *, decode):
    bsz, seq, dm = x.shape
    d_ff = w2.shape[0]
    tm = min(ROW_TILE, seq)
    row = lambda width: pl.BlockSpec((None, tm, width), lambda b, i: (b, i, 0))
    if decode:
        buf_spec = row(4 * d_ff)
        cs_shape, cs_spec = (bsz, seq, 4 * d_ff), row(4 * d_ff)
    else:
        buf_spec = pl.BlockSpec((None, 2, 2 * d_ff), lambda b, i: (b, 0, 0))
        cs_shape, cs_spec = (bsz, 2, 2 * d_ff), pl.BlockSpec((None, 2, 2 * d_ff), lambda b, i: (b, 0, 0))
    return pl.pallas_call(
        functools.partial(_out_ffn_kernel, decode=decode, tm=tm, d_ff=d_ff),
        grid=(bsz, seq // tm),
        in_specs=[row(dm), row(o.shape[-1]), _resident(wo.shape), _resident((1, dm)), _resident(w1.shape),
                  _resident(cw.shape), _resident((1, 2 * d_ff)), _resident(w2.shape), buf_spec],
        out_specs=[row(dm), cs_spec],
        out_shape=[jax.ShapeDtypeStruct((bsz, seq, dm), F32), jax.ShapeDtypeStruct(cs_shape, F32)],
        scratch_shapes=[pltpu.VMEM((tm + 8, FFN_COLS), F32), pltpu.VMEM((tm + 8, FFN_COLS), F32),
                        pltpu.VMEM((8, 2 * d_ff), F32), pltpu.VMEM((tm, dm), F32)],
        compiler_params=_cparams("arbitrary", "arbitrary"),
        name="out_ffn_decode" if decode else "out_ffn_prompt",
    )(x, o, wo, nw, w1, cw, cb, w2, buf)


def _fox_in_kernel(x_ref, nw_ref, wq_ref, wk_ref, wv_ref, wg_ref, wf_ref, wft_ref, bf_ref, bft_ref, qn_ref, kn_ref,
                   q_ref, k_ref, v_ref, og_ref, lf_ref, cum_ref, carry, *, tm, n_heads):
    hb = _rmsnorm_rows(x_ref[...], nw_ref[...]).astype(BF16)
    q = _dot(hb, wq_ref[...])
    k = _dot(hb, wk_ref[...])
    for h in range(n_heads):
        lanes = slice(h * HEAD_DIM, (h + 1) * HEAD_DIM)
        q_ref[:, lanes] = _rmsnorm_rows(q[:, lanes], qn_ref[...])
        k_ref[:, lanes] = _rmsnorm_rows(k[:, lanes], kn_ref[...])
    v_ref[...] = _dot(hb, wv_ref[...])
    og_ref[...] = _dot(hb, wg_ref[...])
    lf_ref[...] = -_softplus(-(_dot(hb, wf_ref[...]) + bf_ref[...]))

    @pl.when(pl.program_id(1) == 0)
    def _():
        carry[...] = jnp.zeros_like(carry)

    lf_t = -_softplus(-(_dot_nt(wft_ref[...], hb) + bft_ref[...]))
    upper = jnp.where(_iota2((tm, tm), 0) <= _iota2((tm, tm), 1), 1.0, 0.0).astype(BF16)
    cum = _dot_sel_right(lf_t, upper) + carry[:, 0:1]
    cum_ref[...] = cum
    carry[...] = jnp.broadcast_to(cum[:, tm - 1:tm], carry.shape)


def _fox_in(x, nw, wq, wk, wv, wg, wf, wft, bf, bft, qn, kn):
    bsz, seq, dm = x.shape
    width = wq.shape[1]
    n_heads = width // HEAD_DIM
    tm = min(ROW_TILE, seq)
    row = lambda w: pl.BlockSpec((None, tm, w), lambda b, i: (b, i, 0))
    wide = jax.ShapeDtypeStruct((bsz, seq, width), F32)
    return pl.pallas_call(
        functools.partial(_fox_in_kernel, tm=tm, n_heads=n_heads),
        grid=(bsz, seq // tm),
        in_specs=[row(dm), _resident((1, dm)), _resident(wq.shape), _resident(wk.shape), _resident(wv.shape),
                  _resident(wg.shape), _resident(wf.shape), _resident(wft.shape), _resident((1, n_heads)),
                  _resident((n_heads, 1)), _resident((1, HEAD_DIM)), _resident((1, HEAD_DIM))],
        out_specs=[row(width), row(width), row(width), row(width), row(n_heads),
                   pl.BlockSpec((None, n_heads, tm), lambda b, i: (b, 0, i))],
        out_shape=[wide, wide, wide, wide, jax.ShapeDtypeStruct((bsz, seq, n_heads), F32),
                   jax.ShapeDtypeStruct((bsz, n_heads, seq), F32)],
        scratch_shapes=[pltpu.VMEM((n_heads, HEAD_DIM), F32)],
        compiler_params=_cparams("arbitrary", "arbitrary"),
        name="fox_in",
    )(x, nw, wq, wk, wv, wg, wf, wft, bf, bft, qn, kn)


def _fox_attn_kernel(q_ref, k_ref, v_ref, cum_ref, og_ref, o_ref, kb_scr, vb_scr, *, tile):
    h = pl.program_id(1)
    qi = pl.program_id(2)
    scale = HEAD_DIM ** -0.5
    exp2_scale = scale * 1.4426950408889634

    @pl.when(qi == 0)
    def _():
        kb_scr[...] = k_ref[...].astype(BF16)
        vb_scr[...] = v_ref[...].astype(BF16)

    qb = q_ref[...].astype(BF16)

    def key_block(j, carry, diagonal):
        m_prev, l_prev, acc = carry
        cols = pl.ds(pl.multiple_of(j * tile, tile), tile)
        s = _dot_nt(qb, kb_scr[cols, :]) - cum_ref[pl.ds(h, 1), cols] * (1.0 / scale)
        if diagonal:
            s = jnp.where(_iota2((tile, tile), 1) <= _iota2((tile, tile), 0), s, -jnp.inf)
        m_new = jnp.maximum(m_prev, jnp.max(s, axis=-1, keepdims=True))
        alpha = jnp.exp2((m_prev - m_new) * exp2_scale)
        p = jnp.exp2((s - m_new) * exp2_scale)
        l_new = alpha * l_prev + jnp.sum(p, axis=-1, keepdims=True)
        acc = alpha * acc + _dot(p.astype(BF16), vb_scr[cols, :])
        return m_new, l_new, acc

    init = (jnp.full((tile, 1), -jnp.inf, F32), jnp.zeros((tile, 1), F32), jnp.zeros((tile, HEAD_DIM), F32))
    carry = lax.fori_loop(0, qi, lambda j, c: key_block(j, c, False), init)
    _, l_fin, acc = key_block(qi, carry, True)
    o_ref[...] = (acc / l_fin * _sigmoid(og_ref[...])).astype(o_ref.dtype)


def _fox_attn_prompt(q, k, v, cum, og):
    bsz, seq, width = q.shape
    n_heads = width // HEAD_DIM
    tile = min(ATTN_TILE, seq)
    q_spec = pl.BlockSpec((None, tile, HEAD_DIM), lambda b, h, i: (b, i, h))
    kv_spec = pl.BlockSpec((None, seq, HEAD_DIM), lambda b, h, i: (b, 0, h))
    cum_spec = pl.BlockSpec((None, n_heads, seq), lambda b, h, i: (b, 0, 0))
    return pl.pallas_call(
        functools.partial(_fox_attn_kernel, tile=tile),
        grid=(bsz, n_heads, seq // tile),
        in_specs=[q_spec, kv_spec, kv_spec, cum_spec, q_spec],
        out_specs=q_spec,
        out_shape=jax.ShapeDtypeStruct((bsz, seq, width), BF16),
        scratch_shapes=[pltpu.VMEM((seq, HEAD_DIM), BF16), pltpu.VMEM((seq, HEAD_DIM), BF16)],
        compiler_params=_cparams("arbitrary", "arbitrary", "arbitrary"),
        name="fox_attn_prompt",
    )(q, k, v, cum, og)


def _fox_attn_decode_kernel(pt_ref, q_ref, kn_ref, vn_ref, lfn_ref, og_ref, *rest, n_heads, page, n_pages):
    del pt_ref
    k_pages, v_pages, lf_pages = rest[:n_pages], rest[n_pages:2 * n_pages], rest[2 * n_pages:3 * n_pages]
    o_ref, lf_scr = rest[3 * n_pages:]
    flat = n_heads * page
    scale = HEAD_DIM ** -0.5

    for p in range(n_pages):
        lf_scr[p:p + 1, :] = lf_pages[p][...]
    lf = lf_scr[...]
    lane = _iota2((n_pages, flat), 1)
    cum, page_total = lf, lf
    shift = n_heads
    while shift < flat:
        cum = cum + jnp.where(lane >= shift, pltpu.roll(cum, shift, axis=1), 0.0)
        page_total = page_total + pltpu.roll(page_total, shift, axis=1)
        shift *= 2
    earlier = jnp.where(_iota2((n_pages, n_pages), 1) < _iota2((n_pages, n_pages), 0), 1.0, 0.0).astype(BF16)
    before = _dot_sel_left(earlier, page_total)
    cum = cum + before
    past_total = before[n_pages - 1:n_pages, :] + page_total[n_pages - 1:n_pages, :]

    own_head = (_iota2((n_heads, flat), 1) & (n_heads - 1)) == _iota2((n_heads, flat), 0)
    eye = _iota2((n_heads, n_heads), 0) == _iota2((n_heads, n_heads), 1)
    to_rows = lambda r: jnp.sum(jnp.where(eye, jnp.broadcast_to(r, (n_heads, n_heads)), 0.0), axis=-1, keepdims=True)

    q = q_ref[...]
    qb = q.astype(BF16)
    scores = []
    for p in range(n_pages):
        s = _dot_nt(qb, k_pages[p][...].astype(BF16)) * scale - cum[p:p + 1, :]
        scores.append(jnp.where(own_head, s, -jnp.inf))
    s_new = (jnp.sum(q * kn_ref[...], axis=-1, keepdims=True) * scale
             - (to_rows(past_total[:, :n_heads]) + to_rows(lfn_ref[...])))
    m = s_new
    for s in scores:
        m = jnp.maximum(m, jnp.max(s, axis=-1, keepdims=True))
    denom = jnp.exp(s_new - m)
    acc = denom * vn_ref[...]
    for p in range(n_pages):
        w = jnp.exp(scores[p] - m)
        denom = denom + jnp.sum(w, axis=-1, keepdims=True)
        acc = acc + _dot(w.astype(BF16), v_pages[p][...].astype(BF16))
    o_ref[...] = (acc / denom * _sigmoid(og_ref[...])).astype(o_ref.dtype)


def _fox_attn_decode(page_table, q, k_new, v_new, lf_new, og, k_pool, v_pool, lf_pool):
    nseq, n_heads, _ = q.shape
    n_pages = page_table.shape[1]
    flat = k_pool.shape[1]
    page = flat // n_heads
    tok = pl.BlockSpec((None, n_heads, HEAD_DIM), lambda s, pt: (s, 0, 0))
    pool = [pl.BlockSpec((None, flat, HEAD_DIM), lambda s, pt, p=p: (pt[s, p], 0, 0)) for p in range(n_pages)]
    lf_specs = [pl.BlockSpec((None, 1, flat), lambda s, pt, p=p: (pt[s, p], 0, 0)) for p in range(n_pages)]
    grid_spec = pltpu.PrefetchScalarGridSpec(
        num_scalar_prefetch=1,
        grid=(nseq,),
        in_specs=[tok, tok, tok, pl.BlockSpec((None, 1, n_heads), lambda s, pt: (s, 0, 0)), tok] + pool + pool + lf_specs,
        out_specs=tok,
        scratch_shapes=[pltpu.VMEM((n_pages, flat), F32)],
    )
    return pl.pallas_call(
        functools.partial(_fox_attn_decode_kernel, n_heads=n_heads, page=page, n_pages=n_pages),
        grid_spec=grid_spec,
        out_shape=jax.ShapeDtypeStruct((nseq, n_heads, HEAD_DIM), BF16),
        compiler_params=_cparams("arbitrary"),
        name="fox_attn_decode",
    )(page_table, q, k_new, v_new, lf_new, og, *([k_pool] * n_pages), *([v_pool] * n_pages), *([lf_pool] * n_pages))


def _prepare_weights(norm_mix, norm_ffn, gdn_w_in, gdn_conv_w, gdn_a_log, gdn_dt_bias, gdn_norm, gdn_w_out,
                     fox_w_in, fox_b_f, fox_q_norm, fox_k_norm, fox_w_out, ffn_w_in, ffn_conv_w, ffn_conv_b,
                     ffn_w_out):
    conv_dim = gdn_conv_w.shape[-1]
    vw = gdn_w_out.shape[1]
    fw = fox_w_out.shape[1]
    layers = []
    for i in range(norm_mix.shape[0]):
        j = i // 2
        ffn = dict(nw=norm_ffn[i][None], w1=ffn_w_in[i].astype(BF16), cw=ffn_conv_w[i], cb=ffn_conv_b[i][None],
                   w2=ffn_w_out[i].astype(BF16))
        if i % 2 == 0:
            w = gdn_w_in[j].astype(BF16)
            mixer = dict(nw=norm_mix[i][None], wqkv=w[:, :conv_dim], wz=w[:, conv_dim:conv_dim + vw],
                         wba=w[:, conv_dim + vw:], cw=gdn_conv_w[j], alog=gdn_a_log[j][None],
                         dtb=gdn_dt_bias[j][None], onw=gdn_norm[j][None], wo=gdn_w_out[j].astype(BF16))
        else:
            w = fox_w_in[j].astype(BF16)
            mixer = dict(nw=norm_mix[i][None], wq=w[:, :fw], wk=w[:, fw:2 * fw], wv=w[:, 2 * fw:3 * fw],
                         wg=w[:, 3 * fw:4 * fw], wf=w[:, 4 * fw:], wft=w[:, 4 * fw:].T, bf=fox_b_f[j][None],
                         bft=fox_b_f[j][:, None], qn=fox_q_norm[j][None], kn=fox_k_norm[j][None],
                         wo=fox_w_out[j].astype(BF16))
        layers.append((mixer, ffn))
    return layers


def _trunk_prompt(x, layers):
    bsz, seq, _ = x.shape
    gdn_s, gdn_cb, ks, vs, lfs, ffn_cb = [], [], [], [], [], []
    for i, (m, f) in enumerate(layers):
        if i % 2 == 0:
            conv_dim = m["wqkv"].shape[1]
            q, k, v, z, beta, g, cb = _gdn_in(x, jnp.zeros((bsz, 3, conv_dim), F32), m["nw"], m["wqkv"], m["wz"],
                                              m["wba"], m["cw"], m["alog"], m["dtb"], decode=False)
            o, s = _gdn_core_prompt(q, k, v, z, beta, g, m["onw"])
            gdn_cb.append(cb)
            gdn_s.append(s)
        else:
            q, k, v, og, lf, cum = _fox_in(x, m["nw"], m["wq"], m["wk"], m["wv"], m["wg"], m["wf"], m["wft"],
                                           m["bf"], m["bft"], m["qn"], m["kn"])
            o = _fox_attn_prompt(q, k, v, cum, og)
            n_heads = lf.shape[-1]
            ks.append(k.reshape(bsz, seq, n_heads, HEAD_DIM))
            vs.append(v.reshape(bsz, seq, n_heads, HEAD_DIM))
            lfs.append(lf)
        d_ff = f["w2"].shape[0]
        x, fcb = _out_ffn(x, o, m["wo"], f["nw"], f["w1"], f["cw"], f["cb"], f["w2"],
                          jnp.zeros((bsz, 2, 2 * d_ff), F32), decode=False)
        ffn_cb.append(fcb)
    return x, jnp.stack(gdn_s), jnp.stack(gdn_cb), jnp.stack(ks), jnp.stack(vs), jnp.stack(lfs), jnp.stack(ffn_cb)


def _trunk_decode(x, state_gdn, state_gdn_conv, cache_k, cache_v, cache_logf, state_ffn_conv, page_table, layers):
    nseq = x.shape[0]
    x = x.reshape(1, nseq, x.shape[-1])
    gdn_s, gdn_cb, ks, vs, lfs, ffn_cb = [], [], [], [], [], []
    for i, (m, f) in enumerate(layers):
        j = i // 2
        if i % 2 == 0:
            q, k, v, z, beta, g, cb = _gdn_in(x, state_gdn_conv[j].reshape(1, nseq, -1), m["nw"], m["wqkv"], m["wz"],
                                              m["wba"], m["cw"], m["alog"], m["dtb"], decode=True)
            per_seq = lambda a: a.reshape(nseq, 1, a.shape[-1])
            o, s = _gdn_core_decode(per_seq(q), per_seq(k), per_seq(v), per_seq(z), per_seq(beta), per_seq(g),
                                    m["onw"], state_gdn[j])
            o = o.reshape(1, nseq, -1)
            gdn_cb.append(cb.reshape(state_gdn_conv[j].shape))
            gdn_s.append(s)
        else:
            q, k, v, og, lf, _ = _fox_in(x, m["nw"], m["wq"], m["wk"], m["wv"], m["wg"], m["wf"], m["wft"],
                                         m["bf"], m["bft"], m["qn"], m["kn"])
            n_heads = lf.shape[-1]
            n_pool, page = cache_k.shape[1], cache_k.shape[2]
            heads = lambda a: a.reshape(nseq, n_heads, HEAD_DIM)
            o = _fox_attn_decode(page_table, heads(q), heads(k), heads(v), lf.reshape(nseq, 1, n_heads), heads(og),
                                 cache_k[j].reshape(n_pool, page * n_heads, HEAD_DIM),
                                 cache_v[j].reshape(n_pool, page * n_heads, HEAD_DIM),
                                 cache_logf[j].reshape(n_pool, 1, page * n_heads))
            o = o.reshape(1, nseq, -1)
            ks.append(k.reshape(nseq, 1, n_heads, HEAD_DIM))
            vs.append(v.reshape(nseq, 1, n_heads, HEAD_DIM))
            lfs.append(lf.reshape(nseq, 1, n_heads))
        x, fcb = _out_ffn(x, o, m["wo"], f["nw"], f["w1"], f["cw"], f["cb"], f["w2"],
                          state_ffn_conv[i].reshape(1, nseq, -1), decode=True)
        ffn_cb.append(fcb.reshape(state_ffn_conv[i].shape))
    return (x.reshape(nseq, 1, -1), jnp.stack(gdn_s), jnp.stack(gdn_cb), jnp.stack(ks), jnp.stack(vs),
            jnp.stack(lfs), jnp.stack(ffn_cb))


def kernel(x_prompt, x_sample, state_gdn, state_gdn_conv, cache_k, cache_v, cache_logf, state_ffn_conv, page_table,
           norm_mix, norm_ffn, gdn_w_in, gdn_conv_w, gdn_a_log, gdn_dt_bias, gdn_norm, gdn_w_out, fox_w_in, fox_b_f,
           fox_q_norm, fox_k_norm, fox_w_out, ffn_w_in, ffn_conv_w, ffn_conv_b, ffn_w_out):
    layers = _prepare_weights(norm_mix, norm_ffn, gdn_w_in, gdn_conv_w, gdn_a_log, gdn_dt_bias, gdn_norm, gdn_w_out,
                              fox_w_in, fox_b_f, fox_q_norm, fox_k_norm, fox_w_out, ffn_w_in, ffn_conv_w, ffn_conv_b,
                              ffn_w_out)
    prompt = _trunk_prompt(x_prompt, layers)
    sample = _trunk_decode(x_sample, state_gdn, state_gdn_conv, cache_k, cache_v, cache_logf, state_ffn_conv,
                           page_table, layers)
    return (prompt[0], sample[0]) + prompt[1:] + sample[1:]
```

```python
import functools

import jax
import jax.numpy as jnp
from jax import lax
from jax.experimental import pallas as pl
from jax.experimental.pallas import tpu as pltpu

F32 = jnp.float32
BF16 = jnp.bfloat16

HEAD_DIM = 128
GDN_CHUNK = 64
EPS = 1e-6
V7X_VMEM_LIMIT_BYTES = 56 * 1024 * 1024
ROW_TILE = 512
ATTN_TILE = 512
CONV_COLS = 512
FFN_COLS = 256
GDN_HEADS_PER_STEP = 4
GDN_SEQ_TILE = 1024
GDN_SOLVE_BATCH = 16


def _cparams(*semantics):
    return pltpu.CompilerParams(dimension_semantics=semantics, vmem_limit_bytes=V7X_VMEM_LIMIT_BYTES)


def _resident(shape):
    return pl.BlockSpec(shape, lambda *_: (0,) * len(shape), pipeline_mode=pl.Buffered(1))


def _sigmoid(x):
    return 1.0 / (1.0 + jnp.exp(-x))


def _silu(x):
    return x * _sigmoid(x)


def _softplus(x):
    return jnp.maximum(x, 0.0) + jnp.log(1.0 + jnp.exp(-jnp.abs(x)))


def _rmsnorm_rows(x, w):
    return x * lax.rsqrt(jnp.mean(x * x, axis=-1, keepdims=True) + EPS) * w


def _dot(a, b):
    return jnp.dot(a, b, preferred_element_type=F32)


def _dot_nt(a, b):
    return lax.dot_general(a, b, (((1,), (1,)), ((), ())), preferred_element_type=F32)


def _dot_tn(a, b):
    return lax.dot_general(a, b, (((0,), (0,)), ((), ())), preferred_element_type=F32)


def _split3(x):
    hi = x.astype(BF16)
    r = x - hi.astype(F32)
    mid = r.astype(BF16)
    lo = (r - mid.astype(F32)).astype(BF16)
    return hi, mid, lo


def _dot_sel_left(sel, x):
    hi, mid, lo = _split3(x)
    return _dot(sel, hi) + _dot(sel, mid) + _dot(sel, lo)


def _dot_sel_right(x, sel):
    hi, mid, lo = _split3(x)
    return _dot(hi, sel) + _dot(mid, sel) + _dot(lo, sel)


def _dot_f32(a, b):
    return jnp.dot(a, b, preferred_element_type=F32, precision=lax.Precision.HIGHEST)


def _iota2(shape, dim):
    return lax.broadcasted_iota(jnp.int32, shape, dim)


def _gdn_in_kernel(x_ref, nw_ref, wqkv_ref, wz_ref, wba_ref, cw_ref, alog_ref, dtb_ref, buf_ref,
                   q_ref, k_ref, v_ref, z_ref, beta_ref, g_ref, cs_ref, ubuf, carry,
                   *, decode, tm, qk_width, conv_dim, n_vheads):
    hb = _rmsnorm_rows(x_ref[...], nw_ref[...]).astype(BF16)

    if not decode:
        @pl.when(pl.program_id(1) == 0)
        def _():
            carry[...] = jnp.zeros_like(carry)
            carry[5:8, :] = buf_ref[...]

    wc = CONV_COLS
    ahead = _dot(hb, wqkv_ref[:, 0:wc])
    for j in range(conv_dim // wc):
        c0 = j * wc
        cols = slice(c0, c0 + wc)
        u = ahead
        if j + 1 < conv_dim // wc:
            ahead = _dot(hb, wqkv_ref[:, c0 + wc:c0 + 2 * wc])
        else:
            ahead = _dot(hb, wz_ref[...])
        if decode:
            b0 = buf_ref[:, c0:c0 + wc]
            b1 = buf_ref[:, conv_dim + c0:conv_dim + c0 + wc]
            b2 = buf_ref[:, 2 * conv_dim + c0:2 * conv_dim + c0 + wc]
            y = cw_ref[0:1, cols] * b0 + cw_ref[1:2, cols] * b1 + cw_ref[2:3, cols] * b2 + cw_ref[3:4, cols] * u
            cs_ref[:, c0:c0 + wc] = b1
            cs_ref[:, conv_dim + c0:conv_dim + c0 + wc] = b2
            cs_ref[:, 2 * conv_dim + c0:2 * conv_dim + c0 + wc] = u
        else:
            ubuf[0:8, :] = carry[:, cols]
            ubuf[8:8 + tm, :] = u
            y = (cw_ref[3:4, cols] * u + cw_ref[2:3, cols] * ubuf[7:7 + tm, :]
                 + cw_ref[1:2, cols] * ubuf[6:6 + tm, :] + cw_ref[0:1, cols] * ubuf[5:5 + tm, :])
            carry[:, cols] = ubuf[tm:tm + 8, :]
        y = _silu(y)
        for hh in range(wc // HEAD_DIM):
            yh = y[:, hh * HEAD_DIM:(hh + 1) * HEAD_DIM]
            col = c0 + hh * HEAD_DIM
            if col < 2 * qk_width:
                yh = yh * lax.rsqrt(jnp.sum(yh * yh, axis=-1, keepdims=True) + EPS)
            dst, col = ((q_ref, col) if col < qk_width else
                        (k_ref, col - qk_width) if col < 2 * qk_width else (v_ref, col - 2 * qk_width))
            if decode:
                dst[:, col:col + HEAD_DIM] = yh
            else:
                dst[col // HEAD_DIM] = yh
    if not decode:
        cs_ref[...] = carry[5:8, :]

    z = ahead
    if decode:
        z_ref[...] = z
    else:
        for h in range(n_vheads):
            z_ref[h] = z[:, h * HEAD_DIM:(h + 1) * HEAD_DIM]
    ba = _dot(hb, wba_ref[...])
    beta_ref[...] = _sigmoid(ba[:, :n_vheads])
    g = -jnp.exp(alog_ref[...]) * _softplus(ba[:, n_vheads:] + dtb_ref[...])
    if decode:
        g_ref[...] = g
    else:
        r, c = _iota2((tm, tm), 0), _iota2((tm, tm), 1)
        same_chunk_lower = jnp.where((r // GDN_CHUNK == c // GDN_CHUNK) & (c <= r), 1.0, 0.0).astype(BF16)
        g_ref[...] = _dot_sel_left(same_chunk_lower, g)


def _gdn_in(x, buf, nw, wqkv, wz, wba, cw, alog, dtb, *, decode):
    bsz, seq, dm = x.shape
    conv_dim = wqkv.shape[1]
    vw = wz.shape[1]
    qk_width = (conv_dim - vw) // 2
    n_vheads = vw // HEAD_DIM
    tm = min(ROW_TILE, seq)
    grid = (bsz, seq // tm)
    row = lambda width: pl.BlockSpec((None, tm, width), lambda b, i: (b, i, 0))
    if decode:
        buf_spec = pl.BlockSpec((None, tm, 3 * conv_dim), lambda b, i: (b, i, 0))
        cs_shape, cs_spec = (bsz, seq, 3 * conv_dim), pl.BlockSpec((None, tm, 3 * conv_dim), lambda b, i: (b, i, 0))
    else:
        buf_spec = pl.BlockSpec((None, 3, conv_dim), lambda b, i: (b, 0, 0))
        cs_shape, cs_spec = (bsz, 3, conv_dim), pl.BlockSpec((None, 3, conv_dim), lambda b, i: (b, 0, 0))
    kern = functools.partial(_gdn_in_kernel, decode=decode, tm=tm, qk_width=qk_width, conv_dim=conv_dim,
                             n_vheads=n_vheads)
    if decode:
        wide = lambda width: (row(width), jax.ShapeDtypeStruct((bsz, seq, width), F32))
    else:
        wide = lambda width: (pl.BlockSpec((None, width // HEAD_DIM, tm, HEAD_DIM), lambda b, i: (b, 0, i, 0)),
                              jax.ShapeDtypeStruct((bsz, width // HEAD_DIM, seq, HEAD_DIM), F32))
    (q_spec, q_shape), (v_spec, v_shape) = wide(qk_width), wide(vw)
    gate_shape = jax.ShapeDtypeStruct((bsz, seq, n_vheads), F32)
    return pl.pallas_call(
        kern,
        grid=grid,
        in_specs=[row(dm), _resident((1, dm)), _resident(wqkv.shape), _resident(wz.shape), _resident(wba.shape),
                  _resident(cw.shape), _resident((1, n_vheads)), _resident((1, n_vheads)), buf_spec],
        out_specs=[q_spec, q_spec, v_spec, v_spec, row(n_vheads), row(n_vheads), cs_spec],
        out_shape=[q_shape, q_shape, v_shape, v_shape, gate_shape, gate_shape, jax.ShapeDtypeStruct(cs_shape, F32)],
        scratch_shapes=[pltpu.VMEM((tm + 8, CONV_COLS), F32), pltpu.VMEM((8, conv_dim), F32)],
        compiler_params=_cparams("arbitrary", "arbitrary"),
        name="gdn_in_decode" if decode else "gdn_in_prompt",
    )(x, nw, wqkv, wz, wba, cw, alog, dtb, buf)


def _bdot(a, b):
    return lax.dot_general(a, b, (((2,), (1,)), ((0,), (0,))), preferred_element_type=F32)


def _bdot_nt(a, b):
    return lax.dot_general(a, b, (((2,), (2,)), ((0,), (0,))), preferred_element_type=F32)


def _unit_lower_inverse(n, row, col):
    eye = jnp.where(row == col, 1.0, 0.0)
    inv = eye - jnp.where((row >> 1) == (col >> 1), n, 0.0)
    size = n.shape[-1]
    bits = 2
    while (1 << bits) <= size:
        off_diag = jnp.where((row >> bits) == (col >> bits),
                             jnp.where((row >> (bits - 1)) == (col >> (bits - 1)), 0.0, n), 0.0)
        inv_b = inv.astype(BF16)
        inv = inv - _bdot(inv_b, _bdot(off_diag.astype(BF16), inv_b).astype(BF16))
        bits += 1
    n_hi, n_mid, _ = _split3(n)
    x_hi, x_mid, _ = _split3(inv)
    resid = eye - inv - (_bdot(n_hi, x_hi) + _bdot(n_mid, x_hi) + _bdot(n_hi, x_mid))
    return inv + _bdot(x_hi, resid.astype(BF16))


def _gdn_core_kernel(q_ref, k_ref, v_ref, z_ref, beta_ref, gcum_ref, nw_ref, o_ref, s_out_ref,
                     s_scr, u_scr, w_scr, att_scr, qg_scr, kd_scr, egl_scr, gb_scr, *, lt, hb, rep, nb):
    c = GDN_CHUNK
    groups = lt // c // nb
    t = pl.program_id(2)
    head0 = pl.program_id(1) * hb
    q_scale = HEAD_DIM ** -0.5

    @pl.when(t == 0)
    def _():
        s_scr[...] = jnp.zeros_like(s_scr)

    row = lax.broadcasted_iota(jnp.int32, (nb, c, c), 1)
    col = lax.broadcasted_iota(jnp.int32, (nb, c, c), 2)
    lower = row >= col
    strict = row > col
    lane = _iota2((nb * c, HEAD_DIM), 1)
    gate_lane = _iota2((nb * c, beta_ref.shape[-1]), 1)
    pick = jnp.where(lane < 3, 1.0, 0.0).astype(BF16).reshape(nb, c, HEAD_DIM)

    def solve(idx, _):
        hh = idx // groups
        rows = pl.ds(pl.multiple_of((idx - hh * groups) * (nb * c), nb * c), nb * c)
        kh = hh // rep
        own = gate_lane == head0 + hh
        bcol = jnp.sum(jnp.where(own, beta_ref[rows, :], 0.0), axis=-1, keepdims=True)
        gcol = jnp.sum(jnp.where(own, gcum_ref[rows, :], 0.0), axis=-1, keepdims=True)
        g_hi = gcol.astype(BF16).astype(F32)
        g_mid = (gcol - g_hi).astype(BF16).astype(F32)
        g_lo = gcol - g_hi - g_mid
        pieces = jnp.where(lane == 0, g_hi, jnp.where(lane == 1, g_mid, jnp.where(lane == 2, g_lo, 0.0)))
        gcum_t = _bdot_nt(pick, pieces.astype(BF16).reshape(nb, c, HEAD_DIM))
        g3 = gcol.reshape(nb, c, 1)
        b3 = bcol.reshape(nb, c, 1)
        decay = jnp.where(lower, jnp.exp(jnp.minimum(g3 - gcum_t, 0.0)), 0.0)
        q3 = (q_ref[kh, rows, :] * q_scale).reshape(nb, c, HEAD_DIM)
        k3 = k_ref[kh, rows, :].reshape(nb, c, HEAD_DIM)
        v3 = v_ref[hh, rows, :].reshape(nb, c, HEAD_DIM)
        kb = k3 * b3
        k_bf = k3.astype(BF16)
        n = jnp.where(strict, _bdot_nt(kb.astype(BF16), k_bf) * decay, 0.0)
        inv = _unit_lower_inverse(n, row, col)
        eg = jnp.exp(g3)
        rhs = jnp.concatenate([v3 * b3, kb * eg], axis=-1).astype(BF16)
        sol = _bdot(inv.astype(BF16), rhs)
        attn = jnp.where(lower, _bdot_nt(q3.astype(BF16), k_bf) * decay, 0.0)
        g_last = g3[:, c - 1:c, :]
        u_scr[hh, rows, :] = sol[:, :, :HEAD_DIM].reshape(nb * c, HEAD_DIM)
        w_scr[hh, rows, :] = sol[:, :, HEAD_DIM:].reshape(nb * c, HEAD_DIM).astype(BF16)
        att_scr[hh, rows, :] = attn.reshape(nb * c, c).astype(BF16)
        qg_scr[hh, rows, :] = (q3 * eg).reshape(nb * c, HEAD_DIM).astype(BF16)
        kd_scr[hh, rows, :] = (k3 * jnp.exp(g_last - g3)).reshape(nb * c, HEAD_DIM).astype(BF16)
        gb_scr[...] = jnp.broadcast_to(gcol, gb_scr.shape)
        chunk0 = pl.multiple_of((idx - hh * groups) * nb, nb)
        egl_scr[hh, pl.ds(chunk0, nb), :] = jnp.exp(gb_scr[pl.ds(c - 1, nb, stride=c), :])
        return 0

    lax.fori_loop(0, hb * groups, solve, 0)

    def step(i, _):
        rows = pl.ds(pl.multiple_of(i * c, c), c)
        heads = range(hb)
        state = [s_scr[hh] for hh in heads]
        res = [_dot(jnp.concatenate([w_scr[hh, rows, :], qg_scr[hh, rows, :]], axis=0), state[hh].astype(BF16))
               for hh in heads]
        v_new = [(u_scr[hh, rows, :] - res[hh][:c]).astype(BF16) for hh in heads]
        o = [res[hh][c:] + _dot(att_scr[hh, rows, :], v_new[hh]) for hh in heads]
        grown = [_dot_tn(kd_scr[hh, rows, :], v_new[hh]) for hh in heads]
        for hh in heads:
            s_scr[hh] = state[hh] * egl_scr[hh, pl.ds(i, 1), :] + grown[hh]
            gated = _rmsnorm_rows(o[hh], nw_ref[...]) * _silu(z_ref[hh, rows, :])
            o_ref[rows, hh * HEAD_DIM:(hh + 1) * HEAD_DIM] = gated.astype(o_ref.dtype)
        return 0

    lax.fori_loop(0, lt // c, step, 0)

    @pl.when(t == pl.num_programs(2) - 1)
    def _():
        s_out_ref[...] = s_scr[...]


def _gdn_core_prompt(q, k, v, z, beta, gcum, nw):
    bsz, n_vheads, seq, _ = v.shape
    rep = n_vheads // q.shape[1]
    hb = min(GDN_HEADS_PER_STEP, n_vheads)
    lt = min(GDN_SEQ_TILE, seq)
    nb = min(GDN_SOLVE_BATCH, lt // GDN_CHUNK)
    heads = lambda n: pl.BlockSpec((None, n, lt, HEAD_DIM), lambda b, h, t: (b, h, t, 0))
    gates = pl.BlockSpec((None, lt, n_vheads), lambda b, h, t: (b, t, 0))
    per_head = lambda width, dtype: pltpu.VMEM((hb, lt, width), dtype)
    return pl.pallas_call(
        functools.partial(_gdn_core_kernel, lt=lt, hb=hb, rep=rep, nb=nb),
        grid=(bsz, n_vheads // hb, seq // lt),
        in_specs=[heads(hb // rep), heads(hb // rep), heads(hb), heads(hb), gates, gates, _resident((1, HEAD_DIM))],
        out_specs=[pl.BlockSpec((None, lt, hb * HEAD_DIM), lambda b, h, t: (b, t, h)),
                   pl.BlockSpec((None, hb, HEAD_DIM, HEAD_DIM), lambda b, h, t: (b, h, 0, 0))],
        out_shape=[jax.ShapeDtypeStruct((bsz, seq, n_vheads * HEAD_DIM), BF16),
                   jax.ShapeDtypeStruct((bsz, n_vheads, HEAD_DIM, HEAD_DIM), F32)],
        scratch_shapes=[pltpu.VMEM((hb, HEAD_DIM, HEAD_DIM), F32), per_head(HEAD_DIM, F32), per_head(HEAD_DIM, BF16),
                        per_head(GDN_CHUNK, BF16), per_head(HEAD_DIM, BF16), per_head(HEAD_DIM, BF16),
                        pltpu.VMEM((hb, lt // GDN_CHUNK, HEAD_DIM), F32), pltpu.VMEM((nb * GDN_CHUNK, HEAD_DIM), F32)],
        compiler_params=_cparams("arbitrary", "arbitrary", "arbitrary"),
        name="gdn_core_prompt",
    )(q, k, v, z, beta, gcum, nw)


def _gdn_core_decode_kernel(q_ref, k_ref, v_ref, z_ref, beta_ref, g_ref, nw_ref, s_ref, o_ref, so_ref,
                            *, n_vheads, rep):
    q_scale = HEAD_DIM ** -0.5
    row = _iota2((8, HEAD_DIM), 0)

    def rows3(a, b, c):
        return jnp.where(row == 0, a, jnp.where(row == 1, b, jnp.where(row == 2, c, 0.0)))

    heads = range(n_vheads)
    lanes = [slice(h * HEAD_DIM, (h + 1) * HEAD_DIM) for h in heads]
    klanes = [slice(h // rep * HEAD_DIM, (h // rep + 1) * HEAD_DIM) for h in heads]
    qh = [q_ref[:, klanes[h]] * q_scale for h in heads]
    kh = [k_ref[:, klanes[h]] for h in heads]
    beta = [beta_ref[:, h:h + 1] for h in heads]
    eg = [jnp.exp(g_ref[:, h:h + 1]) for h in heads]
    res = [_dot(rows3(kh[h] * beta[h] * eg[h], qh[h] * eg[h], 0.0).astype(BF16), s_ref[h].astype(BF16))
           for h in heads]
    v_new = [v_ref[:, lanes[h]] * beta[h] - res[h][0:1, :] for h in heads]
    k_hi = [kh[h].astype(BF16).astype(F32) for h in heads]
    v_hi = [v_new[h].astype(BF16).astype(F32) for h in heads]
    grown = [_dot_tn(rows3(k_hi[h], kh[h] - k_hi[h], k_hi[h]).astype(BF16),
                     rows3(v_hi[h], v_hi[h], v_new[h] - v_hi[h]).astype(BF16)) for h in heads]
    for h in heads:
        so_ref[h] = s_ref[h] * eg[h] + grown[h]
        o = res[h][1:2, :] + jnp.sum(qh[h] * kh[h], axis=-1, keepdims=True) * v_new[h]
        gated = _rmsnorm_rows(o, nw_ref[...]) * _silu(z_ref[:, lanes[h]])
        o_ref[:, lanes[h]] = gated.astype(o_ref.dtype)


def _gdn_core_decode(q, k, v, z, beta, g, nw, state):
    nseq, _, vw = v.shape
    n_vheads = vw // HEAD_DIM
    rep = vw // q.shape[-1]
    row = lambda width: pl.BlockSpec((None, 1, width), lambda s: (s, 0, 0))
    st = pl.BlockSpec((None, n_vheads, HEAD_DIM, HEAD_DIM), lambda s: (s, 0, 0, 0))
    return pl.pallas_call(
        functools.partial(_gdn_core_decode_kernel, n_vheads=n_vheads, rep=rep),
        grid=(nseq,),
        in_specs=[row(q.shape[-1]), row(q.shape[-1]), row(vw), row(vw), row(n_vheads), row(n_vheads),
                  _resident((1, HEAD_DIM)), st],
        out_specs=[row(vw), st],
        out_shape=[jax.ShapeDtypeStruct((nseq, 1, vw), BF16), jax.ShapeDtypeStruct(state.shape, F32)],
        compiler_params=_cparams("arbitrary"),
        name="gdn_core_decode",
    )(q, k, v, z, beta, g, nw, state)


def _out_ffn_kernel(x_ref, o_ref, wo_ref, nw_ref, w1_ref, cw_ref, cb_ref, w2_ref, buf_ref,
                    y_ref, cs_ref, gbuf, ubuf, carry, acc, *, decode, tm, d_ff):
    x1 = x_ref[...] + _dot(o_ref[...], wo_ref[...])
    hb = _rmsnorm_rows(x1, nw_ref[...]).astype(BF16)

    if not decode:
        @pl.when(pl.program_id(1) == 0)
        def _():
            carry[...] = jnp.zeros_like(carry)
            carry[6:8, :] = buf_ref[...]

    def conv(u, scratch, c0, wc):
        cols = slice(c0, c0 + wc)
        if decode:
            width = 2 * d_ff
            b0 = buf_ref[:, c0:c0 + wc]
            b1 = buf_ref[:, width + c0:width + c0 + wc]
            cs_ref[:, c0:c0 + wc] = b1
            cs_ref[:, width + c0:width + c0 + wc] = u
            y = cw_ref[0:1, cols] * b0 + cw_ref[1:2, cols] * b1 + cw_ref[2:3, cols] * u
        else:
            scratch[0:8, :] = carry[:, cols]
            scratch[8:8 + tm, :] = u
            y = (cw_ref[2:3, cols] * u + cw_ref[1:2, cols] * scratch[7:7 + tm, :]
                 + cw_ref[0:1, cols] * scratch[6:6 + tm, :])
            carry[:, cols] = scratch[tm:tm + 8, :]
        return y + cb_ref[:, cols]

    wc = FFN_COLS

    def project(j):
        c0 = j * wc
        return _dot(hb, w1_ref[:, c0:c0 + wc]), _dot(hb, w1_ref[:, d_ff + c0:d_ff + c0 + wc])

    ahead = project(0)
    for j in range(d_ff // wc):
        c0 = j * wc
        pre_gate, pre_up = ahead
        if j + 1 < d_ff // wc:
            ahead = project(j + 1)
        gate = conv(pre_gate, gbuf, c0, wc)
        up = conv(pre_up, ubuf, d_ff + c0, wc)
        act = (_silu(gate) * up).astype(BF16)
        part = _dot(act, w2_ref[c0:c0 + wc, :])
        if j == 0:
            acc[...] = part
        else:
            acc[...] += part
    if not decode:
        cs_ref[...] = carry[6:8, :]
    y_ref[...] = x1 + acc[...]


def _out_ffn(x, o, wo, nw, w1, cw, cb, w2, buf, *, decode):
    bsz, seq, dm = x.shape
    d_ff = w2.shape[0]
    tm = min(ROW_TILE, seq)
    row = lambda width: pl.BlockSpec((None, tm, width), lambda b, i: (b, i, 0))
    if decode:
        buf_spec = row(4 * d_ff)
        cs_shape, cs_spec = (bsz, seq, 4 * d_ff), row(4 * d_ff)
    else:
        buf_spec = pl.BlockSpec((None, 2, 2 * d_ff), lambda b, i: (b, 0, 0))
        cs_shape, cs_spec = (bsz, 2, 2 * d_ff), pl.BlockSpec((None, 2, 2 * d_ff), lambda b, i: (b, 0, 0))
    return pl.pallas_call(
        functools.partial(_out_ffn_kernel, decode=decode, tm=tm, d_ff=d_ff),
        grid=(bsz, seq // tm),
        in_specs=[row(dm), row(o.shape[-1]), _resident(wo.shape), _resident((1, dm)), _resident(w1.shape),
                  _resident(cw.shape), _resident((1, 2 * d_ff)), _resident(w2.shape), buf_spec],
        out_specs=[row(dm), cs_spec],
        out_shape=[jax.ShapeDtypeStruct((bsz, seq, dm), F32), jax.ShapeDtypeStruct(cs_shape, F32)],
        scratch_shapes=[pltpu.VMEM((tm + 8, FFN_COLS), F32), pltpu.VMEM((tm + 8, FFN_COLS), F32),
                        pltpu.VMEM((8, 2 * d_ff), F32), pltpu.VMEM((tm, dm), F32)],
        compiler_params=_cparams("arbitrary", "arbitrary"),
        name="out_ffn_decode" if decode else "out_ffn_prompt",
    )(x, o, wo, nw, w1, cw, cb, w2, buf)


def _fox_in_kernel(x_ref, nw_ref, wq_ref, wk_ref, wv_ref, wg_ref, wf_ref, wft_ref, bf_ref, bft_ref, qn_ref, kn_ref,
                   q_ref, k_ref, v_ref, og_ref, lf_ref, cum_ref, carry, *, tm, n_heads):
    hb = _rmsnorm_rows(x_ref[...], nw_ref[...]).astype(BF16)
    q = _dot(hb, wq_ref[...])
    k = _dot(hb, wk_ref[...])
    for h in range(n_heads):
        lanes = slice(h * HEAD_DIM, (h + 1) * HEAD_DIM)
        q_ref[:, lanes] = _rmsnorm_rows(q[:, lanes], qn_ref[...])
    v_ref[...] = _dot(hb, wv_ref[...])
    for h in range(n_heads):
        lanes = slice(h * HEAD_DIM, (h + 1) * HEAD_DIM)
        k_ref[:, lanes] = _rmsnorm_rows(k[:, lanes], kn_ref[...])
    og_ref[...] = _dot(hb, wg_ref[...])
    lf_ref[...] = -_softplus(-(_dot(hb, wf_ref[...]) + bf_ref[...]))

    @pl.when(pl.program_id(1) == 0)
    def _():
        carry[...] = jnp.zeros_like(carry)

    lf_t = -_softplus(-(_dot_nt(wft_ref[...], hb) + bft_ref[...]))
    upper = jnp.where(_iota2((tm, tm), 0) <= _iota2((tm, tm), 1), 1.0, 0.0).astype(BF16)
    cum = _dot_sel_right(lf_t, upper) + carry[:, 0:1]
    cum_ref[...] = cum
    carry[...] = jnp.broadcast_to(cum[:, tm - 1:tm], carry.shape)


def _fox_in(x, nw, wq, wk, wv, wg, wf, wft, bf, bft, qn, kn):
    bsz, seq, dm = x.shape
    width = wq.shape[1]
    n_heads = width // HEAD_DIM
    tm = min(ROW_TILE, seq)
    row = lambda w: pl.BlockSpec((None, tm, w), lambda b, i: (b, i, 0))
    wide = jax.ShapeDtypeStruct((bsz, seq, width), F32)
    return pl.pallas_call(
        functools.partial(_fox_in_kernel, tm=tm, n_heads=n_heads),
        grid=(bsz, seq // tm),
        in_specs=[row(dm), _resident((1, dm)), _resident(wq.shape), _resident(wk.shape), _resident(wv.shape),
                  _resident(wg.shape), _resident(wf.shape), _resident(wft.shape), _resident((1, n_heads)),
                  _resident((n_heads, 1)), _resident((1, HEAD_DIM)), _resident((1, HEAD_DIM))],
        out_specs=[row(width), row(width), row(width), row(width), row(n_heads),
                   pl.BlockSpec((None, n_heads, tm), lambda b, i: (b, 0, i))],
        out_shape=[wide, wide, wide, wide, jax.ShapeDtypeStruct((bsz, seq, n_heads), F32),
                   jax.ShapeDtypeStruct((bsz, n_heads, seq), F32)],
        scratch_shapes=[pltpu.VMEM((n_heads, HEAD_DIM), F32)],
        compiler_params=_cparams("arbitrary", "arbitrary"),
        name="fox_in",
    )(x, nw, wq, wk, wv, wg, wf, wft, bf, bft, qn, kn)


def _fox_attn_kernel(q_ref, k_ref, v_ref, cum_ref, og_ref, o_ref, kb_scr, vb_scr, *, tile):
    h = pl.program_id(1)
    qi = pl.program_id(2)
    scale = HEAD_DIM ** -0.5
    exp2_scale = scale * 1.4426950408889634

    @pl.when(qi == 0)
    def _():
        kb_scr[...] = k_ref[...].astype(BF16)
        vb_scr[...] = v_ref[...].astype(BF16)

    qb = q_ref[...].astype(BF16)

    def lane_tiles(x):
        return [x[:, t * HEAD_DIM:(t + 1) * HEAD_DIM] for t in range(tile // HEAD_DIM)]

    def key_block(j, carry, diagonal):
        m_prev, l_prev, acc = carry
        cols = pl.ds(pl.multiple_of(j * tile, tile), tile)
        s = _dot_nt(qb, kb_scr[cols, :]) - cum_ref[pl.ds(h, 1), cols] * (1.0 / scale)
        if diagonal:
            s = jnp.where(_iota2((tile, tile), 1) <= _iota2((tile, tile), 0), s, -jnp.inf)
        m_new = jnp.maximum(m_prev, jnp.max(functools.reduce(jnp.maximum, lane_tiles(s)), axis=-1, keepdims=True))
        alpha = jnp.exp2((m_prev - m_new) * exp2_scale)
        p = jnp.exp2((s - m_new) * exp2_scale)
        l_new = alpha * l_prev + jnp.sum(functools.reduce(jnp.add, lane_tiles(p)), axis=-1, keepdims=True)
        acc = alpha * acc + _dot(p.astype(BF16), vb_scr[cols, :])
        return m_new, l_new, acc

    init = (jnp.full((tile, 1), -jnp.inf, F32), jnp.zeros((tile, 1), F32), jnp.zeros((tile, HEAD_DIM), F32))
    carry = lax.fori_loop(0, qi, lambda j, c: key_block(j, c, False), init)
    _, l_fin, acc = key_block(qi, carry, True)
    o_ref[...] = (acc / l_fin * _sigmoid(og_ref[...])).astype(o_ref.dtype)


def _fox_attn_prompt(q, k, v, cum, og):
    bsz, seq, width = q.shape
    n_heads = width // HEAD_DIM
    tile = min(ATTN_TILE, seq)
    q_spec = pl.BlockSpec((None, tile, HEAD_DIM), lambda b, h, i: (b, i, h))
    kv_spec = pl.BlockSpec((None, seq, HEAD_DIM), lambda b, h, i: (b, 0, h))
    cum_spec = pl.BlockSpec((None, n_heads, seq), lambda b, h, i: (b, 0, 0))
    return pl.pallas_call(
        functools.partial(_fox_attn_kernel, tile=tile),
        grid=(bsz, n_heads, seq // tile),
        in_specs=[q_spec, kv_spec, kv_spec, cum_spec, q_spec],
        out_specs=q_spec,
        out_shape=jax.ShapeDtypeStruct((bsz, seq, width), BF16),
        scratch_shapes=[pltpu.VMEM((seq, HEAD_DIM), BF16), pltpu.VMEM((seq, HEAD_DIM), BF16)],
        compiler_params=_cparams("arbitrary", "arbitrary", "arbitrary"),
        name="fox_attn_prompt",
    )(q, k, v, cum, og)


def _fox_attn_decode_kernel(pt_ref, q_ref, kn_ref, vn_ref, lfn_ref, og_ref, *rest, n_heads, page, n_pages):
    del pt_ref
    k_pages, v_pages, lf_pages = rest[:n_pages], rest[n_pages:2 * n_pages], rest[2 * n_pages:3 * n_pages]
    o_ref, lf_scr = rest[3 * n_pages:]
    flat = n_heads * page
    scale = HEAD_DIM ** -0.5

    for p in range(n_pages):
        lf_scr[p:p + 1, :] = lf_pages[p][...]
    lf = lf_scr[...]
    lane = _iota2((n_pages, flat), 1)
    cum, page_total = lf, lf
    shift = n_heads
    while shift < flat:
        cum = cum + jnp.where(lane >= shift, pltpu.roll(cum, shift, axis=1), 0.0)
        page_total = page_total + pltpu.roll(page_total, shift, axis=1)
        shift *= 2
    earlier = jnp.where(_iota2((n_pages, n_pages), 1) < _iota2((n_pages, n_pages), 0), 1.0, 0.0).astype(BF16)
    before = _dot_sel_left(earlier, page_total)
    cum = cum + before
    past_total = before[n_pages - 1:n_pages, :] + page_total[n_pages - 1:n_pages, :]

    own_head = (_iota2((n_heads, flat), 1) & (n_heads - 1)) == _iota2((n_heads, flat), 0)
    eye = _iota2((n_heads, n_heads), 0) == _iota2((n_heads, n_heads), 1)
    to_rows = lambda r: jnp.sum(jnp.where(eye, jnp.broadcast_to(r, (n_heads, n_heads)), 0.0), axis=-1, keepdims=True)

    q = q_ref[...]
    qb = q.astype(BF16)
    scores = []
    for p in range(n_pages):
        s = _dot_nt(qb, k_pages[p][...].astype(BF16)) * scale - cum[p:p + 1, :]
        scores.append(jnp.where(own_head, s, -jnp.inf))
    s_new = (jnp.sum(q * kn_ref[...], axis=-1, keepdims=True) * scale
             - (to_rows(past_total[:, :n_heads]) + to_rows(lfn_ref[...])))
    m = s_new
    for s in scores:
        m = jnp.maximum(m, jnp.max(s, axis=-1, keepdims=True))
    denom = jnp.exp(s_new - m)
    acc = denom * vn_ref[...]
    for p in range(n_pages):
        w = jnp.exp(scores[p] - m)
        denom = denom + jnp.sum(w, axis=-1, keepdims=True)
        acc = acc + _dot(w.astype(BF16), v_pages[p][...].astype(BF16))
    o_ref[...] = (acc / denom * _sigmoid(og_ref[...])).astype(o_ref.dtype)


def _fox_attn_decode(page_table, q, k_new, v_new, lf_new, og, k_pool, v_pool, lf_pool):
    nseq, n_heads, _ = q.shape
    n_pages = page_table.shape[1]
    flat = k_pool.shape[1]
    page = flat // n_heads
    tok = pl.BlockSpec((None, n_heads, HEAD_DIM), lambda s, pt: (s, 0, 0))
    pool = [pl.BlockSpec((None, flat, HEAD_DIM), lambda s, pt, p=p: (pt[s, p], 0, 0)) for p in range(n_pages)]
    lf_specs = [pl.BlockSpec((None, 1, flat), lambda s, pt, p=p: (pt[s, p], 0, 0)) for p in range(n_pages)]
    grid_spec = pltpu.PrefetchScalarGridSpec(
        num_scalar_prefetch=1,
        grid=(nseq,),
        in_specs=[tok, tok, tok, pl.BlockSpec((None, 1, n_heads), lambda s, pt: (s, 0, 0)), tok] + pool + pool + lf_specs,
        out_specs=tok,
        scratch_shapes=[pltpu.VMEM((n_pages, flat), F32)],
    )
    return pl.pallas_call(
        functools.partial(_fox_attn_decode_kernel, n_heads=n_heads, page=page, n_pages=n_pages),
        grid_spec=grid_spec,
        out_shape=jax.ShapeDtypeStruct((nseq, n_heads, HEAD_DIM), BF16),
        compiler_params=_cparams("arbitrary"),
        name="fox_attn_decode",
    )(page_table, q, k_new, v_new, lf_new, og, *([k_pool] * n_pages), *([v_pool] * n_pages), *([lf_pool] * n_pages))


def _prepare_weights(norm_mix, norm_ffn, gdn_w_in, gdn_conv_w, gdn_a_log, gdn_dt_bias, gdn_norm, gdn_w_out,
                     fox_w_in, fox_b_f, fox_q_norm, fox_k_norm, fox_w_out, ffn_w_in, ffn_conv_w, ffn_conv_b,
                     ffn_w_out):
    conv_dim = gdn_conv_w.shape[-1]
    vw = gdn_w_out.shape[1]
    fw = fox_w_out.shape[1]
    layers = []
    for i in range(norm_mix.shape[0]):
        j = i // 2
        ffn = dict(nw=norm_ffn[i][None], w1=ffn_w_in[i].astype(BF16), cw=ffn_conv_w[i], cb=ffn_conv_b[i][None],
                   w2=ffn_w_out[i].astype(BF16))
        if i % 2 == 0:
            w = gdn_w_in[j].astype(BF16)
            mixer = dict(nw=norm_mix[i][None], wqkv=w[:, :conv_dim], wz=w[:, conv_dim:conv_dim + vw],
                         wba=w[:, conv_dim + vw:], cw=gdn_conv_w[j], alog=gdn_a_log[j][None],
                         dtb=gdn_dt_bias[j][None], onw=gdn_norm[j][None], wo=gdn_w_out[j].astype(BF16))
        else:
            w = fox_w_in[j].astype(BF16)
            mixer = dict(nw=norm_mix[i][None], wq=w[:, :fw], wk=w[:, fw:2 * fw], wv=w[:, 2 * fw:3 * fw],
                         wg=w[:, 3 * fw:4 * fw], wf=w[:, 4 * fw:], wft=w[:, 4 * fw:].T, bf=fox_b_f[j][None],
                         bft=fox_b_f[j][:, None], qn=fox_q_norm[j][None], kn=fox_k_norm[j][None],
                         wo=fox_w_out[j].astype(BF16))
        layers.append((mixer, ffn))
    return layers


def _trunk_prompt(x, layers):
    bsz, seq, _ = x.shape
    gdn_s, gdn_cb, ks, vs, lfs, ffn_cb = [], [], [], [], [], []
    for i, (m, f) in enumerate(layers):
        if i % 2 == 0:
            conv_dim = m["wqkv"].shape[1]
            q, k, v, z, beta, g, cb = _gdn_in(x, jnp.zeros((bsz, 3, conv_dim), F32), m["nw"], m["wqkv"], m["wz"],
                                              m["wba"], m["cw"], m["alog"], m["dtb"], decode=False)
            o, s = _gdn_core_prompt(q, k, v, z, beta, g, m["onw"])
            gdn_cb.append(cb)
            gdn_s.append(s)
        else:
            q, k, v, og, lf, cum = _fox_in(x, m["nw"], m["wq"], m["wk"], m["wv"], m["wg"], m["wf"], m["wft"],
                                           m["bf"], m["bft"], m["qn"], m["kn"])
            o = _fox_attn_prompt(q, k, v, cum, og)
            n_heads = lf.shape[-1]
            ks.append(k.reshape(bsz, seq, n_heads, HEAD_DIM))
            vs.append(v.reshape(bsz, seq, n_heads, HEAD_DIM))
            lfs.append(lf)
        d_ff = f["w2"].shape[0]
        x, fcb = _out_ffn(x, o, m["wo"], f["nw"], f["w1"], f["cw"], f["cb"], f["w2"],
                          jnp.zeros((bsz, 2, 2 * d_ff), F32), decode=False)
        ffn_cb.append(fcb)
    return x, jnp.stack(gdn_s), jnp.stack(gdn_cb), jnp.stack(ks), jnp.stack(vs), jnp.stack(lfs), jnp.stack(ffn_cb)


def _trunk_decode(x, state_gdn, state_gdn_conv, cache_k, cache_v, cache_logf, state_ffn_conv, page_table, layers):
    nseq = x.shape[0]
    x = x.reshape(1, nseq, x.shape[-1])
    gdn_s, gdn_cb, ks, vs, lfs, ffn_cb = [], [], [], [], [], []
    for i, (m, f) in enumerate(layers):
        j = i // 2
        if i % 2 == 0:
            q, k, v, z, beta, g, cb = _gdn_in(x, state_gdn_conv[j].reshape(1, nseq, -1), m["nw"], m["wqkv"], m["wz"],
                                              m["wba"], m["cw"], m["alog"], m["dtb"], decode=True)
            per_seq = lambda a: a.reshape(nseq, 1, a.shape[-1])
            o, s = _gdn_core_decode(per_seq(q), per_seq(k), per_seq(v), per_seq(z), per_seq(beta), per_seq(g),
                                    m["onw"], state_gdn[j])
            o = o.reshape(1, nseq, -1)
            gdn_cb.append(cb.reshape(state_gdn_conv[j].shape))
            gdn_s.append(s)
        else:
            q, k, v, og, lf, _ = _fox_in(x, m["nw"], m["wq"], m["wk"], m["wv"], m["wg"], m["wf"], m["wft"],
                                         m["bf"], m["bft"], m["qn"], m["kn"])
            n_heads = lf.shape[-1]
            n_pool, page = cache_k.shape[1], cache_k.shape[2]
            heads = lambda a: a.reshape(nseq, n_heads, HEAD_DIM)
            o = _fox_attn_decode(page_table, heads(q), heads(k), heads(v), lf.reshape(nseq, 1, n_heads), heads(og),
                                 cache_k[j].reshape(n_pool, page * n_heads, HEAD_DIM),
                                 cache_v[j].reshape(n_pool, page * n_heads, HEAD_DIM),
                                 cache_logf[j].reshape(n_pool, 1, page * n_heads))
            o = o.reshape(1, nseq, -1)
            ks.append(k.reshape(nseq, 1, n_heads, HEAD_DIM))
            vs.append(v.reshape(nseq, 1, n_heads, HEAD_DIM))
            lfs.append(lf.reshape(nseq, 1, n_heads))
        x, fcb = _out_ffn(x, o, m["wo"], f["nw"], f["w1"], f["cw"], f["cb"], f["w2"],
                          state_ffn_conv[i].reshape(1, nseq, -1), decode=True)
        ffn_cb.append(fcb.reshape(state_ffn_conv[i].shape))
    return (x.reshape(nseq, 1, -1), jnp.stack(gdn_s), jnp.stack(gdn_cb), jnp.stack(ks), jnp.stack(vs),
            jnp.stack(lfs), jnp.stack(ffn_cb))


def kernel(x_prompt, x_sample, state_gdn, state_gdn_conv, cache_k, cache_v, cache_logf, state_ffn_conv, page_table,
           norm_mix, norm_ffn, gdn_w_in, gdn_conv_w, gdn_a_log, gdn_dt_bias, gdn_norm, gdn_w_out, fox_w_in, fox_b_f,
           fox_q_norm, fox_k_norm, fox_w_out, ffn_w_in, ffn_conv_w, ffn_conv_b, ffn_w_out):
    layers = _prepare_weights(norm_mix, norm_ffn, gdn_w_in, gdn_conv_w, gdn_a_log, gdn_dt_bias, gdn_norm, gdn_w_out,
                              fox_w_in, fox_b_f, fox_q_norm, fox_k_norm, fox_w_out, ffn_w_in, ffn_conv_w, ffn_conv_b,
                              ffn_w_out)
    prompt = _trunk_prompt(x_prompt, layers)
    sample = _trunk_decode(x_sample, state_gdn, state_gdn_conv, cache_k, cache_v, cache_logf, state_ffn_conv,
                           page_table, layers)
    return (prompt[0], sample[0]) + prompt[1:] + sample[1:]
```

```python
import functools

import jax
import jax.numpy as jnp
from jax import lax
from jax.experimental import pallas as pl
from jax.experimental.pallas import tpu as pltpu

F32 = jnp.float32
BF16 = jnp.bfloat16

HEAD_DIM = 128
GDN_CHUNK = 64
EPS = 1e-6
V7X_VMEM_LIMIT_BYTES = 56 * 1024 * 1024
ROW_TILE = 512
ATTN_TILE = 512
CONV_COLS = 512
FFN_COLS = 256
GDN_HEADS_PER_STEP = 8
GDN_SEQ_TILE = 1024
GDN_SOLVE_BATCH = 16


def _cparams(*semantics):
    return pltpu.CompilerParams(dimension_semantics=semantics, vmem_limit_bytes=V7X_VMEM_LIMIT_BYTES)


def _resident(shape):
    return pl.BlockSpec(shape, lambda *_: (0,) * len(shape), pipeline_mode=pl.Buffered(1))


def _layer_of(w):
    stacked, layer = w
    shape = stacked.shape[1:]
    return pl.BlockSpec((None,) + shape, lambda *_: (layer,) + (0,) * len(shape), pipeline_mode=pl.Buffered(1))


def _sigmoid(x):
    return 1.0 / (1.0 + jnp.exp(-x))


def _silu(x):
    return x * _sigmoid(x)


def _softplus(x):
    return jnp.maximum(x, 0.0) + jnp.log(1.0 + jnp.exp(-jnp.abs(x)))


def _rmsnorm_rows(x, w):
    return x * lax.rsqrt(jnp.mean(x * x, axis=-1, keepdims=True) + EPS) * w


def _dot(a, b):
    return jnp.dot(a, b, preferred_element_type=F32)


def _dot_nt(a, b):
    return lax.dot_general(a, b, (((1,), (1,)), ((), ())), preferred_element_type=F32)


def _dot_tn(a, b):
    return lax.dot_general(a, b, (((0,), (0,)), ((), ())), preferred_element_type=F32)


def _split3(x):
    hi = x.astype(BF16)
    r = x - hi.astype(F32)
    mid = r.astype(BF16)
    lo = (r - mid.astype(F32)).astype(BF16)
    return hi, mid, lo


def _dot_sel_left(sel, x):
    hi, mid, lo = _split3(x)
    return _dot(sel, hi) + _dot(sel, mid) + _dot(sel, lo)


def _dot_sel_right(x, sel):
    hi, mid, lo = _split3(x)
    return _dot(hi, sel) + _dot(mid, sel) + _dot(lo, sel)


def _dot_f32(a, b):
    return jnp.dot(a, b, preferred_element_type=F32, precision=lax.Precision.HIGHEST)


def _iota2(shape, dim):
    return lax.broadcasted_iota(jnp.int32, shape, dim)


def _gdn_in_kernel(x_ref, nw_ref, w_ref, cw_ref, alog_ref, dtb_ref, buf_ref,
                   q_ref, k_ref, v_ref, z_ref, beta_ref, g_ref, cs_ref, ubuf, carry,
                   *, decode, tm, qk_width, conv_dim, n_vheads):
    z_cols = slice(conv_dim, conv_dim + n_vheads * HEAD_DIM)
    gate_cols = slice(conv_dim + n_vheads * HEAD_DIM, conv_dim + n_vheads * HEAD_DIM + 2 * n_vheads)
    hb = _rmsnorm_rows(x_ref[...], nw_ref[...]).astype(BF16)

    if not decode:
        @pl.when(pl.program_id(1) == 0)
        def _():
            carry[...] = jnp.zeros_like(carry)
            carry[5:8, :] = buf_ref[...]

    wc = CONV_COLS
    ahead = _dot(hb, w_ref[:, 0:wc])
    for j in range(conv_dim // wc):
        c0 = j * wc
        cols = slice(c0, c0 + wc)
        u = ahead
        if j + 1 < conv_dim // wc:
            ahead = _dot(hb, w_ref[:, c0 + wc:c0 + 2 * wc])
        else:
            ahead = _dot(hb, w_ref[:, z_cols])
        if decode:
            b0, b1, b2 = buf_ref[:, 0, cols], buf_ref[:, 1, cols], buf_ref[:, 2, cols]
            y = cw_ref[0:1, cols] * b0 + cw_ref[1:2, cols] * b1 + cw_ref[2:3, cols] * b2 + cw_ref[3:4, cols] * u
            cs_ref[:, 0, cols] = b1
            cs_ref[:, 1, cols] = b2
            cs_ref[:, 2, cols] = u
        else:
            ubuf[0:8, :] = carry[:, cols]
            ubuf[8:8 + tm, :] = u
            y = (cw_ref[3:4, cols] * u + cw_ref[2:3, cols] * ubuf[7:7 + tm, :]
                 + cw_ref[1:2, cols] * ubuf[6:6 + tm, :] + cw_ref[0:1, cols] * ubuf[5:5 + tm, :])
            carry[:, cols] = ubuf[tm:tm + 8, :]
        y = _silu(y)
        for hh in range(wc // HEAD_DIM):
            yh = y[:, hh * HEAD_DIM:(hh + 1) * HEAD_DIM]
            col = c0 + hh * HEAD_DIM
            if col < 2 * qk_width:
                yh = yh * lax.rsqrt(jnp.sum(yh * yh, axis=-1, keepdims=True) + EPS)
            dst, col = ((q_ref, col) if col < qk_width else
                        (k_ref, col - qk_width) if col < 2 * qk_width else (v_ref, col - 2 * qk_width))
            if decode:
                dst[:, col:col + HEAD_DIM] = yh
            else:
                dst[col // HEAD_DIM] = yh
    if not decode:
        cs_ref[...] = carry[5:8, :]

    z = ahead
    if decode:
        z_ref[...] = z
    else:
        for h in range(n_vheads):
            z_ref[h] = z[:, h * HEAD_DIM:(h + 1) * HEAD_DIM]
    ba = _dot(hb, w_ref[:, gate_cols])
    beta_ref[...] = _sigmoid(ba[:, :n_vheads])
    g = -jnp.exp(alog_ref[...]) * _softplus(ba[:, n_vheads:] + dtb_ref[...])
    if decode:
        g_ref[...] = g
    else:
        r, c = _iota2((tm, tm), 0), _iota2((tm, tm), 1)
        same_chunk_lower = jnp.where((r // GDN_CHUNK == c // GDN_CHUNK) & (c <= r), 1.0, 0.0).astype(BF16)
        g_ref[...] = _dot_sel_left(same_chunk_lower, g)


def _gdn_in(x, buf, nw, w_in, cw, alog, dtb, *, decode):
    bsz, seq, dm = x.shape
    conv_dim = cw.shape[1]
    vw = (w_in[0].shape[-1] - conv_dim) * HEAD_DIM // (HEAD_DIM + 2)
    qk_width = (conv_dim - vw) // 2
    n_vheads = vw // HEAD_DIM
    tm = min(ROW_TILE, seq)
    grid = (bsz, seq // tm)
    row = lambda width: pl.BlockSpec((None, tm, width), lambda b, i: (b, i, 0))
    if decode:
        buf_spec = pl.BlockSpec((tm, 3, conv_dim), lambda b, i: (i, 0, 0))
        cs_shape, cs_spec = (seq, 3, conv_dim), buf_spec
    else:
        buf_spec = pl.BlockSpec((None, 3, conv_dim), lambda b, i: (b, 0, 0))
        cs_shape, cs_spec = (bsz, 3, conv_dim), pl.BlockSpec((None, 3, conv_dim), lambda b, i: (b, 0, 0))
    kern = functools.partial(_gdn_in_kernel, decode=decode, tm=tm, qk_width=qk_width, conv_dim=conv_dim,
                             n_vheads=n_vheads)
    if decode:
        wide = lambda width: (row(width), jax.ShapeDtypeStruct((bsz, seq, width), F32))
    else:
        wide = lambda width: (pl.BlockSpec((None, width // HEAD_DIM, tm, HEAD_DIM), lambda b, i: (b, 0, i, 0)),
                              jax.ShapeDtypeStruct((bsz, width // HEAD_DIM, seq, HEAD_DIM), F32))
    (q_spec, q_shape), (v_spec, v_shape) = wide(qk_width), wide(vw)
    gate_shape = jax.ShapeDtypeStruct((bsz, seq, n_vheads), F32)
    return pl.pallas_call(
        kern,
        grid=grid,
        in_specs=[row(dm), _resident((1, dm)), _layer_of(w_in), _resident(cw.shape), _resident((1, n_vheads)),
                  _resident((1, n_vheads)), buf_spec],
        out_specs=[q_spec, q_spec, v_spec, v_spec, row(n_vheads), row(n_vheads), cs_spec],
        out_shape=[q_shape, q_shape, v_shape, v_shape, gate_shape, gate_shape, jax.ShapeDtypeStruct(cs_shape, F32)],
        scratch_shapes=[pltpu.VMEM((tm + 8, CONV_COLS), F32), pltpu.VMEM((8, conv_dim), F32)],
        compiler_params=_cparams("arbitrary", "arbitrary"),
        name="gdn_in_decode" if decode else "gdn_in_prompt",
    )(x, nw, w_in[0], cw, alog, dtb, buf)


def _bdot(a, b):
    return lax.dot_general(a, b, (((2,), (1,)), ((0,), (0,))), preferred_element_type=F32)


def _bdot_nt(a, b):
    return lax.dot_general(a, b, (((2,), (2,)), ((0,), (0,))), preferred_element_type=F32)


def _unit_lower_inverse(n, row, col):
    eye = jnp.where(row == col, 1.0, 0.0)
    inv = eye - jnp.where((row >> 1) == (col >> 1), n, 0.0)
    size = n.shape[-1]
    bits = 2
    while (1 << bits) <= size:
        off_diag = jnp.where((row >> bits) == (col >> bits),
                             jnp.where((row >> (bits - 1)) == (col >> (bits - 1)), 0.0, n), 0.0)
        inv_b = inv.astype(BF16)
        inv = inv - _bdot(inv_b, _bdot(off_diag.astype(BF16), inv_b).astype(BF16))
        bits += 1
    n_hi = n.astype(BF16).astype(F32)
    x_hi = inv.astype(BF16).astype(F32)
    lhs = jnp.concatenate([n_hi, n - n_hi, n_hi], axis=-1).astype(BF16)
    rhs = jnp.concatenate([x_hi, x_hi, inv - x_hi], axis=1).astype(BF16)
    resid = eye - inv - _bdot(lhs, rhs)
    return inv + _bdot(x_hi.astype(BF16), resid.astype(BF16))


def _gdn_core_kernel(q_ref, k_ref, v_ref, z_ref, beta_ref, gcum_ref, nw_ref, o_ref, s_out_ref,
                     s_scr, u_scr, w_scr, att_scr, qg_scr, kd_scr, egl_scr, gb_scr, *, lt, hb, rep, nb):
    c = GDN_CHUNK
    groups = lt // c // nb
    t = pl.program_id(2)
    head0 = pl.program_id(1) * hb
    q_scale = HEAD_DIM ** -0.5

    @pl.when(t == 0)
    def _():
        s_scr[...] = jnp.zeros_like(s_scr)

    row = lax.broadcasted_iota(jnp.int32, (nb, c, c), 1)
    col = lax.broadcasted_iota(jnp.int32, (nb, c, c), 2)
    lower = row >= col
    strict = row > col
    lane = _iota2((nb * c, HEAD_DIM), 1)
    gate_lane = _iota2((nb * c, beta_ref.shape[-1]), 1)
    pick = jnp.where(lane < 3, 1.0, 0.0).astype(BF16).reshape(nb, c, HEAD_DIM)

    def solve(idx, _):
        hh = idx // groups
        rows = pl.ds(pl.multiple_of((idx - hh * groups) * (nb * c), nb * c), nb * c)
        kh = hh // rep
        own = gate_lane == head0 + hh
        bcol = jnp.sum(jnp.where(own, beta_ref[rows, :], 0.0), axis=-1, keepdims=True)
        gcol = jnp.sum(jnp.where(own, gcum_ref[rows, :], 0.0), axis=-1, keepdims=True)
        g_hi = gcol.astype(BF16).astype(F32)
        g_mid = (gcol - g_hi).astype(BF16).astype(F32)
        g_lo = gcol - g_hi - g_mid
        pieces = jnp.where(lane == 0, g_hi, jnp.where(lane == 1, g_mid, jnp.where(lane == 2, g_lo, 0.0)))
        gcum_t = _bdot_nt(pick, pieces.astype(BF16).reshape(nb, c, HEAD_DIM))
        g3 = gcol.reshape(nb, c, 1)
        b3 = bcol.reshape(nb, c, 1)
        decay = jnp.where(lower, jnp.exp(jnp.minimum(g3 - gcum_t, 0.0)), 0.0)
        q3 = (q_ref[kh, rows, :] * q_scale).reshape(nb, c, HEAD_DIM)
        k3 = k_ref[kh, rows, :].reshape(nb, c, HEAD_DIM)
        v3 = v_ref[hh, rows, :].reshape(nb, c, HEAD_DIM)
        kb = k3 * b3
        k_bf = k3.astype(BF16)
        n = jnp.where(strict, _bdot_nt(kb.astype(BF16), k_bf) * decay, 0.0)
        inv = _unit_lower_inverse(n, row, col)
        eg = jnp.exp(g3)
        rhs = jnp.concatenate([v3 * b3, kb * eg], axis=-1).astype(BF16)
        sol = _bdot(inv.astype(BF16), rhs)
        attn = jnp.where(lower, _bdot_nt(q3.astype(BF16), k_bf) * decay, 0.0)
        g_last = g3[:, c - 1:c, :]
        u_scr[hh, rows, :] = sol[:, :, :HEAD_DIM].reshape(nb * c, HEAD_DIM)
        w_scr[hh, rows, :] = sol[:, :, HEAD_DIM:].reshape(nb * c, HEAD_DIM).astype(BF16)
        att_scr[hh, rows, :] = attn.reshape(nb * c, c).astype(BF16)
        qg_scr[hh, rows, :] = (q3 * eg).reshape(nb * c, HEAD_DIM).astype(BF16)
        kd_scr[hh, rows, :] = (k3 * jnp.exp(g_last - g3)).reshape(nb * c, HEAD_DIM).astype(BF16)
        gb_scr[...] = jnp.broadcast_to(gcol, gb_scr.shape)
        chunk0 = pl.multiple_of((idx - hh * groups) * nb, nb)
        egl_scr[hh, pl.ds(chunk0, nb), :] = jnp.exp(gb_scr[pl.ds(c - 1, nb, stride=c), :])
        return 0

    lax.fori_loop(0, hb * groups, solve, 0)

    def step(i, _):
        rows = pl.ds(pl.multiple_of(i * c, c), c)
        heads = range(hb)
        state = [s_scr[hh] for hh in heads]
        res = [_dot(jnp.concatenate([w_scr[hh, rows, :], qg_scr[hh, rows, :]], axis=0), state[hh].astype(BF16))
               for hh in heads]
        v_new = [(u_scr[hh, rows, :] - res[hh][:c]).astype(BF16) for hh in heads]
        o = [res[hh][c:] + _dot(att_scr[hh, rows, :], v_new[hh]) for hh in heads]
        grown = [_dot_tn(kd_scr[hh, rows, :], v_new[hh]) for hh in heads]
        for hh in heads:
            s_scr[hh] = state[hh] * egl_scr[hh, pl.ds(i, 1), :] + grown[hh]
            gated = _rmsnorm_rows(o[hh], nw_ref[...]) * _silu(z_ref[hh, rows, :])
            o_ref[rows, hh * HEAD_DIM:(hh + 1) * HEAD_DIM] = gated.astype(o_ref.dtype)
        return 0

    lax.fori_loop(0, lt // c, step, 0)

    @pl.when(t == pl.num_programs(2) - 1)
    def _():
        s_out_ref[...] = s_scr[...]


def _gdn_core_prompt(q, k, v, z, beta, gcum, nw):
    bsz, n_vheads, seq, _ = v.shape
    rep = n_vheads // q.shape[1]
    hb = min(GDN_HEADS_PER_STEP, n_vheads)
    lt = min(GDN_SEQ_TILE, seq)
    nb = min(GDN_SOLVE_BATCH, lt // GDN_CHUNK)
    heads = lambda n: pl.BlockSpec((None, n, lt, HEAD_DIM), lambda b, h, t: (b, h, t, 0))
    gates = pl.BlockSpec((None, lt, n_vheads), lambda b, h, t: (b, t, 0))
    per_head = lambda width, dtype: pltpu.VMEM((hb, lt, width), dtype)
    return pl.pallas_call(
        functools.partial(_gdn_core_kernel, lt=lt, hb=hb, rep=rep, nb=nb),
        grid=(bsz, n_vheads // hb, seq // lt),
        in_specs=[heads(hb // rep), heads(hb // rep), heads(hb), heads(hb), gates, gates, _resident((1, HEAD_DIM))],
        out_specs=[pl.BlockSpec((None, lt, hb * HEAD_DIM), lambda b, h, t: (b, t, h)),
                   pl.BlockSpec((None, hb, HEAD_DIM, HEAD_DIM), lambda b, h, t: (b, h, 0, 0))],
        out_shape=[jax.ShapeDtypeStruct((bsz, seq, n_vheads * HEAD_DIM), BF16),
                   jax.ShapeDtypeStruct((bsz, n_vheads, HEAD_DIM, HEAD_DIM), F32)],
        scratch_shapes=[pltpu.VMEM((hb, HEAD_DIM, HEAD_DIM), F32), per_head(HEAD_DIM, F32), per_head(HEAD_DIM, BF16),
                        per_head(GDN_CHUNK, BF16), per_head(HEAD_DIM, BF16), per_head(HEAD_DIM, BF16),
                        pltpu.VMEM((hb, lt // GDN_CHUNK, HEAD_DIM), F32), pltpu.VMEM((nb * GDN_CHUNK, HEAD_DIM), F32)],
        compiler_params=_cparams("arbitrary", "arbitrary", "arbitrary"),
        name="gdn_core_prompt",
    )(q, k, v, z, beta, gcum, nw)


def _gdn_core_decode_kernel(q_ref, k_ref, v_ref, z_ref, beta_ref, g_ref, nw_ref, s_ref, o_ref, so_ref,
                            *, n_vheads, rep):
    q_scale = HEAD_DIM ** -0.5
    row = _iota2((8, HEAD_DIM), 0)

    def rows3(a, b, c):
        return jnp.where(row == 0, a, jnp.where(row == 1, b, jnp.where(row == 2, c, 0.0)))

    heads = range(n_vheads)
    lanes = [slice(h * HEAD_DIM, (h + 1) * HEAD_DIM) for h in heads]
    klanes = [slice(h // rep * HEAD_DIM, (h // rep + 1) * HEAD_DIM) for h in heads]
    qh = [q_ref[:, klanes[h]] * q_scale for h in heads]
    kh = [k_ref[:, klanes[h]] for h in heads]
    beta = [beta_ref[:, h:h + 1] for h in heads]
    eg = [jnp.exp(g_ref[:, h:h + 1]) for h in heads]
    res = [_dot(rows3(kh[h] * beta[h] * eg[h], qh[h] * eg[h], 0.0).astype(BF16), s_ref[h].astype(BF16))
           for h in heads]
    v_new = [v_ref[:, lanes[h]] * beta[h] - res[h][0:1, :] for h in heads]
    k_hi = [kh[h].astype(BF16).astype(F32) for h in heads]
    v_hi = [v_new[h].astype(BF16).astype(F32) for h in heads]
    grown = [_dot_tn(rows3(k_hi[h], kh[h] - k_hi[h], k_hi[h]).astype(BF16),
                     rows3(v_hi[h], v_hi[h], v_new[h] - v_hi[h]).astype(BF16)) for h in heads]
    for h in heads:
        so_ref[h] = s_ref[h] * eg[h] + grown[h]
        o = res[h][1:2, :] + jnp.sum(qh[h] * kh[h], axis=-1, keepdims=True) * v_new[h]
        gated = _rmsnorm_rows(o, nw_ref[...]) * _silu(z_ref[:, lanes[h]])
        o_ref[:, lanes[h]] = gated.astype(o_ref.dtype)


def _gdn_core_decode(q, k, v, z, beta, g, nw, state):
    nseq, _, vw = v.shape
    n_vheads = vw // HEAD_DIM
    rep = vw // q.shape[-1]
    row = lambda width: pl.BlockSpec((None, 1, width), lambda s: (s, 0, 0))
    st = pl.BlockSpec((None, n_vheads, HEAD_DIM, HEAD_DIM), lambda s: (s, 0, 0, 0))
    return pl.pallas_call(
        functools.partial(_gdn_core_decode_kernel, n_vheads=n_vheads, rep=rep),
        grid=(nseq,),
        in_specs=[row(q.shape[-1]), row(q.shape[-1]), row(vw), row(vw), row(n_vheads), row(n_vheads),
                  _resident((1, HEAD_DIM)), st],
        out_specs=[row(vw), st],
        out_shape=[jax.ShapeDtypeStruct((nseq, 1, vw), BF16), jax.ShapeDtypeStruct(state.shape, F32)],
        compiler_params=_cparams("arbitrary"),
        name="gdn_core_decode",
    )(q, k, v, z, beta, g, nw, state)


def _out_ffn_kernel(x_ref, o_ref, wo_ref, nw_ref, w1_ref, cw_ref, cb_ref, w2_ref, buf_ref,
                    y_ref, cs_ref, gbuf, ubuf, carry, acc, *, decode, tm, d_ff):
    x1 = x_ref[...] + _dot(o_ref[...], wo_ref[...])
    hb = _rmsnorm_rows(x1, nw_ref[...]).astype(BF16)

    if not decode:
        @pl.when(pl.program_id(1) == 0)
        def _():
            carry[...] = jnp.zeros_like(carry)
            carry[6:8, :] = buf_ref[...]

    def conv(u, scratch, c0, wc):
        cols = slice(c0, c0 + wc)
        if decode:
            b0 = buf_ref[:, 0, cols]
            b1 = buf_ref[:, 1, cols]
            cs_ref[:, 0, cols] = b1
            cs_ref[:, 1, cols] = u
            y = cw_ref[0:1, cols] * b0 + cw_ref[1:2, cols] * b1 + cw_ref[2:3, cols] * u
        else:
            scratch[0:8, :] = carry[:, cols]
            scratch[8:8 + tm, :] = u
            y = (cw_ref[2:3, cols] * u + cw_ref[1:2, cols] * scratch[7:7 + tm, :]
                 + cw_ref[0:1, cols] * scratch[6:6 + tm, :])
            carry[:, cols] = scratch[tm:tm + 8, :]
        return y + cb_ref[:, cols]

    wc = FFN_COLS

    def project(j):
        c0 = j * wc
        return _dot(hb, w1_ref[:, c0:c0 + wc]), _dot(hb, w1_ref[:, d_ff + c0:d_ff + c0 + wc])

    ahead = project(0)
    for j in range(d_ff // wc):
        c0 = j * wc
        pre_gate, pre_up = ahead
        if j + 1 < d_ff // wc:
            ahead = project(j + 1)
        gate = conv(pre_gate, gbuf, c0, wc)
        up = conv(pre_up, ubuf, d_ff + c0, wc)
        act = (_silu(gate) * up).astype(BF16)
        part = _dot(act, w2_ref[c0:c0 + wc, :])
        if j == 0:
            acc[...] = part
        else:
            acc[...] += part
    if not decode:
        cs_ref[...] = carry[6:8, :]
    y_ref[...] = x1 + acc[...]


def _out_ffn(x, o, wo, nw, w1, cw, cb, w2, buf, *, decode):
    bsz, seq, dm = x.shape
    d_ff = w2[0].shape[1]
    tm = min(ROW_TILE, seq)
    row = lambda width: pl.BlockSpec((None, tm, width), lambda b, i: (b, i, 0))
    if decode:
        buf, layer = buf
        buf_spec = pl.BlockSpec((None, tm, 2, 2 * d_ff), lambda b, i: (layer, i, 0, 0))
        cs_shape, cs_spec = (seq, 2, 2 * d_ff), pl.BlockSpec((tm, 2, 2 * d_ff), lambda b, i: (i, 0, 0))
    else:
        buf_spec = pl.BlockSpec((None, 2, 2 * d_ff), lambda b, i: (b, 0, 0))
        cs_shape, cs_spec = (bsz, 2, 2 * d_ff), pl.BlockSpec((None, 2, 2 * d_ff), lambda b, i: (b, 0, 0))
    return pl.pallas_call(
        functools.partial(_out_ffn_kernel, decode=decode, tm=tm, d_ff=d_ff),
        grid=(bsz, seq // tm),
        in_specs=[row(dm), row(o.shape[-1]), _layer_of(wo), _resident((1, dm)), _layer_of(w1),
                  _resident(cw.shape), _resident((1, 2 * d_ff)), _layer_of(w2), buf_spec],
        out_specs=[row(dm), cs_spec],
        out_shape=[jax.ShapeDtypeStruct((bsz, seq, dm), F32), jax.ShapeDtypeStruct(cs_shape, F32)],
        scratch_shapes=[pltpu.VMEM((tm + 8, FFN_COLS), F32), pltpu.VMEM((tm + 8, FFN_COLS), F32),
                        pltpu.VMEM((8, 2 * d_ff), F32), pltpu.VMEM((tm, dm), F32)],
        compiler_params=_cparams("arbitrary", "arbitrary"),
        name="out_ffn_decode" if decode else "out_ffn_prompt",
    )(x, o, wo[0], nw, w1[0], cw, cb, w2[0], buf)


def _fox_in_kernel(x_ref, nw_ref, w_ref, wft_ref, bf_ref, bft_ref, qn_ref, kn_ref,
                   q_ref, k_ref, v_ref, og_ref, lf_ref, cum_ref, carry, *, tm, n_heads):
    width = n_heads * HEAD_DIM
    hb = _rmsnorm_rows(x_ref[...], nw_ref[...]).astype(BF16)
    q = _dot(hb, w_ref[:, 0:width])
    k = _dot(hb, w_ref[:, width:2 * width])
    for h in range(n_heads):
        lanes = slice(h * HEAD_DIM, (h + 1) * HEAD_DIM)
        q_ref[:, lanes] = _rmsnorm_rows(q[:, lanes], qn_ref[...])
    v_ref[...] = _dot(hb, w_ref[:, 2 * width:3 * width])
    for h in range(n_heads):
        lanes = slice(h * HEAD_DIM, (h + 1) * HEAD_DIM)
        k_ref[:, lanes] = _rmsnorm_rows(k[:, lanes], kn_ref[...])
    og_ref[...] = _dot(hb, w_ref[:, 3 * width:4 * width])
    lf_ref[...] = -_softplus(-(_dot(hb, w_ref[:, 4 * width:4 * width + n_heads]) + bf_ref[...]))

    @pl.when(pl.program_id(1) == 0)
    def _():
        carry[...] = jnp.zeros_like(carry)

    lf_t = -_softplus(-(_dot_nt(wft_ref[...], hb) + bft_ref[...]))
    upper = jnp.where(_iota2((tm, tm), 0) <= _iota2((tm, tm), 1), 1.0, 0.0).astype(BF16)
    cum = _dot_sel_right(lf_t, upper) + carry[:, 0:1]
    cum_ref[...] = cum
    carry[...] = jnp.broadcast_to(cum[:, tm - 1:tm], carry.shape)


def _fox_in(x, nw, w_in, wft, bf, bft, qn, kn):
    bsz, seq, dm = x.shape
    n_heads = wft.shape[0]
    width = n_heads * HEAD_DIM
    tm = min(ROW_TILE, seq)
    row = lambda w: pl.BlockSpec((None, tm, w), lambda b, i: (b, i, 0))
    wide = jax.ShapeDtypeStruct((bsz, seq, width), F32)
    return pl.pallas_call(
        functools.partial(_fox_in_kernel, tm=tm, n_heads=n_heads),
        grid=(bsz, seq // tm),
        in_specs=[row(dm), _resident((1, dm)), _layer_of(w_in), _resident(wft.shape), _resident((1, n_heads)),
                  _resident((n_heads, 1)), _resident((1, HEAD_DIM)), _resident((1, HEAD_DIM))],
        out_specs=[row(width), row(width), row(width), row(width), row(n_heads),
                   pl.BlockSpec((None, n_heads, tm), lambda b, i: (b, 0, i))],
        out_shape=[wide, wide, wide, wide, jax.ShapeDtypeStruct((bsz, seq, n_heads), F32),
                   jax.ShapeDtypeStruct((bsz, n_heads, seq), F32)],
        scratch_shapes=[pltpu.VMEM((n_heads, HEAD_DIM), F32)],
        compiler_params=_cparams("arbitrary", "arbitrary"),
        name="fox_in",
    )(x, nw, w_in[0], wft, bf, bft, qn, kn)


def _fox_attn_kernel(q_ref, k_ref, v_ref, cum_ref, og_ref, o_ref, kb_scr, vb_scr, *, tile):
    h = pl.program_id(1)
    qi = pl.program_id(2)
    scale = HEAD_DIM ** -0.5
    exp2_scale = scale * 1.4426950408889634

    @pl.when(qi == 0)
    def _():
        kb_scr[...] = k_ref[...].astype(BF16)
        vb_scr[...] = v_ref[...].astype(BF16)

    qb = q_ref[...].astype(BF16)

    def lane_tiles(x):
        return [x[:, t * HEAD_DIM:(t + 1) * HEAD_DIM] for t in range(tile // HEAD_DIM)]

    def key_block(j, carry, diagonal):
        m_prev, l_prev, acc = carry
        cols = pl.ds(pl.multiple_of(j * tile, tile), tile)
        s = _dot_nt(qb, kb_scr[cols, :]) - cum_ref[pl.ds(h, 1), cols] * (1.0 / scale)
        if diagonal:
            s = jnp.where(_iota2((tile, tile), 1) <= _iota2((tile, tile), 0), s, -jnp.inf)
        m_new = jnp.maximum(m_prev, jnp.max(functools.reduce(jnp.maximum, lane_tiles(s)), axis=-1, keepdims=True))
        alpha = jnp.exp2((m_prev - m_new) * exp2_scale)
        p = jnp.exp2((s - m_new) * exp2_scale)
        l_new = alpha * l_prev + jnp.sum(functools.reduce(jnp.add, lane_tiles(p)), axis=-1, keepdims=True)
        acc = alpha * acc + _dot(p.astype(BF16), vb_scr[cols, :])
        return m_new, l_new, acc

    init = (jnp.full((tile, 1), -jnp.inf, F32), jnp.zeros((tile, 1), F32), jnp.zeros((tile, HEAD_DIM), F32))
    carry = lax.fori_loop(0, qi, lambda j, c: key_block(j, c, False), init)
    _, l_fin, acc = key_block(qi, carry, True)
    o_ref[...] = (acc / l_fin * _sigmoid(og_ref[...])).astype(o_ref.dtype)


def _fox_attn_prompt(q, k, v, cum, og):
    bsz, seq, width = q.shape
    n_heads = width // HEAD_DIM
    tile = min(ATTN_TILE, seq)
    q_spec = pl.BlockSpec((None, tile, HEAD_DIM), lambda b, h, i: (b, i, h))
    kv_spec = pl.BlockSpec((None, seq, HEAD_DIM), lambda b, h, i: (b, 0, h))
    cum_spec = pl.BlockSpec((None, n_heads, seq), lambda b, h, i: (b, 0, 0))
    return pl.pallas_call(
        functools.partial(_fox_attn_kernel, tile=tile),
        grid=(bsz, n_heads, seq // tile),
        in_specs=[q_spec, kv_spec, kv_spec, cum_spec, q_spec],
        out_specs=q_spec,
        out_shape=jax.ShapeDtypeStruct((bsz, seq, width), BF16),
        scratch_shapes=[pltpu.VMEM((seq, HEAD_DIM), BF16), pltpu.VMEM((seq, HEAD_DIM), BF16)],
        compiler_params=_cparams("arbitrary", "arbitrary", "arbitrary"),
        name="fox_attn_prompt",
    )(q, k, v, cum, og)


def _fox_attn_decode_kernel(pt_ref, q_ref, kn_ref, vn_ref, lfn_ref, og_ref, *rest, n_heads, page, n_pages):
    del pt_ref
    k_pages, v_pages, lf_pages = rest[:n_pages], rest[n_pages:2 * n_pages], rest[2 * n_pages:3 * n_pages]
    o_ref, lf_scr = rest[3 * n_pages:]
    flat = n_heads * page
    scale = HEAD_DIM ** -0.5

    for p in range(n_pages):
        lf_scr[p:p + 1, :] = lf_pages[p][...]
    lf = lf_scr[...]
    lane = _iota2((n_pages, flat), 1)
    cum, page_total = lf, lf
    shift = n_heads
    while shift < flat:
        cum = cum + jnp.where(lane >= shift, pltpu.roll(cum, shift, axis=1), 0.0)
        page_total = page_total + pltpu.roll(page_total, shift, axis=1)
        shift *= 2
    earlier = jnp.where(_iota2((n_pages, n_pages), 1) < _iota2((n_pages, n_pages), 0), 1.0, 0.0).astype(BF16)
    before = _dot_sel_left(earlier, page_total)
    cum = cum + before
    past_total = before[n_pages - 1:n_pages, :] + page_total[n_pages - 1:n_pages, :]

    own_head = (_iota2((n_heads, flat), 1) & (n_heads - 1)) == _iota2((n_heads, flat), 0)
    eye = _iota2((n_heads, n_heads), 0) == _iota2((n_heads, n_heads), 1)
    to_rows = lambda r: jnp.sum(jnp.where(eye, jnp.broadcast_to(r, (n_heads, n_heads)), 0.0), axis=-1, keepdims=True)

    q = q_ref[...]
    qb = q.astype(BF16)
    scores = []
    for p in range(n_pages):
        s = _dot_nt(qb, k_pages[p][...].astype(BF16)) * scale - cum[p:p + 1, :]
        scores.append(jnp.where(own_head, s, -jnp.inf))
    s_new = (jnp.sum(q * kn_ref[...], axis=-1, keepdims=True) * scale
             - (to_rows(past_total[:, :n_heads]) + to_rows(lfn_ref[...])))
    m = s_new
    for s in scores:
        m = jnp.maximum(m, jnp.max(s, axis=-1, keepdims=True))
    denom = jnp.exp(s_new - m)
    acc = denom * vn_ref[...]
    for p in range(n_pages):
        w = jnp.exp(scores[p] - m)
        denom = denom + jnp.sum(w, axis=-1, keepdims=True)
        acc = acc + _dot(w.astype(BF16), v_pages[p][...].astype(BF16))
    o_ref[...] = (acc / denom * _sigmoid(og_ref[...])).astype(o_ref.dtype)


def _fox_attn_decode(page_table, q, k_new, v_new, lf_new, og, k_pool, v_pool, lf_pool):
    nseq, n_heads, _ = q.shape
    n_pages = page_table.shape[1]
    flat = k_pool.shape[1]
    page = flat // n_heads
    tok = pl.BlockSpec((None, n_heads, HEAD_DIM), lambda s, pt: (s, 0, 0))
    pool = [pl.BlockSpec((None, flat, HEAD_DIM), lambda s, pt, p=p: (pt[s, p], 0, 0)) for p in range(n_pages)]
    lf_specs = [pl.BlockSpec((None, 1, flat), lambda s, pt, p=p: (pt[s, p], 0, 0)) for p in range(n_pages)]
    grid_spec = pltpu.PrefetchScalarGridSpec(
        num_scalar_prefetch=1,
        grid=(nseq,),
        in_specs=[tok, tok, tok, pl.BlockSpec((None, 1, n_heads), lambda s, pt: (s, 0, 0)), tok] + pool + pool + lf_specs,
        out_specs=tok,
        scratch_shapes=[pltpu.VMEM((n_pages, flat), F32)],
    )
    return pl.pallas_call(
        functools.partial(_fox_attn_decode_kernel, n_heads=n_heads, page=page, n_pages=n_pages),
        grid_spec=grid_spec,
        out_shape=jax.ShapeDtypeStruct((nseq, n_heads, HEAD_DIM), BF16),
        compiler_params=_cparams("arbitrary"),
        name="fox_attn_decode",
    )(page_table, q, k_new, v_new, lf_new, og, *([k_pool] * n_pages), *([v_pool] * n_pages), *([lf_pool] * n_pages))


def _prepare_weights(norm_mix, norm_ffn, gdn_w_in, gdn_conv_w, gdn_a_log, gdn_dt_bias, gdn_norm, gdn_w_out,
                     fox_w_in, fox_b_f, fox_q_norm, fox_k_norm, fox_w_out, ffn_w_in, ffn_conv_w, ffn_conv_b,
                     ffn_w_out):
    fw = fox_w_out.shape[1]
    gdn_in_b, gdn_out_b = gdn_w_in.astype(BF16), gdn_w_out.astype(BF16)
    fox_in_b, fox_out_b = fox_w_in.astype(BF16), fox_w_out.astype(BF16)
    ffn_in_b, ffn_out_b = ffn_w_in.astype(BF16), ffn_w_out.astype(BF16)
    layers = []
    for i in range(norm_mix.shape[0]):
        j = i // 2
        ffn = dict(nw=norm_ffn[i][None], w1=(ffn_in_b, i), cw=ffn_conv_w[i], cb=ffn_conv_b[i][None], w2=(ffn_out_b, i))
        if i % 2 == 0:
            mixer = dict(nw=norm_mix[i][None], w_in=(gdn_in_b, j), cw=gdn_conv_w[j], alog=gdn_a_log[j][None],
                         dtb=gdn_dt_bias[j][None], onw=gdn_norm[j][None], wo=(gdn_out_b, j))
        else:
            mixer = dict(nw=norm_mix[i][None], w_in=(fox_in_b, j), wft=fox_w_in[j][:, 4 * fw:].T.astype(BF16),
                         bf=fox_b_f[j][None], bft=fox_b_f[j][:, None], qn=fox_q_norm[j][None],
                         kn=fox_k_norm[j][None], wo=(fox_out_b, j))
        layers.append((mixer, ffn))
    return layers


def _trunk_prompt(x, layers):
    bsz, seq, _ = x.shape
    gdn_s, gdn_cb, ks, vs, lfs, ffn_cb = [], [], [], [], [], []
    for i, (m, f) in enumerate(layers):
        if i % 2 == 0:
            conv_dim = m["cw"].shape[1]
            q, k, v, z, beta, g, cb = _gdn_in(x, jnp.zeros((bsz, 3, conv_dim), F32), m["nw"], m["w_in"], m["cw"],
                                              m["alog"], m["dtb"], decode=False)
            o, s = _gdn_core_prompt(q, k, v, z, beta, g, m["onw"])
            gdn_cb.append(cb)
            gdn_s.append(s)
        else:
            q, k, v, og, lf, cum = _fox_in(x, m["nw"], m["w_in"], m["wft"], m["bf"], m["bft"], m["qn"], m["kn"])
            o = _fox_attn_prompt(q, k, v, cum, og)
            n_heads = lf.shape[-1]
            ks.append(k.reshape(bsz, seq, n_heads, HEAD_DIM))
            vs.append(v.reshape(bsz, seq, n_heads, HEAD_DIM))
            lfs.append(lf)
        x, fcb = _out_ffn(x, o, m["wo"], f["nw"], f["w1"], f["cw"], f["cb"], f["w2"],
                          jnp.zeros((bsz, 2, f["cw"].shape[1]), F32), decode=False)
        ffn_cb.append(fcb)
    return x, jnp.stack(gdn_s), jnp.stack(gdn_cb), jnp.stack(ks), jnp.stack(vs), jnp.stack(lfs), jnp.stack(ffn_cb)


def _trunk_decode(x, state_gdn, state_gdn_conv, cache_k, cache_v, cache_logf, state_ffn_conv, page_table, layers):
    nseq = x.shape[0]
    x = x.reshape(1, nseq, x.shape[-1])
    gdn_s, gdn_cb, ks, vs, lfs, ffn_cb = [], [], [], [], [], []
    for i, (m, f) in enumerate(layers):
        j = i // 2
        if i % 2 == 0:
            q, k, v, z, beta, g, cb = _gdn_in(x, state_gdn_conv[j], m["nw"], m["w_in"], m["cw"], m["alog"], m["dtb"],
                                              decode=True)
            per_seq = lambda a: a.reshape(nseq, 1, a.shape[-1])
            o, s = _gdn_core_decode(per_seq(q), per_seq(k), per_seq(v), per_seq(z), per_seq(beta), per_seq(g),
                                    m["onw"], state_gdn[j])
            o = o.reshape(1, nseq, -1)
            gdn_cb.append(cb)
            gdn_s.append(s)
        else:
            q, k, v, og, lf, _ = _fox_in(x, m["nw"], m["w_in"], m["wft"], m["bf"], m["bft"], m["qn"], m["kn"])
            n_heads = lf.shape[-1]
            n_pool, page = cache_k.shape[1], cache_k.shape[2]
            heads = lambda a: a.reshape(nseq, n_heads, HEAD_DIM)
            o = _fox_attn_decode(page_table, heads(q), heads(k), heads(v), lf.reshape(nseq, 1, n_heads), heads(og),
                                 cache_k[j].reshape(n_pool, page * n_heads, HEAD_DIM),
                                 cache_v[j].reshape(n_pool, page * n_heads, HEAD_DIM),
                                 cache_logf[j].reshape(n_pool, 1, page * n_heads))
            o = o.reshape(1, nseq, -1)
            ks.append(k.reshape(nseq, 1, n_heads, HEAD_DIM))
            vs.append(v.reshape(nseq, 1, n_heads, HEAD_DIM))
            lfs.append(lf.reshape(nseq, 1, n_heads))
        x, fcb = _out_ffn(x, o, m["wo"], f["nw"], f["w1"], f["cw"], f["cb"], f["w2"],
                          (state_ffn_conv, i), decode=True)
        ffn_cb.append(fcb)
    return (x.reshape(nseq, 1, -1), jnp.stack(gdn_s), jnp.stack(gdn_cb), jnp.stack(ks), jnp.stack(vs),
            jnp.stack(lfs), jnp.stack(ffn_cb))


def kernel(x_prompt, x_sample, state_gdn, state_gdn_conv, cache_k, cache_v, cache_logf, state_ffn_conv, page_table,
           norm_mix, norm_ffn, gdn_w_in, gdn_conv_w, gdn_a_log, gdn_dt_bias, gdn_norm, gdn_w_out, fox_w_in, fox_b_f,
           fox_q_norm, fox_k_norm, fox_w_out, ffn_w_in, ffn_conv_w, ffn_conv_b, ffn_w_out):
    layers = _prepare_weights(norm_mix, norm_ffn, gdn_w_in, gdn_conv_w, gdn_a_log, gdn_dt_bias, gdn_norm, gdn_w_out,
                              fox_w_in, fox_b_f, fox_q_norm, fox_k_norm, fox_w_out, ffn_w_in, ffn_conv_w, ffn_conv_b,
                              ffn_w_out)
    prompt = _trunk_prompt(x_prompt, layers)
    sample = _trunk_decode(x_sample, state_gdn, state_gdn_conv, cache_k, cache_v, cache_logf, state_ffn_conv,
                           page_table, layers)
    return (prompt[0], sample[0]) + prompt[1:] + sample[1:]
```

```python
import functools

import jax
import jax.numpy as jnp
from jax import lax
from jax.experimental import pallas as pl
from jax.experimental.pallas import tpu as pltpu

F32 = jnp.float32
BF16 = jnp.bfloat16

HEAD_DIM = 128
GDN_CHUNK = 64
EPS = 1e-6
V7X_VMEM_LIMIT_BYTES = 56 * 1024 * 1024
ROW_TILE = 512
ATTN_TILE = 512
CONV_COLS = 512
FFN_COLS = 256
GDN_HEADS_PER_STEP = 8
GDN_SEQ_TILE = 1024
GDN_SOLVE_BATCH = 16
GDN_DECODE_SEQS = 2
GDN_SOLVE_HEADS = 2


def _cparams(*semantics):
    return pltpu.CompilerParams(dimension_semantics=semantics, vmem_limit_bytes=V7X_VMEM_LIMIT_BYTES)


def _resident(shape):
    return pl.BlockSpec(shape, lambda *_: (0,) * len(shape), pipeline_mode=pl.Buffered(1))


def _layer_of(w):
    stacked, layer = w
    shape = stacked.shape[1:]
    return pl.BlockSpec((None,) + shape, lambda *_: (layer,) + (0,) * len(shape), pipeline_mode=pl.Buffered(1))


def _sigmoid(x):
    return 1.0 / (1.0 + jnp.exp(-x))


def _silu(x):
    return x * _sigmoid(x)


def _softplus(x):
    return jnp.maximum(x, 0.0) + jnp.log(1.0 + jnp.exp(-jnp.abs(x)))


def _rmsnorm_rows(x, w):
    return x * lax.rsqrt(jnp.mean(x * x, axis=-1, keepdims=True) + EPS) * w


def _dot(a, b):
    return jnp.dot(a, b, preferred_element_type=F32)


def _dot_nt(a, b):
    return lax.dot_general(a, b, (((1,), (1,)), ((), ())), preferred_element_type=F32)


def _dot_tn(a, b):
    return lax.dot_general(a, b, (((0,), (0,)), ((), ())), preferred_element_type=F32)


def _split3(x):
    hi = x.astype(BF16)
    r = x - hi.astype(F32)
    mid = r.astype(BF16)
    lo = (r - mid.astype(F32)).astype(BF16)
    return hi, mid, lo


def _dot_sel_left(sel, x):
    hi, mid, lo = _split3(x)
    return _dot(sel, hi) + _dot(sel, mid) + _dot(sel, lo)


def _dot_sel_right(x, sel):
    hi, mid, lo = _split3(x)
    return _dot(hi, sel) + _dot(mid, sel) + _dot(lo, sel)


def _dot_f32(a, b):
    return jnp.dot(a, b, preferred_element_type=F32, precision=lax.Precision.HIGHEST)


def _iota2(shape, dim):
    return lax.broadcasted_iota(jnp.int32, shape, dim)


def _gdn_in_kernel(x_ref, nw_ref, w_ref, cw_ref, alog_ref, dtb_ref, buf_ref,
                   q_ref, k_ref, v_ref, z_ref, beta_ref, g_ref, cs_ref, ubuf, carry,
                   *, decode, tm, qk_width, conv_dim, n_vheads):
    z_cols = slice(conv_dim, conv_dim + n_vheads * HEAD_DIM)
    gate_cols = slice(conv_dim + n_vheads * HEAD_DIM, conv_dim + n_vheads * HEAD_DIM + 2 * n_vheads)
    hb = _rmsnorm_rows(x_ref[...], nw_ref[...]).astype(BF16)

    if not decode:
        @pl.when(pl.program_id(1) == 0)
        def _():
            carry[...] = jnp.zeros_like(carry)
            carry[5:8, :] = buf_ref[...]

    wc = CONV_COLS
    ahead = _dot(hb, w_ref[:, 0:wc])
    for j in range(conv_dim // wc):
        c0 = j * wc
        cols = slice(c0, c0 + wc)
        u = ahead
        if j + 1 < conv_dim // wc:
            ahead = _dot(hb, w_ref[:, c0 + wc:c0 + 2 * wc])
        else:
            ahead = _dot(hb, w_ref[:, z_cols])
        if decode:
            b0, b1, b2 = buf_ref[:, 0, cols], buf_ref[:, 1, cols], buf_ref[:, 2, cols]
            y = cw_ref[0:1, cols] * b0 + cw_ref[1:2, cols] * b1 + cw_ref[2:3, cols] * b2 + cw_ref[3:4, cols] * u
            cs_ref[:, 0, cols] = b1
            cs_ref[:, 1, cols] = b2
            cs_ref[:, 2, cols] = u
        else:
            ubuf[0:8, :] = carry[:, cols]
            ubuf[8:8 + tm, :] = u
            y = (cw_ref[3:4, cols] * u + cw_ref[2:3, cols] * ubuf[7:7 + tm, :]
                 + cw_ref[1:2, cols] * ubuf[6:6 + tm, :] + cw_ref[0:1, cols] * ubuf[5:5 + tm, :])
            carry[:, cols] = ubuf[tm:tm + 8, :]
        y = _silu(y)
        for hh in range(wc // HEAD_DIM):
            yh = y[:, hh * HEAD_DIM:(hh + 1) * HEAD_DIM]
            col = c0 + hh * HEAD_DIM
            if col < 2 * qk_width:
                yh = yh * lax.rsqrt(jnp.sum(yh * yh, axis=-1, keepdims=True) + EPS)
            dst, col = ((q_ref, col) if col < qk_width else
                        (k_ref, col - qk_width) if col < 2 * qk_width else (v_ref, col - 2 * qk_width))
            if decode:
                dst[:, col:col + HEAD_DIM] = yh
            else:
                dst[col // HEAD_DIM] = yh
    if not decode:
        cs_ref[...] = carry[5:8, :]

    z = ahead
    if decode:
        z_ref[...] = z
    else:
        for h in range(n_vheads):
            z_ref[h] = z[:, h * HEAD_DIM:(h + 1) * HEAD_DIM]
    ba = _dot(hb, w_ref[:, gate_cols])
    beta_ref[...] = _sigmoid(ba[:, :n_vheads])
    g = -jnp.exp(alog_ref[...]) * _softplus(ba[:, n_vheads:] + dtb_ref[...])
    if decode:
        g_ref[...] = g
    else:
        r, c = _iota2((tm, tm), 0), _iota2((tm, tm), 1)
        same_chunk_lower = jnp.where((r // GDN_CHUNK == c // GDN_CHUNK) & (c <= r), 1.0, 0.0).astype(BF16)
        g_ref[...] = _dot_sel_left(same_chunk_lower, g)


def _gdn_in(x, buf, nw, w_in, cw, alog, dtb, *, decode):
    bsz, seq, dm = x.shape
    conv_dim = cw.shape[1]
    vw = (w_in[0].shape[-1] - conv_dim) * HEAD_DIM // (HEAD_DIM + 2)
    qk_width = (conv_dim - vw) // 2
    n_vheads = vw // HEAD_DIM
    tm = min(ROW_TILE, seq)
    grid = (bsz, seq // tm)
    row = lambda width: pl.BlockSpec((None, tm, width), lambda b, i: (b, i, 0))
    if decode:
        buf_spec = pl.BlockSpec((tm, 3, conv_dim), lambda b, i: (i, 0, 0))
        cs_shape, cs_spec = (seq, 3, conv_dim), buf_spec
    else:
        buf_spec = pl.BlockSpec((None, 3, conv_dim), lambda b, i: (b, 0, 0))
        cs_shape, cs_spec = (bsz, 3, conv_dim), pl.BlockSpec((None, 3, conv_dim), lambda b, i: (b, 0, 0))
    kern = functools.partial(_gdn_in_kernel, decode=decode, tm=tm, qk_width=qk_width, conv_dim=conv_dim,
                             n_vheads=n_vheads)
    if decode:
        wide = lambda width: (row(width), jax.ShapeDtypeStruct((bsz, seq, width), F32))
    else:
        wide = lambda width: (pl.BlockSpec((None, width // HEAD_DIM, tm, HEAD_DIM), lambda b, i: (b, 0, i, 0)),
                              jax.ShapeDtypeStruct((bsz, width // HEAD_DIM, seq, HEAD_DIM), F32))
    (q_spec, q_shape), (v_spec, v_shape) = wide(qk_width), wide(vw)
    gate_shape = jax.ShapeDtypeStruct((bsz, seq, n_vheads), F32)
    return pl.pallas_call(
        kern,
        grid=grid,
        in_specs=[row(dm), _resident((1, dm)), _layer_of(w_in), _resident(cw.shape), _resident((1, n_vheads)),
                  _resident((1, n_vheads)), buf_spec],
        out_specs=[q_spec, q_spec, v_spec, v_spec, row(n_vheads), row(n_vheads), cs_spec],
        out_shape=[q_shape, q_shape, v_shape, v_shape, gate_shape, gate_shape, jax.ShapeDtypeStruct(cs_shape, F32)],
        scratch_shapes=[pltpu.VMEM((tm + 8, CONV_COLS), F32), pltpu.VMEM((8, conv_dim), F32)],
        compiler_params=_cparams("arbitrary", "arbitrary"),
        name="gdn_in_decode" if decode else "gdn_in_prompt",
    )(x, nw, w_in[0], cw, alog, dtb, buf)


def _bdot(a, b):
    return lax.dot_general(a, b, (((2,), (1,)), ((0,), (0,))), preferred_element_type=F32)


def _bdot_nt(a, b):
    return lax.dot_general(a, b, (((2,), (2,)), ((0,), (0,))), preferred_element_type=F32)


def _unit_lower_inverse(n, row, col):
    eye = jnp.where(row == col, 1.0, 0.0)
    inv = eye - jnp.where((row >> 1) == (col >> 1), n, 0.0)
    size = n.shape[-1]
    bits = 2
    while (1 << bits) <= size:
        off_diag = jnp.where((row >> bits) == (col >> bits),
                             jnp.where((row >> (bits - 1)) == (col >> (bits - 1)), 0.0, n), 0.0)
        inv_b = inv.astype(BF16)
        inv = inv - _bdot(inv_b, _bdot(off_diag.astype(BF16), inv_b).astype(BF16))
        bits += 1
    n_hi = n.astype(BF16).astype(F32)
    x_hi = inv.astype(BF16).astype(F32)
    lhs = jnp.concatenate([n_hi, n - n_hi, n_hi], axis=-1).astype(BF16)
    rhs = jnp.concatenate([x_hi, x_hi, inv - x_hi], axis=1).astype(BF16)
    resid = eye - inv - _bdot(lhs, rhs)
    return inv + _bdot(x_hi.astype(BF16), resid.astype(BF16))


def _gdn_core_kernel(q_ref, k_ref, v_ref, z_ref, beta_ref, gcum_ref, nw_ref, o_ref, s_out_ref,
                     s_scr, u_scr, w_scr, att_scr, qg_scr, kd_scr, egl_scr, gb_scr, *, lt, hb, rep, nb, hs):
    c = GDN_CHUNK
    groups = lt // c // nb
    t = pl.program_id(2)
    head0 = pl.program_id(1) * hb
    q_scale = HEAD_DIM ** -0.5

    @pl.when(t == 0)
    def _():
        s_scr[...] = jnp.zeros_like(s_scr)

    nbh = nb * hs
    row = lax.broadcasted_iota(jnp.int32, (nbh, c, c), 1)
    col = lax.broadcasted_iota(jnp.int32, (nbh, c, c), 2)
    lower = row >= col
    strict = row > col
    lane = _iota2((nb * c, HEAD_DIM), 1)
    gate_lane = _iota2((nb * c, beta_ref.shape[-1]), 1)
    pick = jnp.where(_iota2((nbh * c, HEAD_DIM), 1) < 3, 1.0, 0.0).astype(BF16).reshape(nbh, c, HEAD_DIM)

    def solve(idx, _):
        hset = idx // groups
        chunk0 = pl.multiple_of((idx - hset * groups) * nb, nb)
        rows = pl.ds(pl.multiple_of(chunk0 * c, nb * c), nb * c)
        heads = [hset * hs + m for m in range(hs)]
        per_head = []
        for hh in heads:
            own = gate_lane == head0 + hh
            bcol = jnp.sum(jnp.where(own, beta_ref[rows, :], 0.0), axis=-1, keepdims=True)
            gcol = jnp.sum(jnp.where(own, gcum_ref[rows, :], 0.0), axis=-1, keepdims=True)
            g_hi = gcol.astype(BF16).astype(F32)
            g_mid = (gcol - g_hi).astype(BF16).astype(F32)
            g_lo = gcol - g_hi - g_mid
            pieces = jnp.where(lane == 0, g_hi, jnp.where(lane == 1, g_mid, jnp.where(lane == 2, g_lo, 0.0)))
            per_head.append((gcol, bcol, pieces.astype(BF16), q_ref[hh // rep, rows, :] * q_scale,
                             k_ref[hh // rep, rows, :], v_ref[hh, rows, :]))
        stack = lambda i, width: jnp.concatenate([p[i].reshape(nb, c, width) for p in per_head], axis=0)
        g3, b3 = stack(0, 1), stack(1, 1)
        gcum_t = _bdot_nt(pick, stack(2, HEAD_DIM))
        decay = jnp.where(lower, jnp.exp(jnp.minimum(g3 - gcum_t, 0.0)), 0.0)
        q3, k3, v3 = stack(3, HEAD_DIM), stack(4, HEAD_DIM), stack(5, HEAD_DIM)
        kb = k3 * b3
        k_bf = k3.astype(BF16)
        n = jnp.where(strict, _bdot_nt(kb.astype(BF16), k_bf) * decay, 0.0)
        inv = _unit_lower_inverse(n, row, col)
        eg = jnp.exp(g3)
        rhs = jnp.concatenate([v3 * b3, kb * eg], axis=-1).astype(BF16)
        sol = _bdot(inv.astype(BF16), rhs)
        attn = jnp.where(lower, _bdot_nt(q3.astype(BF16), k_bf) * decay, 0.0)
        g_last = g3[:, c - 1:c, :]
        qg = q3 * eg
        kd = k3 * jnp.exp(g_last - g3)
        for m, hh in enumerate(heads):
            mine = lambda a: a[m * nb:(m + 1) * nb].reshape(nb * c, a.shape[-1])
            u_scr[hh, rows, :] = mine(sol[:, :, :HEAD_DIM])
            w_scr[hh, rows, :] = mine(sol[:, :, HEAD_DIM:]).astype(BF16)
            att_scr[hh, rows, :] = mine(attn).astype(BF16)
            qg_scr[hh, rows, :] = mine(qg).astype(BF16)
            kd_scr[hh, rows, :] = mine(kd).astype(BF16)
            gb_scr[m] = jnp.broadcast_to(per_head[m][0], gb_scr.shape[1:])
            egl_scr[hh, pl.ds(chunk0, nb), :] = jnp.exp(gb_scr[m, pl.ds(c - 1, nb, stride=c), :])
        return 0

    lax.fori_loop(0, hb // hs * groups, solve, 0)

    def step(i, _):
        rows = pl.ds(pl.multiple_of(i * c, c), c)
        heads = range(hb)
        state = [s_scr[hh] for hh in heads]
        res = [_dot(jnp.concatenate([w_scr[hh, rows, :], qg_scr[hh, rows, :]], axis=0), state[hh].astype(BF16))
               for hh in heads]
        v_new = [(u_scr[hh, rows, :] - res[hh][:c]).astype(BF16) for hh in heads]
        o = [res[hh][c:] + _dot(att_scr[hh, rows, :], v_new[hh]) for hh in heads]
        grown = [_dot_tn(kd_scr[hh, rows, :], v_new[hh]) for hh in heads]
        for hh in heads:
            s_scr[hh] = state[hh] * egl_scr[hh, pl.ds(i, 1), :] + grown[hh]
            gated = _rmsnorm_rows(o[hh], nw_ref[...]) * _silu(z_ref[hh, rows, :])
            o_ref[rows, hh * HEAD_DIM:(hh + 1) * HEAD_DIM] = gated.astype(o_ref.dtype)
        return 0

    lax.fori_loop(0, lt // c, step, 0)

    @pl.when(t == pl.num_programs(2) - 1)
    def _():
        s_out_ref[...] = s_scr[...]


def _gdn_core_prompt(q, k, v, z, beta, gcum, nw):
    bsz, n_vheads, seq, _ = v.shape
    rep = n_vheads // q.shape[1]
    hb = min(GDN_HEADS_PER_STEP, n_vheads)
    lt = min(GDN_SEQ_TILE, seq)
    nb = min(GDN_SOLVE_BATCH, lt // GDN_CHUNK)
    hs = min(GDN_SOLVE_HEADS, hb)
    heads = lambda n: pl.BlockSpec((None, n, lt, HEAD_DIM), lambda b, h, t: (b, h, t, 0))
    gates = pl.BlockSpec((None, lt, n_vheads), lambda b, h, t: (b, t, 0))
    per_head = lambda width, dtype: pltpu.VMEM((hb, lt, width), dtype)
    return pl.pallas_call(
        functools.partial(_gdn_core_kernel, lt=lt, hb=hb, rep=rep, nb=nb, hs=hs),
        grid=(bsz, n_vheads // hb, seq // lt),
        in_specs=[heads(hb // rep), heads(hb // rep), heads(hb), heads(hb), gates, gates, _resident((1, HEAD_DIM))],
        out_specs=[pl.BlockSpec((None, lt, hb * HEAD_DIM), lambda b, h, t: (b, t, h)),
                   pl.BlockSpec((None, hb, HEAD_DIM, HEAD_DIM), lambda b, h, t: (b, h, 0, 0))],
        out_shape=[jax.ShapeDtypeStruct((bsz, seq, n_vheads * HEAD_DIM), BF16),
                   jax.ShapeDtypeStruct((bsz, n_vheads, HEAD_DIM, HEAD_DIM), F32)],
        scratch_shapes=[pltpu.VMEM((hb, HEAD_DIM, HEAD_DIM), F32), per_head(HEAD_DIM, F32), per_head(HEAD_DIM, BF16),
                        per_head(GDN_CHUNK, BF16), per_head(HEAD_DIM, BF16), per_head(HEAD_DIM, BF16),
                        pltpu.VMEM((hb, lt // GDN_CHUNK, HEAD_DIM), F32),
                        pltpu.VMEM((hs, nb * GDN_CHUNK, HEAD_DIM), F32)],
        compiler_params=_cparams("arbitrary", "arbitrary", "arbitrary"),
        name="gdn_core_prompt",
    )(q, k, v, z, beta, gcum, nw)


def _gdn_core_decode_kernel(q_ref, k_ref, v_ref, z_ref, beta_ref, g_ref, nw_ref, s_ref, o_ref, so_ref,
                            *, n_vheads, rep):
    q_scale = HEAD_DIM ** -0.5
    row = _iota2((8, HEAD_DIM), 0)

    def rows3(a, b, c):
        return jnp.where(row == 0, a, jnp.where(row == 1, b, jnp.where(row == 2, c, 0.0)))

    pairs = [(s, h) for s in range(q_ref.shape[0]) for h in range(n_vheads)]
    idx = range(len(pairs))
    lanes = [slice(h * HEAD_DIM, (h + 1) * HEAD_DIM) for _, h in pairs]
    klanes = [slice(h // rep * HEAD_DIM, (h // rep + 1) * HEAD_DIM) for _, h in pairs]
    qh = [q_ref[s, :, klanes[i]] * q_scale for i, (s, _) in enumerate(pairs)]
    kh = [k_ref[s, :, klanes[i]] for i, (s, _) in enumerate(pairs)]
    beta = [beta_ref[s, :, h:h + 1] for s, h in pairs]
    eg = [jnp.exp(g_ref[s, :, h:h + 1]) for s, h in pairs]
    res = [_dot(rows3(kh[i] * beta[i] * eg[i], qh[i] * eg[i], 0.0).astype(BF16), s_ref[pairs[i]].astype(BF16))
           for i in idx]
    v_new = [v_ref[pairs[i][0], :, lanes[i]] * beta[i] - res[i][0:1, :] for i in idx]
    k_hi = [kh[i].astype(BF16).astype(F32) for i in idx]
    v_hi = [v_new[i].astype(BF16).astype(F32) for i in idx]
    grown = [_dot_tn(rows3(k_hi[i], kh[i] - k_hi[i], k_hi[i]).astype(BF16),
                     rows3(v_hi[i], v_hi[i], v_new[i] - v_hi[i]).astype(BF16)) for i in idx]
    for i in idx:
        s = pairs[i][0]
        so_ref[pairs[i]] = s_ref[pairs[i]] * eg[i] + grown[i]
        o = res[i][1:2, :] + jnp.sum(qh[i] * kh[i], axis=-1, keepdims=True) * v_new[i]
        gated = _rmsnorm_rows(o, nw_ref[...]) * _silu(z_ref[s, :, lanes[i]])
        o_ref[s, :, lanes[i]] = gated.astype(o_ref.dtype)


def _gdn_core_decode(q, k, v, z, beta, g, nw, state):
    nseq, _, vw = v.shape
    n_vheads = vw // HEAD_DIM
    rep = vw // q.shape[-1]
    sb = GDN_DECODE_SEQS if nseq % GDN_DECODE_SEQS == 0 else 1
    row = lambda width: pl.BlockSpec((sb, 1, width), lambda s: (s, 0, 0))
    st = pl.BlockSpec((sb, n_vheads, HEAD_DIM, HEAD_DIM), lambda s: (s, 0, 0, 0))
    return pl.pallas_call(
        functools.partial(_gdn_core_decode_kernel, n_vheads=n_vheads, rep=rep),
        grid=(nseq // sb,),
        in_specs=[row(q.shape[-1]), row(q.shape[-1]), row(vw), row(vw), row(n_vheads), row(n_vheads),
                  _resident((1, HEAD_DIM)), st],
        out_specs=[row(vw), st],
        out_shape=[jax.ShapeDtypeStruct((nseq, 1, vw), BF16), jax.ShapeDtypeStruct(state.shape, F32)],
        compiler_params=_cparams("arbitrary"),
        name="gdn_core_decode",
    )(q, k, v, z, beta, g, nw, state)


def _out_ffn_kernel(x_ref, o_ref, wo_ref, nw_ref, w1_ref, cw_ref, cb_ref, w2_ref, buf_ref,
                    y_ref, cs_ref, gbuf, ubuf, carry, act, *, decode, tm, d_ff):
    x1 = x_ref[...] + _dot(o_ref[...], wo_ref[...])
    hb = _rmsnorm_rows(x1, nw_ref[...]).astype(BF16)

    if not decode:
        @pl.when(pl.program_id(1) == 0)
        def _():
            carry[...] = jnp.zeros_like(carry)
            carry[6:8, :] = buf_ref[...]

    def conv(u, scratch, c0, wc):
        cols = slice(c0, c0 + wc)
        if decode:
            b0 = buf_ref[:, 0, cols]
            b1 = buf_ref[:, 1, cols]
            cs_ref[:, 0, cols] = b1
            cs_ref[:, 1, cols] = u
            y = cw_ref[0:1, cols] * b0 + cw_ref[1:2, cols] * b1 + cw_ref[2:3, cols] * u
        else:
            scratch[0:8, :] = carry[:, cols]
            scratch[8:8 + tm, :] = u
            y = (cw_ref[2:3, cols] * u + cw_ref[1:2, cols] * scratch[7:7 + tm, :]
                 + cw_ref[0:1, cols] * scratch[6:6 + tm, :])
            carry[:, cols] = scratch[tm:tm + 8, :]
        return y + cb_ref[:, cols]

    wc = FFN_COLS

    def project(j):
        c0 = j * wc
        return _dot(hb, w1_ref[:, c0:c0 + wc]), _dot(hb, w1_ref[:, d_ff + c0:d_ff + c0 + wc])

    ahead = project(0)
    for j in range(d_ff // wc):
        c0 = j * wc
        pre_gate, pre_up = ahead
        if j + 1 < d_ff // wc:
            ahead = project(j + 1)
        gate = conv(pre_gate, gbuf, c0, wc)
        up = conv(pre_up, ubuf, d_ff + c0, wc)
        act[:, c0:c0 + wc] = (_silu(gate) * up).astype(BF16)
    if not decode:
        cs_ref[...] = carry[6:8, :]
    y_ref[...] = x1 + _dot(act[...], w2_ref[...])


def _out_ffn(x, o, wo, nw, w1, cw, cb, w2, buf, *, decode):
    bsz, seq, dm = x.shape
    d_ff = w2[0].shape[1]
    tm = min(ROW_TILE, seq)
    row = lambda width: pl.BlockSpec((None, tm, width), lambda b, i: (b, i, 0))
    if decode:
        buf, layer = buf
        buf_spec = pl.BlockSpec((None, tm, 2, 2 * d_ff), lambda b, i: (layer, i, 0, 0))
        cs_shape, cs_spec = (seq, 2, 2 * d_ff), pl.BlockSpec((tm, 2, 2 * d_ff), lambda b, i: (i, 0, 0))
    else:
        buf_spec = pl.BlockSpec((None, 2, 2 * d_ff), lambda b, i: (b, 0, 0))
        cs_shape, cs_spec = (bsz, 2, 2 * d_ff), pl.BlockSpec((None, 2, 2 * d_ff), lambda b, i: (b, 0, 0))
    return pl.pallas_call(
        functools.partial(_out_ffn_kernel, decode=decode, tm=tm, d_ff=d_ff),
        grid=(bsz, seq // tm),
        in_specs=[row(dm), row(o.shape[-1]), _layer_of(wo), _resident((1, dm)), _layer_of(w1),
                  _resident(cw.shape), _resident((1, 2 * d_ff)), _layer_of(w2), buf_spec],
        out_specs=[row(dm), cs_spec],
        out_shape=[jax.ShapeDtypeStruct((bsz, seq, dm), F32), jax.ShapeDtypeStruct(cs_shape, F32)],
        scratch_shapes=[pltpu.VMEM((tm + 8, FFN_COLS), F32), pltpu.VMEM((tm + 8, FFN_COLS), F32),
                        pltpu.VMEM((8, 2 * d_ff), F32), pltpu.VMEM((tm, d_ff), BF16)],
        compiler_params=_cparams("arbitrary", "arbitrary"),
        name="out_ffn_decode" if decode else "out_ffn_prompt",
    )(x, o, wo[0], nw, w1[0], cw, cb, w2[0], buf)


def _fox_in_kernel(x_ref, nw_ref, w_ref, wft_ref, bf_ref, bft_ref, qn_ref, kn_ref,
                   q_ref, k_ref, v_ref, og_ref, lf_ref, cum_ref, carry, *, tm, n_heads):
    width = n_heads * HEAD_DIM
    hb = _rmsnorm_rows(x_ref[...], nw_ref[...]).astype(BF16)
    q = _dot(hb, w_ref[:, 0:width])
    k = _dot(hb, w_ref[:, width:2 * width])
    for h in range(n_heads):
        lanes = slice(h * HEAD_DIM, (h + 1) * HEAD_DIM)
        q_ref[:, lanes] = _rmsnorm_rows(q[:, lanes], qn_ref[...])
    v_ref[...] = _dot(hb, w_ref[:, 2 * width:3 * width])
    for h in range(n_heads):
        lanes = slice(h * HEAD_DIM, (h + 1) * HEAD_DIM)
        k_ref[:, lanes] = _rmsnorm_rows(k[:, lanes], kn_ref[...])
    og_ref[...] = _dot(hb, w_ref[:, 3 * width:4 * width])
    lf_ref[...] = -_softplus(-(_dot(hb, w_ref[:, 4 * width:4 * width + n_heads]) + bf_ref[...]))

    @pl.when(pl.program_id(1) == 0)
    def _():
        carry[...] = jnp.zeros_like(carry)

    lf_t = -_softplus(-(_dot_nt(wft_ref[...], hb) + bft_ref[...]))
    upper = jnp.where(_iota2((tm, tm), 0) <= _iota2((tm, tm), 1), 1.0, 0.0).astype(BF16)
    cum = _dot_sel_right(lf_t, upper) + carry[:, 0:1]
    cum_ref[...] = cum
    carry[...] = jnp.broadcast_to(cum[:, tm - 1:tm], carry.shape)


def _fox_in(x, nw, w_in, wft, bf, bft, qn, kn):
    bsz, seq, dm = x.shape
    n_heads = wft.shape[0]
    width = n_heads * HEAD_DIM
    tm = min(ROW_TILE, seq)
    row = lambda w: pl.BlockSpec((None, tm, w), lambda b, i: (b, i, 0))
    wide = jax.ShapeDtypeStruct((bsz, seq, width), F32)
    return pl.pallas_call(
        functools.partial(_fox_in_kernel, tm=tm, n_heads=n_heads),
        grid=(bsz, seq // tm),
        in_specs=[row(dm), _resident((1, dm)), _layer_of(w_in), _resident(wft.shape), _resident((1, n_heads)),
                  _resident((n_heads, 1)), _resident((1, HEAD_DIM)), _resident((1, HEAD_DIM))],
        out_specs=[row(width), row(width), row(width), row(width), row(n_heads),
                   pl.BlockSpec((None, n_heads, tm), lambda b, i: (b, 0, i))],
        out_shape=[wide, wide, wide, wide, jax.ShapeDtypeStruct((bsz, seq, n_heads), F32),
                   jax.ShapeDtypeStruct((bsz, n_heads, seq), F32)],
        scratch_shapes=[pltpu.VMEM((n_heads, HEAD_DIM), F32)],
        compiler_params=_cparams("arbitrary", "arbitrary"),
        name="fox_in",
    )(x, nw, w_in[0], wft, bf, bft, qn, kn)


def _fox_attn_kernel(q_ref, k_ref, v_ref, cum_ref, og_ref, o_ref, kb_scr, vb_scr, *, tile):
    h = pl.program_id(1)
    qi = pl.program_id(2)
    scale = HEAD_DIM ** -0.5
    exp2_scale = scale * 1.4426950408889634

    @pl.when(qi == 0)
    def _():
        kb_scr[...] = k_ref[...].astype(BF16)
        vb_scr[...] = v_ref[...].astype(BF16)

    qb = q_ref[...].astype(BF16)

    def lane_tiles(x):
        return [x[:, t * HEAD_DIM:(t + 1) * HEAD_DIM] for t in range(tile // HEAD_DIM)]

    def key_block(j, carry, diagonal):
        m_prev, l_prev, acc = carry
        cols = pl.ds(pl.multiple_of(j * tile, tile), tile)
        s = _dot_nt(qb, kb_scr[cols, :]) - cum_ref[pl.ds(h, 1), cols] * (1.0 / scale)
        if diagonal:
            s = jnp.where(_iota2((tile, tile), 1) <= _iota2((tile, tile), 0), s, -jnp.inf)
        m_new = jnp.maximum(m_prev, jnp.max(functools.reduce(jnp.maximum, lane_tiles(s)), axis=-1, keepdims=True))
        alpha = jnp.exp2((m_prev - m_new) * exp2_scale)
        p = jnp.exp2((s - m_new) * exp2_scale)
        l_new = alpha * l_prev + jnp.sum(functools.reduce(jnp.add, lane_tiles(p)), axis=-1, keepdims=True)
        acc = alpha * acc + _dot(p.astype(BF16), vb_scr[cols, :])
        return m_new, l_new, acc

    init = (jnp.full((tile, 1), -jnp.inf, F32), jnp.zeros((tile, 1), F32), jnp.zeros((tile, HEAD_DIM), F32))
    carry = lax.fori_loop(0, qi, lambda j, c: key_block(j, c, False), init)
    _, l_fin, acc = key_block(qi, carry, True)
    o_ref[...] = (acc / l_fin * _sigmoid(og_ref[...])).astype(o_ref.dtype)


def _fox_attn_prompt(q, k, v, cum, og):
    bsz, seq, width = q.shape
    n_heads = width // HEAD_DIM
    tile = min(ATTN_TILE, seq)
    q_spec = pl.BlockSpec((None, tile, HEAD_DIM), lambda b, h, i: (b, i, h))
    kv_spec = pl.BlockSpec((None, seq, HEAD_DIM), lambda b, h, i: (b, 0, h))
    cum_spec = pl.BlockSpec((None, n_heads, seq), lambda b, h, i: (b, 0, 0))
    return pl.pallas_call(
        functools.partial(_fox_attn_kernel, tile=tile),
        grid=(bsz, n_heads, seq // tile),
        in_specs=[q_spec, kv_spec, kv_spec, cum_spec, q_spec],
        out_specs=q_spec,
        out_shape=jax.ShapeDtypeStruct((bsz, seq, width), BF16),
        scratch_shapes=[pltpu.VMEM((seq, HEAD_DIM), BF16), pltpu.VMEM((seq, HEAD_DIM), BF16)],
        compiler_params=_cparams("arbitrary", "arbitrary", "arbitrary"),
        name="fox_attn_prompt",
    )(q, k, v, cum, og)


def _fox_attn_decode_kernel(pt_ref, q_ref, kn_ref, vn_ref, lfn_ref, og_ref, *rest, n_heads, page, n_pages):
    del pt_ref
    k_pages, v_pages, lf_pages = rest[:n_pages], rest[n_pages:2 * n_pages], rest[2 * n_pages:3 * n_pages]
    o_ref, lf_scr = rest[3 * n_pages:]
    flat = n_heads * page
    scale = HEAD_DIM ** -0.5

    for p in range(n_pages):
        lf_scr[p:p + 1, :] = lf_pages[p][...]
    lf = lf_scr[...]
    lane = _iota2((n_pages, flat), 1)
    cum, page_total = lf, lf
    shift = n_heads
    while shift < flat:
        cum = cum + jnp.where(lane >= shift, pltpu.roll(cum, shift, axis=1), 0.0)
        page_total = page_total + pltpu.roll(page_total, shift, axis=1)
        shift *= 2
    earlier = jnp.where(_iota2((n_pages, n_pages), 1) < _iota2((n_pages, n_pages), 0), 1.0, 0.0).astype(BF16)
    before = _dot_sel_left(earlier, page_total)
    cum = cum + before
    past_total = before[n_pages - 1:n_pages, :] + page_total[n_pages - 1:n_pages, :]

    own_head = (_iota2((n_heads, flat), 1) & (n_heads - 1)) == _iota2((n_heads, flat), 0)
    eye = _iota2((n_heads, n_heads), 0) == _iota2((n_heads, n_heads), 1)
    to_rows = lambda r: jnp.sum(jnp.where(eye, jnp.broadcast_to(r, (n_heads, n_heads)), 0.0), axis=-1, keepdims=True)

    q = q_ref[...]
    qb = q.astype(BF16)
    scores = []
    for p in range(n_pages):
        s = _dot_nt(qb, k_pages[p][...].astype(BF16)) * scale - cum[p:p + 1, :]
        scores.append(jnp.where(own_head, s, -jnp.inf))
    s_new = (jnp.sum(q * kn_ref[...], axis=-1, keepdims=True) * scale
             - (to_rows(past_total[:, :n_heads]) + to_rows(lfn_ref[...])))
    m = s_new
    for s in scores:
        m = jnp.maximum(m, jnp.max(s, axis=-1, keepdims=True))
    denom = jnp.exp(s_new - m)
    acc = denom * vn_ref[...]
    for p in range(n_pages):
        w = jnp.exp(scores[p] - m)
        denom = denom + jnp.sum(w, axis=-1, keepdims=True)
        acc = acc + _dot(w.astype(BF16), v_pages[p][...].astype(BF16))
    o_ref[...] = (acc / denom * _sigmoid(og_ref[...])).astype(o_ref.dtype)


def _fox_attn_decode(page_table, q, k_new, v_new, lf_new, og, k_pool, v_pool, lf_pool):
    nseq, n_heads, _ = q.shape
    n_pages = page_table.shape[1]
    flat = k_pool.shape[1]
    page = flat // n_heads
    tok = pl.BlockSpec((None, n_heads, HEAD_DIM), lambda s, pt: (s, 0, 0))
    pool = [pl.BlockSpec((None, flat, HEAD_DIM), lambda s, pt, p=p: (pt[s, p], 0, 0)) for p in range(n_pages)]
    lf_specs = [pl.BlockSpec((None, 1, flat), lambda s, pt, p=p: (pt[s, p], 0, 0)) for p in range(n_pages)]
    grid_spec = pltpu.PrefetchScalarGridSpec(
        num_scalar_prefetch=1,
        grid=(nseq,),
        in_specs=[tok, tok, tok, pl.BlockSpec((None, 1, n_heads), lambda s, pt: (s, 0, 0)), tok] + pool + pool + lf_specs,
        out_specs=tok,
        scratch_shapes=[pltpu.VMEM((n_pages, flat), F32)],
    )
    return pl.pallas_call(
        functools.partial(_fox_attn_decode_kernel, n_heads=n_heads, page=page, n_pages=n_pages),
        grid_spec=grid_spec,
        out_shape=jax.ShapeDtypeStruct((nseq, n_heads, HEAD_DIM), BF16),
        compiler_params=_cparams("arbitrary"),
        name="fox_attn_decode",
    )(page_table, q, k_new, v_new, lf_new, og, *([k_pool] * n_pages), *([v_pool] * n_pages), *([lf_pool] * n_pages))


def _prepare_weights(norm_mix, norm_ffn, gdn_w_in, gdn_conv_w, gdn_a_log, gdn_dt_bias, gdn_norm, gdn_w_out,
                     fox_w_in, fox_b_f, fox_q_norm, fox_k_norm, fox_w_out, ffn_w_in, ffn_conv_w, ffn_conv_b,
                     ffn_w_out):
    fw = fox_w_out.shape[1]
    gdn_in_b, gdn_out_b = gdn_w_in.astype(BF16), gdn_w_out.astype(BF16)
    fox_in_b, fox_out_b = fox_w_in.astype(BF16), fox_w_out.astype(BF16)
    ffn_in_b, ffn_out_b = ffn_w_in.astype(BF16), ffn_w_out.astype(BF16)
    layers = []
    for i in range(norm_mix.shape[0]):
        j = i // 2
        ffn = dict(nw=norm_ffn[i][None], w1=(ffn_in_b, i), cw=ffn_conv_w[i], cb=ffn_conv_b[i][None], w2=(ffn_out_b, i))
        if i % 2 == 0:
            mixer = dict(nw=norm_mix[i][None], w_in=(gdn_in_b, j), cw=gdn_conv_w[j], alog=gdn_a_log[j][None],
                         dtb=gdn_dt_bias[j][None], onw=gdn_norm[j][None], wo=(gdn_out_b, j))
        else:
            mixer = dict(nw=norm_mix[i][None], w_in=(fox_in_b, j), wft=fox_w_in[j][:, 4 * fw:].T.astype(BF16),
                         bf=fox_b_f[j][None], bft=fox_b_f[j][:, None], qn=fox_q_norm[j][None],
                         kn=fox_k_norm[j][None], wo=(fox_out_b, j))
        layers.append((mixer, ffn))
    return layers


def _trunk_prompt(x, layers):
    bsz, seq, _ = x.shape
    gdn_s, gdn_cb, ks, vs, lfs, ffn_cb = [], [], [], [], [], []
    for i, (m, f) in enumerate(layers):
        if i % 2 == 0:
            conv_dim = m["cw"].shape[1]
            q, k, v, z, beta, g, cb = _gdn_in(x, jnp.zeros((bsz, 3, conv_dim), F32), m["nw"], m["w_in"], m["cw"],
                                              m["alog"], m["dtb"], decode=False)
            o, s = _gdn_core_prompt(q, k, v, z, beta, g, m["onw"])
            gdn_cb.append(cb)
            gdn_s.append(s)
        else:
            q, k, v, og, lf, cum = _fox_in(x, m["nw"], m["w_in"], m["wft"], m["bf"], m["bft"], m["qn"], m["kn"])
            o = _fox_attn_prompt(q, k, v, cum, og)
            n_heads = lf.shape[-1]
            ks.append(k.reshape(bsz, seq, n_heads, HEAD_DIM))
            vs.append(v.reshape(bsz, seq, n_heads, HEAD_DIM))
            lfs.append(lf)
        x, fcb = _out_ffn(x, o, m["wo"], f["nw"], f["w1"], f["cw"], f["cb"], f["w2"],
                          jnp.zeros((bsz, 2, f["cw"].shape[1]), F32), decode=False)
        ffn_cb.append(fcb)
    return x, jnp.stack(gdn_s), jnp.stack(gdn_cb), jnp.stack(ks), jnp.stack(vs), jnp.stack(lfs), jnp.stack(ffn_cb)


def _trunk_decode(x, state_gdn, state_gdn_conv, cache_k, cache_v, cache_logf, state_ffn_conv, page_table, layers):
    nseq = x.shape[0]
    x = x.reshape(1, nseq, x.shape[-1])
    gdn_s, gdn_cb, ks, vs, lfs, ffn_cb = [], [], [], [], [], []
    for i, (m, f) in enumerate(layers):
        j = i // 2
        if i % 2 == 0:
            q, k, v, z, beta, g, cb = _gdn_in(x, state_gdn_conv[j], m["nw"], m["w_in"], m["cw"], m["alog"], m["dtb"],
                                              decode=True)
            per_seq = lambda a: a.reshape(nseq, 1, a.shape[-1])
            o, s = _gdn_core_decode(per_seq(q), per_seq(k), per_seq(v), per_seq(z), per_seq(beta), per_seq(g),
                                    m["onw"], state_gdn[j])
            o = o.reshape(1, nseq, -1)
            gdn_cb.append(cb)
            gdn_s.append(s)
        else:
            q, k, v, og, lf, _ = _fox_in(x, m["nw"], m["w_in"], m["wft"], m["bf"], m["bft"], m["qn"], m["kn"])
            n_heads = lf.shape[-1]
            n_pool, page = cache_k.shape[1], cache_k.shape[2]
            heads = lambda a: a.reshape(nseq, n_heads, HEAD_DIM)
            o = _fox_attn_decode(page_table, heads(q), heads(k), heads(v), lf.reshape(nseq, 1, n_heads), heads(og),
                                 cache_k[j].reshape(n_pool, page * n_heads, HEAD_DIM),
                                 cache_v[j].reshape(n_pool, page * n_heads, HEAD_DIM),
                                 cache_logf[j].reshape(n_pool, 1, page * n_heads))
            o = o.reshape(1, nseq, -1)
            ks.append(k.reshape(nseq, 1, n_heads, HEAD_DIM))
            vs.append(v.reshape(nseq, 1, n_heads, HEAD_DIM))
            lfs.append(lf.reshape(nseq, 1, n_heads))
        x, fcb = _out_ffn(x, o, m["wo"], f["nw"], f["w1"], f["cw"], f["cb"], f["w2"],
                          (state_ffn_conv, i), decode=True)
        ffn_cb.append(fcb)
    return (x.reshape(nseq, 1, -1), jnp.stack(gdn_s), jnp.stack(gdn_cb), jnp.stack(ks), jnp.stack(vs),
            jnp.stack(lfs), jnp.stack(ffn_cb))


def kernel(x_prompt, x_sample, state_gdn, state_gdn_conv, cache_k, cache_v, cache_logf, state_ffn_conv, page_table,
           norm_mix, norm_ffn, gdn_w_in, gdn_conv_w, gdn_a_log, gdn_dt_bias, gdn_norm, gdn_w_out, fox_w_in, fox_b_f,
           fox_q_norm, fox_k_norm, fox_w_out, ffn_w_in, ffn_conv_w, ffn_conv_b, ffn_w_out):
    layers = _prepare_weights(norm_mix, norm_ffn, gdn_w_in, gdn_conv_w, gdn_a_log, gdn_dt_bias, gdn_norm, gdn_w_out,
                              fox_w_in, fox_b_f, fox_q_norm, fox_k_norm, fox_w_out, ffn_w_in, ffn_conv_w, ffn_conv_b,
                              ffn_w_out)
    prompt = _trunk_prompt(x_prompt, layers)
    sample = _trunk_decode(x_sample, state_gdn, state_gdn_conv, cache_k, cache_v, cache_logf, state_ffn_conv,
                           page_table, layers)
    return (prompt[0], sample[0]) + prompt[1:] + sample[1:]
```

```python
import functools

import jax
import jax.numpy as jnp
from jax import lax
from jax.experimental import pallas as pl
from jax.experimental.pallas import tpu as pltpu

F32 = jnp.float32
BF16 = jnp.bfloat16

HEAD_DIM = 128
GDN_CHUNK = 64
EPS = 1e-6
V7X_VMEM_LIMIT_BYTES = 56 * 1024 * 1024
ROW_TILE = 512
ATTN_TILE = 512
CONV_COLS = 512
FFN_COLS = 256
GDN_HEADS_PER_STEP = 8
GDN_SEQ_TILE = 1024
GDN_SOLVE_BATCH = 16
GDN_DECODE_SEQS = 2
GDN_SOLVE_HEADS = 2


def _cparams(*semantics):
    return pltpu.CompilerParams(dimension_semantics=semantics, vmem_limit_bytes=V7X_VMEM_LIMIT_BYTES)


def _resident(shape):
    return pl.BlockSpec(shape, lambda *_: (0,) * len(shape), pipeline_mode=pl.Buffered(1))


def _layer_of(w):
    stacked, layer = w
    shape = stacked.shape[1:]
    return pl.BlockSpec((None,) + shape, lambda *_: (layer,) + (0,) * len(shape), pipeline_mode=pl.Buffered(1))


def _sigmoid(x):
    return 1.0 / (1.0 + jnp.exp(-x))


def _silu(x):
    return x * _sigmoid(x)


def _softplus(x):
    return jnp.maximum(x, 0.0) + jnp.log(1.0 + jnp.exp(-jnp.abs(x)))


def _rmsnorm_rows(x, w):
    return x * lax.rsqrt(jnp.mean(x * x, axis=-1, keepdims=True) + EPS) * w


def _dot(a, b):
    return jnp.dot(a, b, preferred_element_type=F32)


def _dot_nt(a, b):
    return lax.dot_general(a, b, (((1,), (1,)), ((), ())), preferred_element_type=F32)


def _dot_tn(a, b):
    return lax.dot_general(a, b, (((0,), (0,)), ((), ())), preferred_element_type=F32)


def _split3(x):
    hi = x.astype(BF16)
    r = x - hi.astype(F32)
    mid = r.astype(BF16)
    lo = (r - mid.astype(F32)).astype(BF16)
    return hi, mid, lo


def _dot_sel_left(sel, x):
    hi, mid, lo = _split3(x)
    return _dot(sel, hi) + _dot(sel, mid) + _dot(sel, lo)


def _dot_sel_right(x, sel):
    hi, mid, lo = _split3(x)
    return _dot(hi, sel) + _dot(mid, sel) + _dot(lo, sel)


def _dot_f32(a, b):
    return jnp.dot(a, b, preferred_element_type=F32, precision=lax.Precision.HIGHEST)


def _iota2(shape, dim):
    return lax.broadcasted_iota(jnp.int32, shape, dim)


def _gdn_in_kernel(x_ref, nw_ref, w_ref, cw_ref, alog_ref, dtb_ref, buf_ref,
                   q_ref, k_ref, v_ref, z_ref, beta_ref, g_ref, cs_ref, ubuf, carry,
                   *, decode, tm, qk_width, conv_dim, n_vheads):
    z_cols = slice(conv_dim, conv_dim + n_vheads * HEAD_DIM)
    gate_cols = slice(conv_dim + n_vheads * HEAD_DIM, conv_dim + n_vheads * HEAD_DIM + 2 * n_vheads)
    hb = _rmsnorm_rows(x_ref[...], nw_ref[...]).astype(BF16)

    if not decode:
        @pl.when(pl.program_id(1) == 0)
        def _():
            carry[...] = jnp.zeros_like(carry)
            carry[5:8, :] = buf_ref[...]

    wc = CONV_COLS
    ahead = _dot(hb, w_ref[:, 0:wc])
    for j in range(conv_dim // wc):
        c0 = j * wc
        cols = slice(c0, c0 + wc)
        u = ahead
        if j + 1 < conv_dim // wc:
            ahead = _dot(hb, w_ref[:, c0 + wc:c0 + 2 * wc])
        else:
            ahead = _dot(hb, w_ref[:, z_cols])
        if decode:
            b0, b1, b2 = buf_ref[:, 0, cols], buf_ref[:, 1, cols], buf_ref[:, 2, cols]
            y = cw_ref[0:1, cols] * b0 + cw_ref[1:2, cols] * b1 + cw_ref[2:3, cols] * b2 + cw_ref[3:4, cols] * u
            cs_ref[:, 0, cols] = b1
            cs_ref[:, 1, cols] = b2
            cs_ref[:, 2, cols] = u
        else:
            ubuf[0:8, :] = carry[:, cols]
            ubuf[8:8 + tm, :] = u
            y = (cw_ref[3:4, cols] * u + cw_ref[2:3, cols] * ubuf[7:7 + tm, :]
                 + cw_ref[1:2, cols] * ubuf[6:6 + tm, :] + cw_ref[0:1, cols] * ubuf[5:5 + tm, :])
            carry[:, cols] = ubuf[tm:tm + 8, :]
        y = _silu(y)
        for hh in range(wc // HEAD_DIM):
            yh = y[:, hh * HEAD_DIM:(hh + 1) * HEAD_DIM]
            col = c0 + hh * HEAD_DIM
            if col < 2 * qk_width:
                yh = yh * lax.rsqrt(jnp.sum(yh * yh, axis=-1, keepdims=True) + EPS)
            dst, col = ((q_ref, col) if col < qk_width else
                        (k_ref, col - qk_width) if col < 2 * qk_width else (v_ref, col - 2 * qk_width))
            if decode:
                dst[:, col:col + HEAD_DIM] = yh
            else:
                dst[col // HEAD_DIM] = yh
    if not decode:
        cs_ref[...] = carry[5:8, :]

    z = ahead
    if decode:
        z_ref[...] = z
    else:
        for h in range(n_vheads):
            z_ref[h] = z[:, h * HEAD_DIM:(h + 1) * HEAD_DIM]
    ba = _dot(hb, w_ref[:, gate_cols])
    beta_ref[...] = _sigmoid(ba[:, :n_vheads])
    g = -jnp.exp(alog_ref[...]) * _softplus(ba[:, n_vheads:] + dtb_ref[...])
    if decode:
        g_ref[...] = g
    else:
        r, c = _iota2((tm, tm), 0), _iota2((tm, tm), 1)
        same_chunk_lower = jnp.where((r // GDN_CHUNK == c // GDN_CHUNK) & (c <= r), 1.0, 0.0).astype(BF16)
        g_ref[...] = _dot_sel_left(same_chunk_lower, g)


def _gdn_in(x, buf, nw, w_in, cw, alog, dtb, *, decode):
    bsz, seq, dm = x.shape
    conv_dim = cw.shape[1]
    vw = (w_in[0].shape[-1] - conv_dim) * HEAD_DIM // (HEAD_DIM + 2)
    qk_width = (conv_dim - vw) // 2
    n_vheads = vw // HEAD_DIM
    tm = min(ROW_TILE, seq)
    grid = (bsz, seq // tm)
    row = lambda width: pl.BlockSpec((None, tm, width), lambda b, i: (b, i, 0))
    if decode:
        buf, layer = buf
        buf_spec = pl.BlockSpec((None, tm, 3, conv_dim), lambda b, i: (layer, i, 0, 0))
        cs_shape, cs_spec = (seq, 3, conv_dim), pl.BlockSpec((tm, 3, conv_dim), lambda b, i: (i, 0, 0))
    else:
        buf_spec = pl.BlockSpec((None, 3, conv_dim), lambda b, i: (b, 0, 0))
        cs_shape, cs_spec = (bsz, 3, conv_dim), pl.BlockSpec((None, 3, conv_dim), lambda b, i: (b, 0, 0))
    kern = functools.partial(_gdn_in_kernel, decode=decode, tm=tm, qk_width=qk_width, conv_dim=conv_dim,
                             n_vheads=n_vheads)
    if decode:
        wide = lambda width: (row(width), jax.ShapeDtypeStruct((bsz, seq, width), F32))
    else:
        wide = lambda width: (pl.BlockSpec((None, width // HEAD_DIM, tm, HEAD_DIM), lambda b, i: (b, 0, i, 0)),
                              jax.ShapeDtypeStruct((bsz, width // HEAD_DIM, seq, HEAD_DIM), F32))
    (q_spec, q_shape), (v_spec, v_shape) = wide(qk_width), wide(vw)
    gate_shape = jax.ShapeDtypeStruct((bsz, seq, n_vheads), F32)
    return pl.pallas_call(
        kern,
        grid=grid,
        in_specs=[row(dm), _resident((1, dm)), _layer_of(w_in), _resident(cw.shape), _resident((1, n_vheads)),
                  _resident((1, n_vheads)), buf_spec],
        out_specs=[q_spec, q_spec, v_spec, v_spec, row(n_vheads), row(n_vheads), cs_spec],
        out_shape=[q_shape, q_shape, v_shape, v_shape, gate_shape, gate_shape, jax.ShapeDtypeStruct(cs_shape, F32)],
        scratch_shapes=[pltpu.VMEM((tm + 8, CONV_COLS), F32), pltpu.VMEM((8, conv_dim), F32)],
        compiler_params=_cparams("arbitrary", "arbitrary"),
        name="gdn_in_decode" if decode else "gdn_in_prompt",
    )(x, nw, w_in[0], cw, alog, dtb, buf)


def _bdot(a, b):
    return lax.dot_general(a, b, (((2,), (1,)), ((0,), (0,))), preferred_element_type=F32)


def _bdot_nt(a, b):
    return lax.dot_general(a, b, (((2,), (2,)), ((0,), (0,))), preferred_element_type=F32)


def _unit_lower_inverse(n, row, col):
    eye = jnp.where(row == col, 1.0, 0.0)
    inv = eye - jnp.where((row >> 1) == (col >> 1), n, 0.0)
    size = n.shape[-1]
    bits = 2
    while (1 << bits) <= size:
        off_diag = jnp.where((row >> bits) == (col >> bits),
                             jnp.where((row >> (bits - 1)) == (col >> (bits - 1)), 0.0, n), 0.0)
        inv_b = inv.astype(BF16)
        inv = inv - _bdot(inv_b, _bdot(off_diag.astype(BF16), inv_b).astype(BF16))
        bits += 1
    n_hi = n.astype(BF16).astype(F32)
    x_hi = inv.astype(BF16).astype(F32)
    lhs = jnp.concatenate([n_hi, n - n_hi, n_hi], axis=-1).astype(BF16)
    rhs = jnp.concatenate([x_hi, x_hi, inv - x_hi], axis=1).astype(BF16)
    resid = eye - inv - _bdot(lhs, rhs)
    return inv + _bdot(x_hi.astype(BF16), resid.astype(BF16))


def _gdn_core_kernel(q_ref, k_ref, v_ref, z_ref, beta_ref, gcum_ref, nw_ref, o_ref, s_out_ref,
                     s_scr, u_scr, w_scr, att_scr, qg_scr, kd_scr, egl_scr, gb_scr, *, lt, hb, rep, nb, hs):
    c = GDN_CHUNK
    groups = lt // c // nb
    t = pl.program_id(2)
    head0 = pl.program_id(1) * hb
    q_scale = HEAD_DIM ** -0.5

    @pl.when(t == 0)
    def _():
        s_scr[...] = jnp.zeros_like(s_scr)

    nbh = nb * hs
    row = lax.broadcasted_iota(jnp.int32, (nbh, c, c), 1)
    col = lax.broadcasted_iota(jnp.int32, (nbh, c, c), 2)
    lower = row >= col
    strict = row > col
    lane = _iota2((nb * c, HEAD_DIM), 1)
    gate_lane = _iota2((nb * c, beta_ref.shape[-1]), 1)
    pick = jnp.where(_iota2((nbh * c, HEAD_DIM), 1) < 3, 1.0, 0.0).astype(BF16).reshape(nbh, c, HEAD_DIM)

    def solve(idx, _):
        hset = idx // groups
        chunk0 = pl.multiple_of((idx - hset * groups) * nb, nb)
        rows = pl.ds(pl.multiple_of(chunk0 * c, nb * c), nb * c)
        heads = [hset * hs + m for m in range(hs)]
        per_head = []
        for hh in heads:
            own = gate_lane == head0 + hh
            bcol = jnp.sum(jnp.where(own, beta_ref[rows, :], 0.0), axis=-1, keepdims=True)
            gcol = jnp.sum(jnp.where(own, gcum_ref[rows, :], 0.0), axis=-1, keepdims=True)
            g_hi = gcol.astype(BF16).astype(F32)
            g_mid = (gcol - g_hi).astype(BF16).astype(F32)
            g_lo = gcol - g_hi - g_mid
            pieces = jnp.where(lane == 0, g_hi, jnp.where(lane == 1, g_mid, jnp.where(lane == 2, g_lo, 0.0)))
            per_head.append((gcol, bcol, pieces.astype(BF16), q_ref[hh // rep, rows, :] * q_scale,
                             k_ref[hh // rep, rows, :], v_ref[hh, rows, :]))
        stack = lambda i, width: jnp.concatenate([p[i].reshape(nb, c, width) for p in per_head], axis=0)
        g3, b3 = stack(0, 1), stack(1, 1)
        gcum_t = _bdot_nt(pick, stack(2, HEAD_DIM))
        decay = jnp.where(lower, jnp.exp(jnp.minimum(g3 - gcum_t, 0.0)), 0.0)
        q3, k3, v3 = stack(3, HEAD_DIM), stack(4, HEAD_DIM), stack(5, HEAD_DIM)
        kb = k3 * b3
        k_bf = k3.astype(BF16)
        n = jnp.where(strict, _bdot_nt(kb.astype(BF16), k_bf) * decay, 0.0)
        inv = _unit_lower_inverse(n, row, col)
        eg = jnp.exp(g3)
        rhs = jnp.concatenate([v3 * b3, kb * eg], axis=-1).astype(BF16)
        sol = _bdot(inv.astype(BF16), rhs)
        attn = jnp.where(lower, _bdot_nt(q3.astype(BF16), k_bf) * decay, 0.0)
        g_last = g3[:, c - 1:c, :]
        qg = q3 * eg
        kd = k3 * jnp.exp(g_last - g3)
        for m, hh in enumerate(heads):
            mine = lambda a: a[m * nb:(m + 1) * nb].reshape(nb * c, a.shape[-1])
            u_scr[hh, rows, :] = mine(sol[:, :, :HEAD_DIM])
            w_scr[hh, rows, :] = mine(sol[:, :, HEAD_DIM:]).astype(BF16)
            att_scr[hh, rows, :] = mine(attn).astype(BF16)
            qg_scr[hh, rows, :] = mine(qg).astype(BF16)
            kd_scr[hh, rows, :] = mine(kd).astype(BF16)
            gb_scr[m] = jnp.broadcast_to(per_head[m][0], gb_scr.shape[1:])
            egl_scr[hh, pl.ds(chunk0, nb), :] = jnp.exp(gb_scr[m, pl.ds(c - 1, nb, stride=c), :])
        return 0

    lax.fori_loop(0, hb // hs * groups, solve, 0)

    def step(i, _):
        rows = pl.ds(pl.multiple_of(i * c, c), c)
        heads = range(hb)
        state = [s_scr[hh] for hh in heads]
        res = [_dot(jnp.concatenate([w_scr[hh, rows, :], qg_scr[hh, rows, :]], axis=0), state[hh].astype(BF16))
               for hh in heads]
        v_new = [(u_scr[hh, rows, :] - res[hh][:c]).astype(BF16) for hh in heads]
        o = [res[hh][c:] + _dot(att_scr[hh, rows, :], v_new[hh]) for hh in heads]
        grown = [_dot_tn(kd_scr[hh, rows, :], v_new[hh]) for hh in heads]
        for hh in heads:
            s_scr[hh] = state[hh] * egl_scr[hh, pl.ds(i, 1), :] + grown[hh]
            gated = _rmsnorm_rows(o[hh], nw_ref[...]) * _silu(z_ref[hh, rows, :])
            o_ref[rows, hh * HEAD_DIM:(hh + 1) * HEAD_DIM] = gated.astype(o_ref.dtype)
        return 0

    lax.fori_loop(0, lt // c, step, 0)

    @pl.when(t == pl.num_programs(2) - 1)
    def _():
        s_out_ref[...] = s_scr[...]


def _gdn_core_prompt(q, k, v, z, beta, gcum, nw):
    bsz, n_vheads, seq, _ = v.shape
    rep = n_vheads // q.shape[1]
    hb = min(GDN_HEADS_PER_STEP, n_vheads)
    lt = min(GDN_SEQ_TILE, seq)
    nb = min(GDN_SOLVE_BATCH, lt // GDN_CHUNK)
    hs = min(GDN_SOLVE_HEADS, hb)
    heads = lambda n: pl.BlockSpec((None, n, lt, HEAD_DIM), lambda b, h, t: (b, h, t, 0))
    gates = pl.BlockSpec((None, lt, n_vheads), lambda b, h, t: (b, t, 0))
    per_head = lambda width, dtype: pltpu.VMEM((hb, lt, width), dtype)
    return pl.pallas_call(
        functools.partial(_gdn_core_kernel, lt=lt, hb=hb, rep=rep, nb=nb, hs=hs),
        grid=(bsz, n_vheads // hb, seq // lt),
        in_specs=[heads(hb // rep), heads(hb // rep), heads(hb), heads(hb), gates, gates, _resident((1, HEAD_DIM))],
        out_specs=[pl.BlockSpec((None, lt, hb * HEAD_DIM), lambda b, h, t: (b, t, h)),
                   pl.BlockSpec((None, hb, HEAD_DIM, HEAD_DIM), lambda b, h, t: (b, h, 0, 0))],
        out_shape=[jax.ShapeDtypeStruct((bsz, seq, n_vheads * HEAD_DIM), BF16),
                   jax.ShapeDtypeStruct((bsz, n_vheads, HEAD_DIM, HEAD_DIM), F32)],
        scratch_shapes=[pltpu.VMEM((hb, HEAD_DIM, HEAD_DIM), F32), per_head(HEAD_DIM, F32), per_head(HEAD_DIM, BF16),
                        per_head(GDN_CHUNK, BF16), per_head(HEAD_DIM, BF16), per_head(HEAD_DIM, BF16),
                        pltpu.VMEM((hb, lt // GDN_CHUNK, HEAD_DIM), F32),
                        pltpu.VMEM((hs, nb * GDN_CHUNK, HEAD_DIM), F32)],
        compiler_params=_cparams("arbitrary", "arbitrary", "arbitrary"),
        name="gdn_core_prompt",
    )(q, k, v, z, beta, gcum, nw)


def _gdn_core_decode_kernel(q_ref, k_ref, v_ref, z_ref, beta_ref, g_ref, nw_ref, s_ref, o_ref, so_ref,
                            *, n_vheads, rep):
    q_scale = HEAD_DIM ** -0.5
    row = _iota2((8, HEAD_DIM), 0)

    def rows3(a, b, c):
        return jnp.where(row == 0, a, jnp.where(row == 1, b, jnp.where(row == 2, c, 0.0)))

    pairs = [(s, h) for s in range(q_ref.shape[0]) for h in range(n_vheads)]
    idx = range(len(pairs))
    lanes = [slice(h * HEAD_DIM, (h + 1) * HEAD_DIM) for _, h in pairs]
    klanes = [slice(h // rep * HEAD_DIM, (h // rep + 1) * HEAD_DIM) for _, h in pairs]
    qh = [q_ref[s, :, klanes[i]] * q_scale for i, (s, _) in enumerate(pairs)]
    kh = [k_ref[s, :, klanes[i]] for i, (s, _) in enumerate(pairs)]
    beta = [beta_ref[s, :, h:h + 1] for s, h in pairs]
    eg = [jnp.exp(g_ref[s, :, h:h + 1]) for s, h in pairs]
    res = [_dot(rows3(kh[i] * beta[i] * eg[i], qh[i] * eg[i], 0.0).astype(BF16), s_ref[pairs[i]].astype(BF16))
           for i in idx]
    v_new = [v_ref[pairs[i][0], :, lanes[i]] * beta[i] - res[i][0:1, :] for i in idx]
    k_hi = [kh[i].astype(BF16).astype(F32) for i in idx]
    v_hi = [v_new[i].astype(BF16).astype(F32) for i in idx]
    grown = [_dot_tn(rows3(k_hi[i], kh[i] - k_hi[i], k_hi[i]).astype(BF16),
                     rows3(v_hi[i], v_hi[i], v_new[i] - v_hi[i]).astype(BF16)) for i in idx]
    for i in idx:
        s = pairs[i][0]
        so_ref[pairs[i]] = s_ref[pairs[i]] * eg[i] + grown[i]
        o = res[i][1:2, :] + jnp.sum(qh[i] * kh[i], axis=-1, keepdims=True) * v_new[i]
        gated = _rmsnorm_rows(o, nw_ref[...]) * _silu(z_ref[s, :, lanes[i]])
        o_ref[s, :, lanes[i]] = gated.astype(o_ref.dtype)


def _gdn_core_decode(q, k, v, z, beta, g, nw, state):
    nseq, _, vw = v.shape
    n_vheads = vw // HEAD_DIM
    rep = vw // q.shape[-1]
    sb = GDN_DECODE_SEQS if nseq % GDN_DECODE_SEQS == 0 else 1
    row = lambda width: pl.BlockSpec((sb, 1, width), lambda s: (s, 0, 0))
    st = pl.BlockSpec((sb, n_vheads, HEAD_DIM, HEAD_DIM), lambda s: (s, 0, 0, 0))
    return pl.pallas_call(
        functools.partial(_gdn_core_decode_kernel, n_vheads=n_vheads, rep=rep),
        grid=(nseq // sb,),
        in_specs=[row(q.shape[-1]), row(q.shape[-1]), row(vw), row(vw), row(n_vheads), row(n_vheads),
                  _resident((1, HEAD_DIM)), st],
        out_specs=[row(vw), st],
        out_shape=[jax.ShapeDtypeStruct((nseq, 1, vw), BF16), jax.ShapeDtypeStruct(state.shape, F32)],
        compiler_params=_cparams("arbitrary"),
        name="gdn_core_decode",
    )(q, k, v, z, beta, g, nw, state)


def _out_ffn_kernel(x_ref, o_ref, wo_ref, nw_ref, w1_ref, cw_ref, cb_ref, w2_ref, buf_ref,
                    y_ref, cs_ref, gbuf, ubuf, carry, act, *, decode, tm, d_ff):
    x1 = x_ref[...] + _dot(o_ref[...], wo_ref[...])
    hb = _rmsnorm_rows(x1, nw_ref[...]).astype(BF16)

    if not decode:
        @pl.when(pl.program_id(1) == 0)
        def _():
            carry[...] = jnp.zeros_like(carry)
            carry[6:8, :] = buf_ref[...]

    def conv(u, scratch, c0, wc):
        cols = slice(c0, c0 + wc)
        if decode:
            b0 = buf_ref[:, 0, cols]
            b1 = buf_ref[:, 1, cols]
            cs_ref[:, 0, cols] = b1
            cs_ref[:, 1, cols] = u
            y = cw_ref[0:1, cols] * b0 + cw_ref[1:2, cols] * b1 + cw_ref[2:3, cols] * u
        else:
            scratch[0:8, :] = carry[:, cols]
            scratch[8:8 + tm, :] = u
            y = (cw_ref[2:3, cols] * u + cw_ref[1:2, cols] * scratch[7:7 + tm, :]
                 + cw_ref[0:1, cols] * scratch[6:6 + tm, :])
            carry[:, cols] = scratch[tm:tm + 8, :]
        return y + cb_ref[:, cols]

    wc = FFN_COLS

    def project(j):
        c0 = j * wc
        return _dot(hb, w1_ref[:, c0:c0 + wc]), _dot(hb, w1_ref[:, d_ff + c0:d_ff + c0 + wc])

    ahead = project(0)
    for j in range(d_ff // wc):
        c0 = j * wc
        pre_gate, pre_up = ahead
        if j + 1 < d_ff // wc:
            ahead = project(j + 1)
        gate = conv(pre_gate, gbuf, c0, wc)
        up = conv(pre_up, ubuf, d_ff + c0, wc)
        act[:, c0:c0 + wc] = (_silu(gate) * up).astype(BF16)
    if not decode:
        cs_ref[...] = carry[6:8, :]
    y_ref[...] = x1 + _dot(act[...], w2_ref[...])


def _out_ffn(x, o, wo, nw, w1, cw, cb, w2, buf, *, decode):
    bsz, seq, dm = x.shape
    d_ff = w2[0].shape[1]
    tm = min(ROW_TILE, seq)
    row = lambda width: pl.BlockSpec((None, tm, width), lambda b, i: (b, i, 0))
    if decode:
        buf, layer = buf
        buf_spec = pl.BlockSpec((None, tm, 2, 2 * d_ff), lambda b, i: (layer, i, 0, 0))
        cs_shape, cs_spec = (seq, 2, 2 * d_ff), pl.BlockSpec((tm, 2, 2 * d_ff), lambda b, i: (i, 0, 0))
    else:
        buf_spec = pl.BlockSpec((None, 2, 2 * d_ff), lambda b, i: (b, 0, 0))
        cs_shape, cs_spec = (bsz, 2, 2 * d_ff), pl.BlockSpec((None, 2, 2 * d_ff), lambda b, i: (b, 0, 0))
    return pl.pallas_call(
        functools.partial(_out_ffn_kernel, decode=decode, tm=tm, d_ff=d_ff),
        grid=(bsz, seq // tm),
        in_specs=[row(dm), row(o.shape[-1]), _layer_of(wo), _resident((1, dm)), _layer_of(w1),
                  _resident(cw.shape), _resident((1, 2 * d_ff)), _layer_of(w2), buf_spec],
        out_specs=[row(dm), cs_spec],
        out_shape=[jax.ShapeDtypeStruct((bsz, seq, dm), F32), jax.ShapeDtypeStruct(cs_shape, F32)],
        scratch_shapes=[pltpu.VMEM((tm + 8, FFN_COLS), F32), pltpu.VMEM((tm + 8, FFN_COLS), F32),
                        pltpu.VMEM((8, 2 * d_ff), F32), pltpu.VMEM((tm, d_ff), BF16)],
        compiler_params=_cparams("arbitrary", "arbitrary"),
        name="out_ffn_decode" if decode else "out_ffn_prompt",
    )(x, o, wo[0], nw, w1[0], cw, cb, w2[0], buf)


def _fox_in_kernel(x_ref, nw_ref, w_ref, bf_ref, bft_ref, qn_ref, kn_ref,
                   q_ref, k_ref, v_ref, og_ref, lf_ref, cum_ref, carry, *, tm, n_heads):
    width = n_heads * HEAD_DIM
    hb = _rmsnorm_rows(x_ref[...], nw_ref[...]).astype(BF16)
    q = _dot(hb, w_ref[:, 0:width])
    k = _dot(hb, w_ref[:, width:2 * width])
    for h in range(n_heads):
        lanes = slice(h * HEAD_DIM, (h + 1) * HEAD_DIM)
        q_ref[:, lanes] = _rmsnorm_rows(q[:, lanes], qn_ref[...])
    v_ref[...] = _dot(hb, w_ref[:, 2 * width:3 * width])
    for h in range(n_heads):
        lanes = slice(h * HEAD_DIM, (h + 1) * HEAD_DIM)
        k_ref[:, lanes] = _rmsnorm_rows(k[:, lanes], kn_ref[...])
    og_ref[...] = _dot(hb, w_ref[:, 3 * width:4 * width])
    lf_ref[...] = -_softplus(-(_dot(hb, w_ref[:, 4 * width:4 * width + n_heads]) + bf_ref[...]))

    @pl.when(pl.program_id(1) == 0)
    def _():
        carry[...] = jnp.zeros_like(carry)

    logits_t = lax.dot_general(w_ref[:, 4 * width:4 * width + n_heads], hb, (((0,), (1,)), ((), ())),
                               preferred_element_type=F32)
    lf_t = -_softplus(-(logits_t + bft_ref[...]))
    upper = jnp.where(_iota2((tm, tm), 0) <= _iota2((tm, tm), 1), 1.0, 0.0).astype(BF16)
    cum = _dot_sel_right(lf_t, upper) + carry[:, 0:1]
    cum_ref[...] = cum
    carry[...] = jnp.broadcast_to(cum[:, tm - 1:tm], carry.shape)


def _fox_in(x, nw, w_in, bf, bft, qn, kn):
    bsz, seq, dm = x.shape
    n_heads = bf.shape[1]
    width = n_heads * HEAD_DIM
    tm = min(ROW_TILE, seq)
    row = lambda w: pl.BlockSpec((None, tm, w), lambda b, i: (b, i, 0))
    wide = jax.ShapeDtypeStruct((bsz, seq, width), F32)
    return pl.pallas_call(
        functools.partial(_fox_in_kernel, tm=tm, n_heads=n_heads),
        grid=(bsz, seq // tm),
        in_specs=[row(dm), _resident((1, dm)), _layer_of(w_in), _resident((1, n_heads)),
                  _resident((n_heads, 1)), _resident((1, HEAD_DIM)), _resident((1, HEAD_DIM))],
        out_specs=[row(width), row(width), row(width), row(width), row(n_heads),
                   pl.BlockSpec((None, n_heads, tm), lambda b, i: (b, 0, i))],
        out_shape=[wide, wide, wide, wide, jax.ShapeDtypeStruct((bsz, seq, n_heads), F32),
                   jax.ShapeDtypeStruct((bsz, n_heads, seq), F32)],
        scratch_shapes=[pltpu.VMEM((n_heads, HEAD_DIM), F32)],
        compiler_params=_cparams("arbitrary", "arbitrary"),
        name="fox_in",
    )(x, nw, w_in[0], bf, bft, qn, kn)


def _fox_attn_kernel(q_ref, k_ref, v_ref, cum_ref, og_ref, o_ref, kb_scr, vb_scr, *, tile):
    h = pl.program_id(1)
    qi = pl.program_id(2)
    scale = HEAD_DIM ** -0.5
    exp2_scale = scale * 1.4426950408889634

    @pl.when(qi == 0)
    def _():
        kb_scr[...] = k_ref[...].astype(BF16)
        vb_scr[...] = v_ref[...].astype(BF16)

    qb = q_ref[...].astype(BF16)

    def lane_tiles(x):
        return [x[:, t * HEAD_DIM:(t + 1) * HEAD_DIM] for t in range(tile // HEAD_DIM)]

    def key_block(j, carry, diagonal):
        m_prev, l_prev, acc = carry
        cols = pl.ds(pl.multiple_of(j * tile, tile), tile)
        s = _dot_nt(qb, kb_scr[cols, :]) - cum_ref[pl.ds(h, 1), cols] * (1.0 / scale)
        if diagonal:
            s = jnp.where(_iota2((tile, tile), 1) <= _iota2((tile, tile), 0), s, -jnp.inf)
        m_new = jnp.maximum(m_prev, jnp.max(functools.reduce(jnp.maximum, lane_tiles(s)), axis=-1, keepdims=True))
        alpha = jnp.exp2((m_prev - m_new) * exp2_scale)
        p = jnp.exp2((s - m_new) * exp2_scale)
        l_new = alpha * l_prev + jnp.sum(functools.reduce(jnp.add, lane_tiles(p)), axis=-1, keepdims=True)
        acc = alpha * acc + _dot(p.astype(BF16), vb_scr[cols, :])
        return m_new, l_new, acc

    init = (jnp.full((tile, 1), -jnp.inf, F32), jnp.zeros((tile, 1), F32), jnp.zeros((tile, HEAD_DIM), F32))
    carry = lax.fori_loop(0, qi, lambda j, c: key_block(j, c, False), init)
    _, l_fin, acc = key_block(qi, carry, True)
    o_ref[...] = (acc / l_fin * _sigmoid(og_ref[...])).astype(o_ref.dtype)


def _fox_attn_prompt(q, k, v, cum, og):
    bsz, seq, width = q.shape
    n_heads = width // HEAD_DIM
    tile = min(ATTN_TILE, seq)
    q_spec = pl.BlockSpec((None, tile, HEAD_DIM), lambda b, h, i: (b, i, h))
    kv_spec = pl.BlockSpec((None, seq, HEAD_DIM), lambda b, h, i: (b, 0, h))
    cum_spec = pl.BlockSpec((None, n_heads, seq), lambda b, h, i: (b, 0, 0))
    return pl.pallas_call(
        functools.partial(_fox_attn_kernel, tile=tile),
        grid=(bsz, n_heads, seq // tile),
        in_specs=[q_spec, kv_spec, kv_spec, cum_spec, q_spec],
        out_specs=q_spec,
        out_shape=jax.ShapeDtypeStruct((bsz, seq, width), BF16),
        scratch_shapes=[pltpu.VMEM((seq, HEAD_DIM), BF16), pltpu.VMEM((seq, HEAD_DIM), BF16)],
        compiler_params=_cparams("arbitrary", "arbitrary", "arbitrary"),
        name="fox_attn_prompt",
    )(q, k, v, cum, og)


def _fox_attn_decode_kernel(pt_ref, q_ref, kn_ref, vn_ref, lfn_ref, og_ref, *rest, n_heads, page, n_pages):
    del pt_ref
    k_pages, v_pages, lf_pages = rest[:n_pages], rest[n_pages:2 * n_pages], rest[2 * n_pages:3 * n_pages]
    o_ref, lf_scr = rest[3 * n_pages:]
    flat = n_heads * page
    scale = HEAD_DIM ** -0.5

    for p in range(n_pages):
        lf_scr[p:p + 1, :] = lf_pages[p][...]
    lf = lf_scr[...]
    lane = _iota2((n_pages, flat), 1)
    cum, page_total = lf, lf
    shift = n_heads
    while shift < flat:
        cum = cum + jnp.where(lane >= shift, pltpu.roll(cum, shift, axis=1), 0.0)
        page_total = page_total + pltpu.roll(page_total, shift, axis=1)
        shift *= 2
    earlier = jnp.where(_iota2((n_pages, n_pages), 1) < _iota2((n_pages, n_pages), 0), 1.0, 0.0).astype(BF16)
    before = _dot_sel_left(earlier, page_total)
    cum = cum + before
    past_total = before[n_pages - 1:n_pages, :] + page_total[n_pages - 1:n_pages, :]

    own_head = (_iota2((n_heads, flat), 1) & (n_heads - 1)) == _iota2((n_heads, flat), 0)
    eye = _iota2((n_heads, n_heads), 0) == _iota2((n_heads, n_heads), 1)
    to_rows = lambda r: jnp.sum(jnp.where(eye, jnp.broadcast_to(r, (n_heads, n_heads)), 0.0), axis=-1, keepdims=True)

    q = q_ref[...]
    qb = q.astype(BF16)
    scores = []
    for p in range(n_pages):
        s = _dot_nt(qb, k_pages[p][...].astype(BF16)) * scale - cum[p:p + 1, :]
        scores.append(jnp.where(own_head, s, -jnp.inf))
    s_new = (jnp.sum(q * kn_ref[...], axis=-1, keepdims=True) * scale
             - (to_rows(past_total[:, :n_heads]) + to_rows(lfn_ref[...])))
    m = s_new
    for s in scores:
        m = jnp.maximum(m, jnp.max(s, axis=-1, keepdims=True))
    denom = jnp.exp(s_new - m)
    acc = denom * vn_ref[...]
    for p in range(n_pages):
        w = jnp.exp(scores[p] - m)
        denom = denom + jnp.sum(w, axis=-1, keepdims=True)
        acc = acc + _dot(w.astype(BF16), v_pages[p][...].astype(BF16))
    o_ref[...] = (acc / denom * _sigmoid(og_ref[...])).astype(o_ref.dtype)


def _fox_attn_decode(page_table, q, k_new, v_new, lf_new, og, k_pool, v_pool, lf_pool):
    nseq, n_heads, _ = q.shape
    n_pages = page_table.shape[1]
    flat = k_pool.shape[1]
    page = flat // n_heads
    tok = pl.BlockSpec((None, n_heads, HEAD_DIM), lambda s, pt: (s, 0, 0))
    pool = [pl.BlockSpec((None, flat, HEAD_DIM), lambda s, pt, p=p: (pt[s, p], 0, 0)) for p in range(n_pages)]
    lf_specs = [pl.BlockSpec((None, 1, flat), lambda s, pt, p=p: (pt[s, p], 0, 0)) for p in range(n_pages)]
    grid_spec = pltpu.PrefetchScalarGridSpec(
        num_scalar_prefetch=1,
        grid=(nseq,),
        in_specs=[tok, tok, tok, pl.BlockSpec((None, 1, n_heads), lambda s, pt: (s, 0, 0)), tok] + pool + pool + lf_specs,
        out_specs=tok,
        scratch_shapes=[pltpu.VMEM((n_pages, flat), F32)],
    )
    return pl.pallas_call(
        functools.partial(_fox_attn_decode_kernel, n_heads=n_heads, page=page, n_pages=n_pages),
        grid_spec=grid_spec,
        out_shape=jax.ShapeDtypeStruct((nseq, n_heads, HEAD_DIM), BF16),
        compiler_params=_cparams("arbitrary"),
        name="fox_attn_decode",
    )(page_table, q, k_new, v_new, lf_new, og, *([k_pool] * n_pages), *([v_pool] * n_pages), *([lf_pool] * n_pages))


def _prepare_weights(norm_mix, norm_ffn, gdn_w_in, gdn_conv_w, gdn_a_log, gdn_dt_bias, gdn_norm, gdn_w_out,
                     fox_w_in, fox_b_f, fox_q_norm, fox_k_norm, fox_w_out, ffn_w_in, ffn_conv_w, ffn_conv_b,
                     ffn_w_out):
    fw = fox_w_out.shape[1]
    gdn_in_b, gdn_out_b = gdn_w_in.astype(BF16), gdn_w_out.astype(BF16)
    fox_in_b, fox_out_b = fox_w_in.astype(BF16), fox_w_out.astype(BF16)
    ffn_in_b, ffn_out_b = ffn_w_in.astype(BF16), ffn_w_out.astype(BF16)
    layers = []
    for i in range(norm_mix.shape[0]):
        j = i // 2
        ffn = dict(nw=norm_ffn[i][None], w1=(ffn_in_b, i), cw=ffn_conv_w[i], cb=ffn_conv_b[i][None], w2=(ffn_out_b, i))
        if i % 2 == 0:
            mixer = dict(nw=norm_mix[i][None], w_in=(gdn_in_b, j), cw=gdn_conv_w[j], alog=gdn_a_log[j][None],
                         dtb=gdn_dt_bias[j][None], onw=gdn_norm[j][None], wo=(gdn_out_b, j))
        else:
            mixer = dict(nw=norm_mix[i][None], w_in=(fox_in_b, j),
                         bf=fox_b_f[j][None], bft=fox_b_f[j][:, None], qn=fox_q_norm[j][None],
                         kn=fox_k_norm[j][None], wo=(fox_out_b, j))
        layers.append((mixer, ffn))
    return layers


def _trunk_prompt(x, layers):
    bsz, seq, _ = x.shape
    gdn_s, gdn_cb, ks, vs, lfs, ffn_cb = [], [], [], [], [], []
    for i, (m, f) in enumerate(layers):
        if i % 2 == 0:
            conv_dim = m["cw"].shape[1]
            q, k, v, z, beta, g, cb = _gdn_in(x, jnp.zeros((bsz, 3, conv_dim), F32), m["nw"], m["w_in"], m["cw"],
                                              m["alog"], m["dtb"], decode=False)
            o, s = _gdn_core_prompt(q, k, v, z, beta, g, m["onw"])
            gdn_cb.append(cb)
            gdn_s.append(s)
        else:
            q, k, v, og, lf, cum = _fox_in(x, m["nw"], m["w_in"], m["bf"], m["bft"], m["qn"], m["kn"])
            o = _fox_attn_prompt(q, k, v, cum, og)
            n_heads = lf.shape[-1]
            ks.append(k.reshape(bsz, seq, n_heads, HEAD_DIM))
            vs.append(v.reshape(bsz, seq, n_heads, HEAD_DIM))
            lfs.append(lf)
        x, fcb = _out_ffn(x, o, m["wo"], f["nw"], f["w1"], f["cw"], f["cb"], f["w2"],
                          jnp.zeros((bsz, 2, f["cw"].shape[1]), F32), decode=False)
        ffn_cb.append(fcb)
    return x, jnp.stack(gdn_s), jnp.stack(gdn_cb), jnp.stack(ks), jnp.stack(vs), jnp.stack(lfs), jnp.stack(ffn_cb)


def _trunk_decode(x, state_gdn, state_gdn_conv, cache_k, cache_v, cache_logf, state_ffn_conv, page_table, layers):
    nseq = x.shape[0]
    x = x.reshape(1, nseq, x.shape[-1])
    gdn_s, gdn_cb, ks, vs, lfs, ffn_cb = [], [], [], [], [], []
    for i, (m, f) in enumerate(layers):
        j = i // 2
        if i % 2 == 0:
            q, k, v, z, beta, g, cb = _gdn_in(x, (state_gdn_conv, j), m["nw"], m["w_in"], m["cw"], m["alog"],
                                              m["dtb"], decode=True)
            per_seq = lambda a: a.reshape(nseq, 1, a.shape[-1])
            o, s = _gdn_core_decode(per_seq(q), per_seq(k), per_seq(v), per_seq(z), per_seq(beta), per_seq(g),
                                    m["onw"], state_gdn[j])
            o = o.reshape(1, nseq, -1)
            gdn_cb.append(cb)
            gdn_s.append(s)
        else:
            q, k, v, og, lf, _ = _fox_in(x, m["nw"], m["w_in"], m["bf"], m["bft"], m["qn"], m["kn"])
            n_heads = lf.shape[-1]
            n_pool, page = cache_k.shape[1], cache_k.shape[2]
            heads = lambda a: a.reshape(nseq, n_heads, HEAD_DIM)
            o = _fox_attn_decode(page_table, heads(q), heads(k), heads(v), lf.reshape(nseq, 1, n_heads), heads(og),
                                 cache_k[j].reshape(n_pool, page * n_heads, HEAD_DIM),
                                 cache_v[j].reshape(n_pool, page * n_heads, HEAD_DIM),
                                 cache_logf[j].reshape(n_pool, 1, page * n_heads))
            o = o.reshape(1, nseq, -1)
            ks.append(k.reshape(nseq, 1, n_heads, HEAD_DIM))
            vs.append(v.reshape(nseq, 1, n_heads, HEAD_DIM))
            lfs.append(lf.reshape(nseq, 1, n_heads))
        x, fcb = _out_ffn(x, o, m["wo"], f["nw"], f["w1"], f["cw"], f["cb"], f["w2"],
                          (state_ffn_conv, i), decode=True)
        ffn_cb.append(fcb)
    return (x.reshape(nseq, 1, -1), jnp.stack(gdn_s), jnp.stack(gdn_cb), jnp.stack(ks), jnp.stack(vs),
            jnp.stack(lfs), jnp.stack(ffn_cb))


def kernel(x_prompt, x_sample, state_gdn, state_gdn_conv, cache_k, cache_v, cache_logf, state_ffn_conv, page_table,
           norm_mix, norm_ffn, gdn_w_in, gdn_conv_w, gdn_a_log, gdn_dt_bias, gdn_norm, gdn_w_out, fox_w_in, fox_b_f,
           fox_q_norm, fox_k_norm, fox_w_out, ffn_w_in, ffn_conv_w, ffn_conv_b, ffn_w_out):
    layers = _prepare_weights(norm_mix, norm_ffn, gdn_w_in, gdn_conv_w, gdn_a_log, gdn_dt_bias, gdn_norm, gdn_w_out,
                              fox_w_in, fox_b_f, fox_q_norm, fox_k_norm, fox_w_out, ffn_w_in, ffn_conv_w, ffn_conv_b,
                              ffn_w_out)
    prompt = _trunk_prompt(x_prompt, layers)
    sample = _trunk_decode(x_sample, state_gdn, state_gdn_conv, cache_k, cache_v, cache_logf, state_ffn_conv,
                           page_table, layers)
    return (prompt[0], sample[0]) + prompt[1:] + sample[1:]
```

```python
import functools

import jax
import jax.numpy as jnp
from jax import lax
from jax.experimental import pallas as pl
from jax.experimental.pallas import tpu as pltpu

F32 = jnp.float32
BF16 = jnp.bfloat16

HEAD_DIM = 128
GDN_CHUNK = 64
EPS = 1e-6
V7X_VMEM_LIMIT_BYTES = 56 * 1024 * 1024
ROW_TILE = 512
ATTN_TILE = 512
CONV_COLS = 512
FFN_COLS = 256
GDN_HEADS_PER_STEP = 8
GDN_SEQ_TILE = 1024
GDN_SOLVE_BATCH = 16
GDN_DECODE_SEQS = 4
GDN_SOLVE_HEADS = 2


def _cparams(*semantics):
    return pltpu.CompilerParams(dimension_semantics=semantics, vmem_limit_bytes=V7X_VMEM_LIMIT_BYTES)


def _resident(shape):
    return pl.BlockSpec(shape, lambda *_: (0,) * len(shape), pipeline_mode=pl.Buffered(1))


def _layer_of(w):
    stacked, layer = w
    shape = stacked.shape[1:]
    return pl.BlockSpec((None,) + shape, lambda *_: (layer,) + (0,) * len(shape), pipeline_mode=pl.Buffered(1))


def _sigmoid(x):
    return 1.0 / (1.0 + jnp.exp(-x))


def _silu(x):
    return x * _sigmoid(x)


def _softplus(x):
    return jnp.maximum(x, 0.0) + jnp.log(1.0 + jnp.exp(-jnp.abs(x)))


def _rmsnorm_rows(x, w):
    return x * lax.rsqrt(jnp.mean(x * x, axis=-1, keepdims=True) + EPS) * w


def _dot(a, b):
    return jnp.dot(a, b, preferred_element_type=F32)


def _dot_nt(a, b):
    return lax.dot_general(a, b, (((1,), (1,)), ((), ())), preferred_element_type=F32)


def _dot_tn(a, b):
    return lax.dot_general(a, b, (((0,), (0,)), ((), ())), preferred_element_type=F32)


def _split3(x):
    hi = x.astype(BF16)
    r = x - hi.astype(F32)
    mid = r.astype(BF16)
    lo = (r - mid.astype(F32)).astype(BF16)
    return hi, mid, lo


def _dot_sel_left(sel, x):
    hi, mid, lo = _split3(x)
    return _dot(sel, hi) + _dot(sel, mid) + _dot(sel, lo)


def _dot_sel_right(x, sel):
    hi, mid, lo = _split3(x)
    return _dot(hi, sel) + _dot(mid, sel) + _dot(lo, sel)


def _dot_f32(a, b):
    return jnp.dot(a, b, preferred_element_type=F32, precision=lax.Precision.HIGHEST)


def _iota2(shape, dim):
    return lax.broadcasted_iota(jnp.int32, shape, dim)


def _gdn_in_kernel(x_ref, nw_ref, w_ref, cw_ref, alog_ref, dtb_ref, buf_ref,
                   q_ref, k_ref, v_ref, z_ref, beta_ref, g_ref, cs_ref, ubuf, carry,
                   *, decode, tm, qk_width, conv_dim, n_vheads):
    z_cols = slice(conv_dim, conv_dim + n_vheads * HEAD_DIM)
    gate_cols = slice(conv_dim + n_vheads * HEAD_DIM, conv_dim + n_vheads * HEAD_DIM + 2 * n_vheads)
    hb = _rmsnorm_rows(x_ref[...], nw_ref[...]).astype(BF16)

    if not decode:
        @pl.when(pl.program_id(1) == 0)
        def _():
            carry[...] = jnp.zeros_like(carry)
            carry[5:8, :] = buf_ref[...]

    wc = CONV_COLS
    ahead = _dot(hb, w_ref[:, 0:wc])
    for j in range(conv_dim // wc):
        c0 = j * wc
        cols = slice(c0, c0 + wc)
        u = ahead
        if j + 1 < conv_dim // wc:
            ahead = _dot(hb, w_ref[:, c0 + wc:c0 + 2 * wc])
        else:
            ahead = _dot(hb, w_ref[:, z_cols])
        if decode:
            b0, b1, b2 = buf_ref[:, 0, cols], buf_ref[:, 1, cols], buf_ref[:, 2, cols]
            y = cw_ref[0:1, cols] * b0 + cw_ref[1:2, cols] * b1 + cw_ref[2:3, cols] * b2 + cw_ref[3:4, cols] * u
            cs_ref[:, 0, cols] = b1
            cs_ref[:, 1, cols] = b2
            cs_ref[:, 2, cols] = u
        else:
            ubuf[0:8, :] = carry[:, cols]
            ubuf[8:8 + tm, :] = u
            y = (cw_ref[3:4, cols] * u + cw_ref[2:3, cols] * ubuf[7:7 + tm, :]
                 + cw_ref[1:2, cols] * ubuf[6:6 + tm, :] + cw_ref[0:1, cols] * ubuf[5:5 + tm, :])
            carry[:, cols] = ubuf[tm:tm + 8, :]
        y = _silu(y)
        for hh in range(wc // HEAD_DIM):
            yh = y[:, hh * HEAD_DIM:(hh + 1) * HEAD_DIM]
            col = c0 + hh * HEAD_DIM
            if col < 2 * qk_width:
                yh = yh * lax.rsqrt(jnp.sum(yh * yh, axis=-1, keepdims=True) + EPS)
            dst, col = ((q_ref, col) if col < qk_width else
                        (k_ref, col - qk_width) if col < 2 * qk_width else (v_ref, col - 2 * qk_width))
            if decode:
                dst[:, col:col + HEAD_DIM] = yh
            else:
                dst[col // HEAD_DIM] = yh
    if not decode:
        cs_ref[...] = carry[5:8, :]

    z = ahead
    if decode:
        z_ref[...] = z
    else:
        for h in range(n_vheads):
            z_ref[h] = z[:, h * HEAD_DIM:(h + 1) * HEAD_DIM]
    ba = _dot(hb, w_ref[:, gate_cols])
    beta_ref[...] = _sigmoid(ba[:, :n_vheads])
    g = -jnp.exp(alog_ref[...]) * _softplus(ba[:, n_vheads:] + dtb_ref[...])
    if decode:
        g_ref[...] = g
    else:
        r, c = _iota2((tm, tm), 0), _iota2((tm, tm), 1)
        same_chunk_lower = jnp.where((r // GDN_CHUNK == c // GDN_CHUNK) & (c <= r), 1.0, 0.0).astype(BF16)
        g_ref[...] = _dot_sel_left(same_chunk_lower, g)


def _gdn_in(x, buf, nw, w_in, cw, alog, dtb, *, decode):
    bsz, seq, dm = x.shape
    conv_dim = cw.shape[1]
    vw = (w_in[0].shape[-1] - conv_dim) * HEAD_DIM // (HEAD_DIM + 2)
    qk_width = (conv_dim - vw) // 2
    n_vheads = vw // HEAD_DIM
    tm = min(ROW_TILE, seq)
    grid = (bsz, seq // tm)
    row = lambda width: pl.BlockSpec((None, tm, width), lambda b, i: (b, i, 0))
    if decode:
        buf, layer = buf
        buf_spec = pl.BlockSpec((None, tm, 3, conv_dim), lambda b, i: (layer, i, 0, 0))
        cs_shape, cs_spec = (seq, 3, conv_dim), pl.BlockSpec((tm, 3, conv_dim), lambda b, i: (i, 0, 0))
    else:
        buf_spec = pl.BlockSpec((None, 3, conv_dim), lambda b, i: (b, 0, 0))
        cs_shape, cs_spec = (bsz, 3, conv_dim), pl.BlockSpec((None, 3, conv_dim), lambda b, i: (b, 0, 0))
    kern = functools.partial(_gdn_in_kernel, decode=decode, tm=tm, qk_width=qk_width, conv_dim=conv_dim,
                             n_vheads=n_vheads)
    if decode:
        wide = lambda width: (row(width), jax.ShapeDtypeStruct((bsz, seq, width), F32))
    else:
        wide = lambda width: (pl.BlockSpec((None, width // HEAD_DIM, tm, HEAD_DIM), lambda b, i: (b, 0, i, 0)),
                              jax.ShapeDtypeStruct((bsz, width // HEAD_DIM, seq, HEAD_DIM), F32))
    (q_spec, q_shape), (v_spec, v_shape) = wide(qk_width), wide(vw)
    gate_shape = jax.ShapeDtypeStruct((bsz, seq, n_vheads), F32)
    return pl.pallas_call(
        kern,
        grid=grid,
        in_specs=[row(dm), _resident((1, dm)), _layer_of(w_in), _resident(cw.shape), _resident((1, n_vheads)),
                  _resident((1, n_vheads)), buf_spec],
        out_specs=[q_spec, q_spec, v_spec, v_spec, row(n_vheads), row(n_vheads), cs_spec],
        out_shape=[q_shape, q_shape, v_shape, v_shape, gate_shape, gate_shape, jax.ShapeDtypeStruct(cs_shape, F32)],
        scratch_shapes=[pltpu.VMEM((tm + 8, CONV_COLS), F32), pltpu.VMEM((8, conv_dim), F32)],
        compiler_params=_cparams("arbitrary", "arbitrary"),
        name="gdn_in_decode" if decode else "gdn_in_prompt",
    )(x, nw, w_in[0], cw, alog, dtb, buf)


def _bdot(a, b):
    return lax.dot_general(a, b, (((2,), (1,)), ((0,), (0,))), preferred_element_type=F32)


def _bdot_nt(a, b):
    return lax.dot_general(a, b, (((2,), (2,)), ((0,), (0,))), preferred_element_type=F32)


def _unit_lower_inverse(n, row, col):
    eye = jnp.where(row == col, 1.0, 0.0)
    inv = eye - jnp.where((row >> 1) == (col >> 1), n, 0.0)
    size = n.shape[-1]
    bits = 2
    while (1 << bits) <= size:
        off_diag = jnp.where((row >> bits) == (col >> bits),
                             jnp.where((row >> (bits - 1)) == (col >> (bits - 1)), 0.0, n), 0.0)
        inv_b = inv.astype(BF16)
        inv = inv - _bdot(inv_b, _bdot(off_diag.astype(BF16), inv_b).astype(BF16))
        bits += 1
    n_hi = n.astype(BF16).astype(F32)
    x_hi = inv.astype(BF16).astype(F32)
    lhs = jnp.concatenate([n_hi, n - n_hi, n_hi], axis=-1).astype(BF16)
    rhs = jnp.concatenate([x_hi, x_hi, inv - x_hi], axis=1).astype(BF16)
    resid = eye - inv - _bdot(lhs, rhs)
    return inv + _bdot(x_hi.astype(BF16), resid.astype(BF16))


def _gdn_core_kernel(q_ref, k_ref, v_ref, z_ref, beta_ref, gcum_ref, nw_ref, o_ref, s_out_ref,
                     s_scr, u_scr, w_scr, att_scr, qg_scr, kd_scr, egl_scr, gb_scr, *, lt, hb, rep, nb, hs):
    c = GDN_CHUNK
    groups = lt // c // nb
    t = pl.program_id(2)
    head0 = pl.program_id(1) * hb
    q_scale = HEAD_DIM ** -0.5

    @pl.when(t == 0)
    def _():
        s_scr[...] = jnp.zeros_like(s_scr)

    nbh = nb * hs
    row = lax.broadcasted_iota(jnp.int32, (nbh, c, c), 1)
    col = lax.broadcasted_iota(jnp.int32, (nbh, c, c), 2)
    lower = row >= col
    strict = row > col
    lane = _iota2((nb * c, HEAD_DIM), 1)
    gate_lane = _iota2((nb * c, beta_ref.shape[-1]), 1)
    pick = jnp.where(_iota2((nbh * c, HEAD_DIM), 1) < 3, 1.0, 0.0).astype(BF16).reshape(nbh, c, HEAD_DIM)

    def solve(idx, _):
        hset = idx // groups
        chunk0 = pl.multiple_of((idx - hset * groups) * nb, nb)
        rows = pl.ds(pl.multiple_of(chunk0 * c, nb * c), nb * c)
        heads = [hset * hs + m for m in range(hs)]
        per_head = []
        for hh in heads:
            own = gate_lane == head0 + hh
            bcol = jnp.sum(jnp.where(own, beta_ref[rows, :], 0.0), axis=-1, keepdims=True)
            gcol = jnp.sum(jnp.where(own, gcum_ref[rows, :], 0.0), axis=-1, keepdims=True)
            g_hi = gcol.astype(BF16).astype(F32)
            g_mid = (gcol - g_hi).astype(BF16).astype(F32)
            g_lo = gcol - g_hi - g_mid
            pieces = jnp.where(lane == 0, g_hi, jnp.where(lane == 1, g_mid, jnp.where(lane == 2, g_lo, 0.0)))
            per_head.append((gcol, bcol, pieces.astype(BF16), q_ref[hh // rep, rows, :] * q_scale,
                             k_ref[hh // rep, rows, :], v_ref[hh, rows, :]))
        stack = lambda i, width: jnp.concatenate([p[i].reshape(nb, c, width) for p in per_head], axis=0)
        g3, b3 = stack(0, 1), stack(1, 1)
        gcum_t = _bdot_nt(pick, stack(2, HEAD_DIM))
        decay = jnp.where(lower, jnp.exp(jnp.minimum(g3 - gcum_t, 0.0)), 0.0)
        q3, k3, v3 = stack(3, HEAD_DIM), stack(4, HEAD_DIM), stack(5, HEAD_DIM)
        kb = k3 * b3
        k_bf = k3.astype(BF16)
        n = jnp.where(strict, _bdot_nt(kb.astype(BF16), k_bf) * decay, 0.0)
        inv = _unit_lower_inverse(n, row, col)
        eg = jnp.exp(g3)
        rhs = jnp.concatenate([v3 * b3, kb * eg], axis=-1).astype(BF16)
        sol = _bdot(inv.astype(BF16), rhs)
        attn = jnp.where(lower, _bdot_nt(q3.astype(BF16), k_bf) * decay, 0.0)
        g_last = g3[:, c - 1:c, :]
        qg = q3 * eg
        kd = k3 * jnp.exp(g_last - g3)
        for m, hh in enumerate(heads):
            mine = lambda a: a[m * nb:(m + 1) * nb].reshape(nb * c, a.shape[-1])
            u_scr[hh, rows, :] = mine(sol[:, :, :HEAD_DIM])
            w_scr[hh, rows, :] = mine(sol[:, :, HEAD_DIM:]).astype(BF16)
            att_scr[hh, rows, :] = mine(attn).astype(BF16)
            qg_scr[hh, rows, :] = mine(qg).astype(BF16)
            kd_scr[hh, rows, :] = mine(kd).astype(BF16)
            gb_scr[m] = jnp.broadcast_to(per_head[m][0], gb_scr.shape[1:])
            egl_scr[hh, pl.ds(chunk0, nb), :] = jnp.exp(gb_scr[m, pl.ds(c - 1, nb, stride=c), :])
        return 0

    lax.fori_loop(0, hb // hs * groups, solve, 0)

    def step(i, _):
        rows = pl.ds(pl.multiple_of(i * c, c), c)
        heads = range(hb)
        state = [s_scr[hh] for hh in heads]
        res = [_dot(jnp.concatenate([w_scr[hh, rows, :], qg_scr[hh, rows, :]], axis=0), state[hh].astype(BF16))
               for hh in heads]
        v_new = [(u_scr[hh, rows, :] - res[hh][:c]).astype(BF16) for hh in heads]
        o = [res[hh][c:] + _dot(att_scr[hh, rows, :], v_new[hh]) for hh in heads]
        grown = [_dot_tn(kd_scr[hh, rows, :], v_new[hh]) for hh in heads]
        for hh in heads:
            s_scr[hh] = state[hh] * egl_scr[hh, pl.ds(i, 1), :] + grown[hh]
            gated = _rmsnorm_rows(o[hh], nw_ref[...]) * _silu(z_ref[hh, rows, :])
            o_ref[rows, hh * HEAD_DIM:(hh + 1) * HEAD_DIM] = gated.astype(o_ref.dtype)
        return 0

    lax.fori_loop(0, lt // c, step, 0)

    @pl.when(t == pl.num_programs(2) - 1)
    def _():
        s_out_ref[...] = s_scr[...]


def _gdn_core_prompt(q, k, v, z, beta, gcum, nw):
    bsz, n_vheads, seq, _ = v.shape
    rep = n_vheads // q.shape[1]
    hb = min(GDN_HEADS_PER_STEP, n_vheads)
    lt = min(GDN_SEQ_TILE, seq)
    nb = min(GDN_SOLVE_BATCH, lt // GDN_CHUNK)
    hs = min(GDN_SOLVE_HEADS, hb)
    heads = lambda n: pl.BlockSpec((None, n, lt, HEAD_DIM), lambda b, h, t: (b, h, t, 0))
    gates = pl.BlockSpec((None, lt, n_vheads), lambda b, h, t: (b, t, 0))
    per_head = lambda width, dtype: pltpu.VMEM((hb, lt, width), dtype)
    return pl.pallas_call(
        functools.partial(_gdn_core_kernel, lt=lt, hb=hb, rep=rep, nb=nb, hs=hs),
        grid=(bsz, n_vheads // hb, seq // lt),
        in_specs=[heads(hb // rep), heads(hb // rep), heads(hb), heads(hb), gates, gates, _resident((1, HEAD_DIM))],
        out_specs=[pl.BlockSpec((None, lt, hb * HEAD_DIM), lambda b, h, t: (b, t, h)),
                   pl.BlockSpec((None, hb, HEAD_DIM, HEAD_DIM), lambda b, h, t: (b, h, 0, 0))],
        out_shape=[jax.ShapeDtypeStruct((bsz, seq, n_vheads * HEAD_DIM), BF16),
                   jax.ShapeDtypeStruct((bsz, n_vheads, HEAD_DIM, HEAD_DIM), F32)],
        scratch_shapes=[pltpu.VMEM((hb, HEAD_DIM, HEAD_DIM), F32), per_head(HEAD_DIM, F32), per_head(HEAD_DIM, BF16),
                        per_head(GDN_CHUNK, BF16), per_head(HEAD_DIM, BF16), per_head(HEAD_DIM, BF16),
                        pltpu.VMEM((hb, lt // GDN_CHUNK, HEAD_DIM), F32),
                        pltpu.VMEM((hs, nb * GDN_CHUNK, HEAD_DIM), F32)],
        compiler_params=_cparams("arbitrary", "arbitrary", "arbitrary"),
        name="gdn_core_prompt",
    )(q, k, v, z, beta, gcum, nw)


def _gdn_core_decode_kernel(q_ref, k_ref, v_ref, z_ref, beta_ref, g_ref, nw_ref, s_ref, o_ref, so_ref,
                            *, n_vheads, rep):
    q_scale = HEAD_DIM ** -0.5
    row = _iota2((8, HEAD_DIM), 0)

    def rows3(a, b, c):
        return jnp.where(row == 0, a, jnp.where(row == 1, b, jnp.where(row == 2, c, 0.0)))

    pairs = [(s, h) for s in range(q_ref.shape[0]) for h in range(n_vheads)]
    idx = range(len(pairs))
    lanes = [slice(h * HEAD_DIM, (h + 1) * HEAD_DIM) for _, h in pairs]
    klanes = [slice(h // rep * HEAD_DIM, (h // rep + 1) * HEAD_DIM) for _, h in pairs]
    qh = [q_ref[s, :, klanes[i]] * q_scale for i, (s, _) in enumerate(pairs)]
    kh = [k_ref[s, :, klanes[i]] for i, (s, _) in enumerate(pairs)]
    beta = [beta_ref[s, :, h:h + 1] for s, h in pairs]
    eg = [jnp.exp(g_ref[s, :, h:h + 1]) for s, h in pairs]
    res = [_dot(rows3(kh[i] * beta[i] * eg[i], qh[i] * eg[i], 0.0).astype(BF16), s_ref[pairs[i]].astype(BF16))
           for i in idx]
    v_new = [v_ref[pairs[i][0], :, lanes[i]] * beta[i] - res[i][0:1, :] for i in idx]
    k_hi = [kh[i].astype(BF16).astype(F32) for i in idx]
    v_hi = [v_new[i].astype(BF16).astype(F32) for i in idx]
    grown = [_dot_tn(rows3(k_hi[i], kh[i] - k_hi[i], k_hi[i]).astype(BF16),
                     rows3(v_hi[i], v_hi[i], v_new[i] - v_hi[i]).astype(BF16)) for i in idx]
    for i in idx:
        s = pairs[i][0]
        so_ref[pairs[i]] = s_ref[pairs[i]] * eg[i] + grown[i]
        o = res[i][1:2, :] + jnp.sum(qh[i] * kh[i], axis=-1, keepdims=True) * v_new[i]
        gated = _rmsnorm_rows(o, nw_ref[...]) * _silu(z_ref[s, :, lanes[i]])
        o_ref[s, :, lanes[i]] = gated.astype(o_ref.dtype)


def _gdn_core_decode(q, k, v, z, beta, g, nw, state):
    nseq, _, vw = v.shape
    n_vheads = vw // HEAD_DIM
    rep = vw // q.shape[-1]
    sb = GDN_DECODE_SEQS if nseq % GDN_DECODE_SEQS == 0 else 1
    row = lambda width: pl.BlockSpec((sb, 1, width), lambda s: (s, 0, 0))
    st = pl.BlockSpec((sb, n_vheads, HEAD_DIM, HEAD_DIM), lambda s: (s, 0, 0, 0))
    return pl.pallas_call(
        functools.partial(_gdn_core_decode_kernel, n_vheads=n_vheads, rep=rep),
        grid=(nseq // sb,),
        in_specs=[row(q.shape[-1]), row(q.shape[-1]), row(vw), row(vw), row(n_vheads), row(n_vheads),
                  _resident((1, HEAD_DIM)), st],
        out_specs=[row(vw), st],
        out_shape=[jax.ShapeDtypeStruct((nseq, 1, vw), BF16), jax.ShapeDtypeStruct(state.shape, F32)],
        compiler_params=_cparams("arbitrary"),
        name="gdn_core_decode",
    )(q, k, v, z, beta, g, nw, state)


def _out_ffn_kernel(x_ref, o_ref, wo_ref, nw_ref, w1_ref, cw_ref, cb_ref, w2_ref, buf_ref,
                    y_ref, cs_ref, gbuf, ubuf, carry, act, *, decode, tm, d_ff):
    x1 = x_ref[...] + _dot(o_ref[...], wo_ref[...])
    hb = _rmsnorm_rows(x1, nw_ref[...]).astype(BF16)

    if not decode:
        @pl.when(pl.program_id(1) == 0)
        def _():
            carry[...] = jnp.zeros_like(carry)
            carry[6:8, :] = buf_ref[...]

    def conv(u, scratch, c0, wc):
        cols = slice(c0, c0 + wc)
        if decode:
            b0 = buf_ref[:, 0, cols]
            b1 = buf_ref[:, 1, cols]
            cs_ref[:, 0, cols] = b1
            cs_ref[:, 1, cols] = u
            y = cw_ref[0:1, cols] * b0 + cw_ref[1:2, cols] * b1 + cw_ref[2:3, cols] * u
        else:
            scratch[0:8, :] = carry[:, cols]
            scratch[8:8 + tm, :] = u
            y = (cw_ref[2:3, cols] * u + cw_ref[1:2, cols] * scratch[7:7 + tm, :]
                 + cw_ref[0:1, cols] * scratch[6:6 + tm, :])
            carry[:, cols] = scratch[tm:tm + 8, :]
        return y + cb_ref[:, cols]

    wc = FFN_COLS

    def project(j):
        c0 = j * wc
        return _dot(hb, w1_ref[:, c0:c0 + wc]), _dot(hb, w1_ref[:, d_ff + c0:d_ff + c0 + wc])

    ahead = project(0)
    for j in range(d_ff // wc):
        c0 = j * wc
        pre_gate, pre_up = ahead
        if j + 1 < d_ff // wc:
            ahead = project(j + 1)
        gate = conv(pre_gate, gbuf, c0, wc)
        up = conv(pre_up, ubuf, d_ff + c0, wc)
        act[:, c0:c0 + wc] = (_silu(gate) * up).astype(BF16)
    if not decode:
        cs_ref[...] = carry[6:8, :]
    y_ref[...] = x1 + _dot(act[...], w2_ref[...])


def _out_ffn(x, o, wo, nw, w1, cw, cb, w2, buf, *, decode):
    bsz, seq, dm = x.shape
    d_ff = w2[0].shape[1]
    tm = min(ROW_TILE, seq)
    row = lambda width: pl.BlockSpec((None, tm, width), lambda b, i: (b, i, 0))
    if decode:
        buf, layer = buf
        buf_spec = pl.BlockSpec((None, tm, 2, 2 * d_ff), lambda b, i: (layer, i, 0, 0))
        cs_shape, cs_spec = (seq, 2, 2 * d_ff), pl.BlockSpec((tm, 2, 2 * d_ff), lambda b, i: (i, 0, 0))
    else:
        buf_spec = pl.BlockSpec((None, 2, 2 * d_ff), lambda b, i: (b, 0, 0))
        cs_shape, cs_spec = (bsz, 2, 2 * d_ff), pl.BlockSpec((None, 2, 2 * d_ff), lambda b, i: (b, 0, 0))
    return pl.pallas_call(
        functools.partial(_out_ffn_kernel, decode=decode, tm=tm, d_ff=d_ff),
        grid=(bsz, seq // tm),
        in_specs=[row(dm), row(o.shape[-1]), _layer_of(wo), _resident((1, dm)), _layer_of(w1),
                  _resident(cw.shape), _resident((1, 2 * d_ff)), _layer_of(w2), buf_spec],
        out_specs=[row(dm), cs_spec],
        out_shape=[jax.ShapeDtypeStruct((bsz, seq, dm), F32), jax.ShapeDtypeStruct(cs_shape, F32)],
        scratch_shapes=[pltpu.VMEM((tm + 8, FFN_COLS), F32), pltpu.VMEM((tm + 8, FFN_COLS), F32),
                        pltpu.VMEM((8, 2 * d_ff), F32), pltpu.VMEM((tm, d_ff), BF16)],
        compiler_params=_cparams("arbitrary", "arbitrary"),
        name="out_ffn_decode" if decode else "out_ffn_prompt",
    )(x, o, wo[0], nw, w1[0], cw, cb, w2[0], buf)


def _fox_in_kernel(x_ref, nw_ref, w_ref, bf_ref, bft_ref, qn_ref, kn_ref,
                   q_ref, k_ref, v_ref, og_ref, lf_ref, cum_ref, carry, *, tm, n_heads):
    width = n_heads * HEAD_DIM
    hb = _rmsnorm_rows(x_ref[...], nw_ref[...]).astype(BF16)
    q = _dot(hb, w_ref[:, 0:width])
    k = _dot(hb, w_ref[:, width:2 * width])
    for h in range(n_heads):
        lanes = slice(h * HEAD_DIM, (h + 1) * HEAD_DIM)
        q_ref[:, lanes] = _rmsnorm_rows(q[:, lanes], qn_ref[...])
    v_ref[...] = _dot(hb, w_ref[:, 2 * width:3 * width])
    for h in range(n_heads):
        lanes = slice(h * HEAD_DIM, (h + 1) * HEAD_DIM)
        k_ref[:, lanes] = _rmsnorm_rows(k[:, lanes], kn_ref[...])
    og_ref[...] = _dot(hb, w_ref[:, 3 * width:4 * width])
    lf_ref[...] = -_softplus(-(_dot(hb, w_ref[:, 4 * width:4 * width + n_heads]) + bf_ref[...]))

    @pl.when(pl.program_id(1) == 0)
    def _():
        carry[...] = jnp.zeros_like(carry)

    logits_t = lax.dot_general(w_ref[:, 4 * width:4 * width + n_heads], hb, (((0,), (1,)), ((), ())),
                               preferred_element_type=F32)
    lf_t = -_softplus(-(logits_t + bft_ref[...]))
    upper = jnp.where(_iota2((tm, tm), 0) <= _iota2((tm, tm), 1), 1.0, 0.0).astype(BF16)
    cum = _dot_sel_right(lf_t, upper) + carry[:, 0:1]
    cum_ref[...] = cum
    carry[...] = jnp.broadcast_to(cum[:, tm - 1:tm], carry.shape)


def _fox_in(x, nw, w_in, bf, bft, qn, kn):
    bsz, seq, dm = x.shape
    n_heads = bf.shape[1]
    width = n_heads * HEAD_DIM
    tm = min(ROW_TILE, seq)
    row = lambda w: pl.BlockSpec((None, tm, w), lambda b, i: (b, i, 0))
    wide = jax.ShapeDtypeStruct((bsz, seq, width), F32)
    return pl.pallas_call(
        functools.partial(_fox_in_kernel, tm=tm, n_heads=n_heads),
        grid=(bsz, seq // tm),
        in_specs=[row(dm), _resident((1, dm)), _layer_of(w_in), _resident((1, n_heads)),
                  _resident((n_heads, 1)), _resident((1, HEAD_DIM)), _resident((1, HEAD_DIM))],
        out_specs=[row(width), row(width), row(width), row(width), row(n_heads),
                   pl.BlockSpec((None, n_heads, tm), lambda b, i: (b, 0, i))],
        out_shape=[wide, wide, wide, wide, jax.ShapeDtypeStruct((bsz, seq, n_heads), F32),
                   jax.ShapeDtypeStruct((bsz, n_heads, seq), F32)],
        scratch_shapes=[pltpu.VMEM((n_heads, HEAD_DIM), F32)],
        compiler_params=_cparams("arbitrary", "arbitrary"),
        name="fox_in",
    )(x, nw, w_in[0], bf, bft, qn, kn)


def _fox_attn_kernel(q_ref, k_ref, v_ref, cum_ref, og_ref, o_ref, kb_scr, vb_scr, *, tile):
    h = pl.program_id(1)
    qi = pl.program_id(2)
    scale = HEAD_DIM ** -0.5
    exp2_scale = scale * 1.4426950408889634

    @pl.when(qi == 0)
    def _():
        kb_scr[...] = k_ref[...].astype(BF16)
        vb_scr[...] = v_ref[...].astype(BF16)

    qb = q_ref[...].astype(BF16)

    def lane_tiles(x):
        return [x[:, t * HEAD_DIM:(t + 1) * HEAD_DIM] for t in range(tile // HEAD_DIM)]

    def key_block(j, carry, diagonal):
        m_prev, l_prev, acc = carry
        cols = pl.ds(pl.multiple_of(j * tile, tile), tile)
        s = _dot_nt(qb, kb_scr[cols, :]) - cum_ref[pl.ds(h, 1), cols] * (1.0 / scale)
        if diagonal:
            s = jnp.where(_iota2((tile, tile), 1) <= _iota2((tile, tile), 0), s, -jnp.inf)
        m_new = jnp.maximum(m_prev, jnp.max(functools.reduce(jnp.maximum, lane_tiles(s)), axis=-1, keepdims=True))
        alpha = jnp.exp2((m_prev - m_new) * exp2_scale)
        p = jnp.exp2((s - m_new) * exp2_scale)
        l_new = alpha * l_prev + jnp.sum(functools.reduce(jnp.add, lane_tiles(p)), axis=-1, keepdims=True)
        acc = alpha * acc + _dot(p.astype(BF16), vb_scr[cols, :])
        return m_new, l_new, acc

    init = (jnp.full((tile, 1), -jnp.inf, F32), jnp.zeros((tile, 1), F32), jnp.zeros((tile, HEAD_DIM), F32))
    carry = lax.fori_loop(0, qi, lambda j, c: key_block(j, c, False), init)
    _, l_fin, acc = key_block(qi, carry, True)
    o_ref[...] = (acc / l_fin * _sigmoid(og_ref[...])).astype(o_ref.dtype)


def _fox_attn_prompt(q, k, v, cum, og):
    bsz, seq, width = q.shape
    n_heads = width // HEAD_DIM
    tile = min(ATTN_TILE, seq)
    q_spec = pl.BlockSpec((None, tile, HEAD_DIM), lambda b, h, i: (b, i, h))
    kv_spec = pl.BlockSpec((None, seq, HEAD_DIM), lambda b, h, i: (b, 0, h))
    cum_spec = pl.BlockSpec((None, n_heads, seq), lambda b, h, i: (b, 0, 0))
    return pl.pallas_call(
        functools.partial(_fox_attn_kernel, tile=tile),
        grid=(bsz, n_heads, seq // tile),
        in_specs=[q_spec, kv_spec, kv_spec, cum_spec, q_spec],
        out_specs=q_spec,
        out_shape=jax.ShapeDtypeStruct((bsz, seq, width), BF16),
        scratch_shapes=[pltpu.VMEM((seq, HEAD_DIM), BF16), pltpu.VMEM((seq, HEAD_DIM), BF16)],
        compiler_params=_cparams("arbitrary", "arbitrary", "arbitrary"),
        name="fox_attn_prompt",
    )(q, k, v, cum, og)


def _fox_attn_decode_kernel(pt_ref, q_ref, kn_ref, vn_ref, lfn_ref, og_ref, *rest, n_heads, page, n_pages):
    del pt_ref
    k_pages, v_pages, lf_pages = rest[:n_pages], rest[n_pages:2 * n_pages], rest[2 * n_pages:3 * n_pages]
    o_ref, lf_scr = rest[3 * n_pages:]
    flat = n_heads * page
    scale = HEAD_DIM ** -0.5

    for p in range(n_pages):
        lf_scr[p:p + 1, :] = lf_pages[p][...]
    lf = lf_scr[...]
    lane = _iota2((n_pages, flat), 1)
    cum, page_total = lf, lf
    shift = n_heads
    while shift < flat:
        cum = cum + jnp.where(lane >= shift, pltpu.roll(cum, shift, axis=1), 0.0)
        page_total = page_total + pltpu.roll(page_total, shift, axis=1)
        shift *= 2
    earlier = jnp.where(_iota2((n_pages, n_pages), 1) < _iota2((n_pages, n_pages), 0), 1.0, 0.0).astype(BF16)
    before = _dot_sel_left(earlier, page_total)
    cum = cum + before
    past_total = before[n_pages - 1:n_pages, :] + page_total[n_pages - 1:n_pages, :]

    own_head = (_iota2((n_heads, flat), 1) & (n_heads - 1)) == _iota2((n_heads, flat), 0)
    eye = _iota2((n_heads, n_heads), 0) == _iota2((n_heads, n_heads), 1)
    to_rows = lambda r: jnp.sum(jnp.where(eye, jnp.broadcast_to(r, (n_heads, n_heads)), 0.0), axis=-1, keepdims=True)

    q = q_ref[...]
    qb = q.astype(BF16)
    scores = []
    for p in range(n_pages):
        s = _dot_nt(qb, k_pages[p][...].astype(BF16)) * scale - cum[p:p + 1, :]
        scores.append(jnp.where(own_head, s, -jnp.inf))
    s_new = (jnp.sum(q * kn_ref[...], axis=-1, keepdims=True) * scale
             - (to_rows(past_total[:, :n_heads]) + to_rows(lfn_ref[...])))
    m = s_new
    for s in scores:
        m = jnp.maximum(m, jnp.max(s, axis=-1, keepdims=True))
    denom = jnp.exp(s_new - m)
    acc = denom * vn_ref[...]
    for p in range(n_pages):
        w = jnp.exp(scores[p] - m)
        denom = denom + jnp.sum(w, axis=-1, keepdims=True)
        acc = acc + _dot(w.astype(BF16), v_pages[p][...].astype(BF16))
    o_ref[...] = (acc / denom * _sigmoid(og_ref[...])).astype(o_ref.dtype)


def _fox_attn_decode(page_table, q, k_new, v_new, lf_new, og, k_pool, v_pool, lf_pool):
    nseq, n_heads, _ = q.shape
    n_pages = page_table.shape[1]
    flat = k_pool.shape[1]
    page = flat // n_heads
    tok = pl.BlockSpec((None, n_heads, HEAD_DIM), lambda s, pt: (s, 0, 0))
    pool = [pl.BlockSpec((None, flat, HEAD_DIM), lambda s, pt, p=p: (pt[s, p], 0, 0)) for p in range(n_pages)]
    lf_specs = [pl.BlockSpec((None, 1, flat), lambda s, pt, p=p: (pt[s, p], 0, 0)) for p in range(n_pages)]
    grid_spec = pltpu.PrefetchScalarGridSpec(
        num_scalar_prefetch=1,
        grid=(nseq,),
        in_specs=[tok, tok, tok, pl.BlockSpec((None, 1, n_heads), lambda s, pt: (s, 0, 0)), tok] + pool + pool + lf_specs,
        out_specs=tok,
        scratch_shapes=[pltpu.VMEM((n_pages, flat), F32)],
    )
    return pl.pallas_call(
        functools.partial(_fox_attn_decode_kernel, n_heads=n_heads, page=page, n_pages=n_pages),
        grid_spec=grid_spec,
        out_shape=jax.ShapeDtypeStruct((nseq, n_heads, HEAD_DIM), BF16),
        compiler_params=_cparams("arbitrary"),
        name="fox_attn_decode",
    )(page_table, q, k_new, v_new, lf_new, og, *([k_pool] * n_pages), *([v_pool] * n_pages), *([lf_pool] * n_pages))


def _prepare_weights(norm_mix, norm_ffn, gdn_w_in, gdn_conv_w, gdn_a_log, gdn_dt_bias, gdn_norm, gdn_w_out,
                     fox_w_in, fox_b_f, fox_q_norm, fox_k_norm, fox_w_out, ffn_w_in, ffn_conv_w, ffn_conv_b,
                     ffn_w_out):
    fw = fox_w_out.shape[1]
    gdn_in_b, gdn_out_b = gdn_w_in.astype(BF16), gdn_w_out.astype(BF16)
    fox_in_b, fox_out_b = fox_w_in.astype(BF16), fox_w_out.astype(BF16)
    ffn_in_b, ffn_out_b = ffn_w_in.astype(BF16), ffn_w_out.astype(BF16)
    layers = []
    for i in range(norm_mix.shape[0]):
        j = i // 2
        ffn = dict(nw=norm_ffn[i][None], w1=(ffn_in_b, i), cw=ffn_conv_w[i], cb=ffn_conv_b[i][None], w2=(ffn_out_b, i))
        if i % 2 == 0:
            mixer = dict(nw=norm_mix[i][None], w_in=(gdn_in_b, j), cw=gdn_conv_w[j], alog=gdn_a_log[j][None],
                         dtb=gdn_dt_bias[j][None], onw=gdn_norm[j][None], wo=(gdn_out_b, j))
        else:
            mixer = dict(nw=norm_mix[i][None], w_in=(fox_in_b, j),
                         bf=fox_b_f[j][None], bft=fox_b_f[j][:, None], qn=fox_q_norm[j][None],
                         kn=fox_k_norm[j][None], wo=(fox_out_b, j))
        layers.append((mixer, ffn))
    return layers


def _trunk_prompt(x, layers):
    bsz, seq, _ = x.shape
    gdn_s, gdn_cb, ks, vs, lfs, ffn_cb = [], [], [], [], [], []
    for i, (m, f) in enumerate(layers):
        if i % 2 == 0:
            conv_dim = m["cw"].shape[1]
            q, k, v, z, beta, g, cb = _gdn_in(x, jnp.zeros((bsz, 3, conv_dim), F32), m["nw"], m["w_in"], m["cw"],
                                              m["alog"], m["dtb"], decode=False)
            o, s = _gdn_core_prompt(q, k, v, z, beta, g, m["onw"])
            gdn_cb.append(cb)
            gdn_s.append(s)
        else:
            q, k, v, og, lf, cum = _fox_in(x, m["nw"], m["w_in"], m["bf"], m["bft"], m["qn"], m["kn"])
            o = _fox_attn_prompt(q, k, v, cum, og)
            n_heads = lf.shape[-1]
            ks.append(k.reshape(bsz, seq, n_heads, HEAD_DIM))
            vs.append(v.reshape(bsz, seq, n_heads, HEAD_DIM))
            lfs.append(lf)
        x, fcb = _out_ffn(x, o, m["wo"], f["nw"], f["w1"], f["cw"], f["cb"], f["w2"],
                          jnp.zeros((bsz, 2, f["cw"].shape[1]), F32), decode=False)
        ffn_cb.append(fcb)
    return x, jnp.stack(gdn_s), jnp.stack(gdn_cb), jnp.stack(ks), jnp.stack(vs), jnp.stack(lfs), jnp.stack(ffn_cb)


def _trunk_decode(x, state_gdn, state_gdn_conv, cache_k, cache_v, cache_logf, state_ffn_conv, page_table, layers):
    nseq = x.shape[0]
    x = x.reshape(1, nseq, x.shape[-1])
    gdn_s, gdn_cb, ks, vs, lfs, ffn_cb = [], [], [], [], [], []
    for i, (m, f) in enumerate(layers):
        j = i // 2
        if i % 2 == 0:
            q, k, v, z, beta, g, cb = _gdn_in(x, (state_gdn_conv, j), m["nw"], m["w_in"], m["cw"], m["alog"],
                                              m["dtb"], decode=True)
            per_seq = lambda a: a.reshape(nseq, 1, a.shape[-1])
            o, s = _gdn_core_decode(per_seq(q), per_seq(k), per_seq(v), per_seq(z), per_seq(beta), per_seq(g),
                                    m["onw"], state_gdn[j])
            o = o.reshape(1, nseq, -1)
            gdn_cb.append(cb)
            gdn_s.append(s)
        else:
            q, k, v, og, lf, _ = _fox_in(x, m["nw"], m["w_in"], m["bf"], m["bft"], m["qn"], m["kn"])
            n_heads = lf.shape[-1]
            n_pool, page = cache_k.shape[1], cache_k.shape[2]
            heads = lambda a: a.reshape(nseq, n_heads, HEAD_DIM)
            o = _fox_attn_decode(page_table, heads(q), heads(k), heads(v), lf.reshape(nseq, 1, n_heads), heads(og),
                                 cache_k[j].reshape(n_pool, page * n_heads, HEAD_DIM),
                                 cache_v[j].reshape(n_pool, page * n_heads, HEAD_DIM),
                                 cache_logf[j].reshape(n_pool, 1, page * n_heads))
            o = o.reshape(1, nseq, -1)
            ks.append(k.reshape(nseq, 1, n_heads, HEAD_DIM))
            vs.append(v.reshape(nseq, 1, n_heads, HEAD_DIM))
            lfs.append(lf.reshape(nseq, 1, n_heads))
        x, fcb = _out_ffn(x, o, m["wo"], f["nw"], f["w1"], f["cw"], f["cb"], f["w2"],
                          (state_ffn_conv, i), decode=True)
        ffn_cb.append(fcb)
    return (x.reshape(nseq, 1, -1), jnp.stack(gdn_s), jnp.stack(gdn_cb), jnp.stack(ks), jnp.stack(vs),
            jnp.stack(lfs), jnp.stack(ffn_cb))


def kernel(x_prompt, x_sample, state_gdn, state_gdn_conv, cache_k, cache_v, cache_logf, state_ffn_conv, page_table,
           norm_mix, norm_ffn, gdn_w_in, gdn_conv_w, gdn_a_log, gdn_dt_bias, gdn_norm, gdn_w_out, fox_w_in, fox_b_f,
           fox_q_norm, fox_k_norm, fox_w_out, ffn_w_in, ffn_conv_w, ffn_conv_b, ffn_w_out):
    layers = _prepare_weights(norm_mix, norm_ffn, gdn_w_in, gdn_conv_w, gdn_a_log, gdn_dt_bias, gdn_norm, gdn_w_out,
                              fox_w_in, fox_b_f, fox_q_norm, fox_k_norm, fox_w_out, ffn_w_in, ffn_conv_w, ffn_conv_b,
                              ffn_w_out)
    prompt = _trunk_prompt(x_prompt, layers)
    sample = _trunk_decode(x_sample, state_gdn, state_gdn_conv, cache_k, cache_v, cache_logf, state_ffn_conv,
                           page_table, layers)
    return (prompt[0], sample[0]) + prompt[1:] + sample[1:]
```

```python
import functools

import jax
import jax.numpy as jnp
from jax import lax
from jax.experimental import pallas as pl
from jax.experimental.pallas import tpu as pltpu

F32 = jnp.float32
BF16 = jnp.bfloat16

HEAD_DIM = 128
GDN_CHUNK = 64
EPS = 1e-6
V7X_VMEM_LIMIT_BYTES = 56 * 1024 * 1024
ROW_TILE = 512
ATTN_TILE = 512
CONV_COLS = 512
FFN_COLS = 256
GDN_HEADS_PER_STEP = 8
GDN_SEQ_TILE = 1024
GDN_SOLVE_BATCH = 16
GDN_DECODE_SEQS = 4
GDN_SOLVE_HEADS = 2


def _cparams(*semantics):
    return pltpu.CompilerParams(dimension_semantics=semantics, vmem_limit_bytes=V7X_VMEM_LIMIT_BYTES)


def _resident(shape):
    return pl.BlockSpec(shape, lambda *_: (0,) * len(shape), pipeline_mode=pl.Buffered(1))


def _layer_of(w):
    stacked, layer = w
    shape = stacked.shape[1:]
    return pl.BlockSpec((None,) + shape, lambda *_: (layer,) + (0,) * len(shape), pipeline_mode=pl.Buffered(1))


def _sigmoid(x):
    return 1.0 / (1.0 + jnp.exp(-x))


def _silu(x):
    return x * _sigmoid(x)


def _softplus(x):
    return jnp.maximum(x, 0.0) + jnp.log(1.0 + jnp.exp(-jnp.abs(x)))


def _rmsnorm_rows(x, w):
    return x * lax.rsqrt(jnp.mean(x * x, axis=-1, keepdims=True) + EPS) * w


def _dot(a, b):
    return jnp.dot(a, b, preferred_element_type=F32)


def _dot_nt(a, b):
    return lax.dot_general(a, b, (((1,), (1,)), ((), ())), preferred_element_type=F32)


def _dot_tn(a, b):
    return lax.dot_general(a, b, (((0,), (0,)), ((), ())), preferred_element_type=F32)


def _split3(x):
    hi = x.astype(BF16)
    r = x - hi.astype(F32)
    mid = r.astype(BF16)
    lo = (r - mid.astype(F32)).astype(BF16)
    return hi, mid, lo


def _dot_sel_left(sel, x):
    hi, mid, lo = _split3(x)
    return _dot(sel, hi) + _dot(sel, mid) + _dot(sel, lo)


def _dot_sel_right(x, sel):
    hi, mid, lo = _split3(x)
    return _dot(hi, sel) + _dot(mid, sel) + _dot(lo, sel)


def _iota2(shape, dim):
    return lax.broadcasted_iota(jnp.int32, shape, dim)


def _gdn_in_kernel(x_ref, nw_ref, w_ref, cw_ref, alog_ref, dtb_ref, buf_ref,
                   q_ref, k_ref, v_ref, z_ref, beta_ref, g_ref, cs_ref, ubuf, carry,
                   *, decode, tm, qk_width, conv_dim, n_vheads):
    z_cols = slice(conv_dim, conv_dim + n_vheads * HEAD_DIM)
    gate_cols = slice(conv_dim + n_vheads * HEAD_DIM, conv_dim + n_vheads * HEAD_DIM + 2 * n_vheads)
    hb = _rmsnorm_rows(x_ref[...], nw_ref[...]).astype(BF16)

    if not decode:
        @pl.when(pl.program_id(1) == 0)
        def _():
            carry[...] = jnp.zeros_like(carry)
            carry[5:8, :] = buf_ref[...]

    wc = CONV_COLS
    ahead = _dot(hb, w_ref[:, 0:wc])
    for j in range(conv_dim // wc):
        c0 = j * wc
        cols = slice(c0, c0 + wc)
        u = ahead
        if j + 1 < conv_dim // wc:
            ahead = _dot(hb, w_ref[:, c0 + wc:c0 + 2 * wc])
        else:
            ahead = _dot(hb, w_ref[:, z_cols])
        if decode:
            b0, b1, b2 = buf_ref[:, 0, cols], buf_ref[:, 1, cols], buf_ref[:, 2, cols]
            y = cw_ref[0:1, cols] * b0 + cw_ref[1:2, cols] * b1 + cw_ref[2:3, cols] * b2 + cw_ref[3:4, cols] * u
            cs_ref[:, 0, cols] = b1
            cs_ref[:, 1, cols] = b2
            cs_ref[:, 2, cols] = u
        else:
            ubuf[0:8, :] = carry[:, cols]
            ubuf[8:8 + tm, :] = u
            y = (cw_ref[3:4, cols] * u + cw_ref[2:3, cols] * ubuf[7:7 + tm, :]
                 + cw_ref[1:2, cols] * ubuf[6:6 + tm, :] + cw_ref[0:1, cols] * ubuf[5:5 + tm, :])
            carry[:, cols] = ubuf[tm:tm + 8, :]
        y = _silu(y)
        for hh in range(wc // HEAD_DIM):
            yh = y[:, hh * HEAD_DIM:(hh + 1) * HEAD_DIM]
            col = c0 + hh * HEAD_DIM
            if col < 2 * qk_width:
                yh = yh * lax.rsqrt(jnp.sum(yh * yh, axis=-1, keepdims=True) + EPS)
            dst, col = ((q_ref, col) if col < qk_width else
                        (k_ref, col - qk_width) if col < 2 * qk_width else (v_ref, col - 2 * qk_width))
            if decode:
                dst[:, col:col + HEAD_DIM] = yh
            else:
                dst[col // HEAD_DIM] = yh
    if not decode:
        cs_ref[...] = carry[5:8, :]

    z = ahead
    if decode:
        z_ref[...] = z
    else:
        for h in range(n_vheads):
            z_ref[h] = z[:, h * HEAD_DIM:(h + 1) * HEAD_DIM]
    ba = _dot(hb, w_ref[:, gate_cols])
    beta_ref[...] = _sigmoid(ba[:, :n_vheads])
    g = -jnp.exp(alog_ref[...]) * _softplus(ba[:, n_vheads:] + dtb_ref[...])
    if decode:
        g_ref[...] = g
    else:
        r, c = _iota2((tm, tm), 0), _iota2((tm, tm), 1)
        same_chunk_lower = jnp.where((r // GDN_CHUNK == c // GDN_CHUNK) & (c <= r), 1.0, 0.0).astype(BF16)
        g_ref[...] = _dot_sel_left(same_chunk_lower, g)


def _gdn_in(x, buf, nw, w_in, cw, alog, dtb, *, decode):
    bsz, seq, dm = x.shape
    conv_dim = cw.shape[1]
    vw = (w_in[0].shape[-1] - conv_dim) * HEAD_DIM // (HEAD_DIM + 2)
    qk_width = (conv_dim - vw) // 2
    n_vheads = vw // HEAD_DIM
    tm = min(ROW_TILE, seq)
    grid = (bsz, seq // tm)
    row = lambda width: pl.BlockSpec((None, tm, width), lambda b, i: (b, i, 0))
    if decode:
        buf, layer = buf
        buf_spec = pl.BlockSpec((None, tm, 3, conv_dim), lambda b, i: (layer, i, 0, 0))
        cs_shape, cs_spec = (seq, 3, conv_dim), pl.BlockSpec((tm, 3, conv_dim), lambda b, i: (i, 0, 0))
    else:
        buf_spec = pl.BlockSpec((None, 3, conv_dim), lambda b, i: (b, 0, 0))
        cs_shape, cs_spec = (bsz, 3, conv_dim), pl.BlockSpec((None, 3, conv_dim), lambda b, i: (b, 0, 0))
    kern = functools.partial(_gdn_in_kernel, decode=decode, tm=tm, qk_width=qk_width, conv_dim=conv_dim,
                             n_vheads=n_vheads)
    if decode:
        wide = lambda width: (row(width), jax.ShapeDtypeStruct((bsz, seq, width), F32))
    else:
        wide = lambda width: (pl.BlockSpec((None, width // HEAD_DIM, tm, HEAD_DIM), lambda b, i: (b, 0, i, 0)),
                              jax.ShapeDtypeStruct((bsz, width // HEAD_DIM, seq, HEAD_DIM), F32))
    (q_spec, q_shape), (v_spec, v_shape) = wide(qk_width), wide(vw)
    gate_shape = jax.ShapeDtypeStruct((bsz, seq, n_vheads), F32)
    return pl.pallas_call(
        kern,
        grid=grid,
        in_specs=[row(dm), _resident((1, dm)), _layer_of(w_in), _resident(cw.shape), _resident((1, n_vheads)),
                  _resident((1, n_vheads)), buf_spec],
        out_specs=[q_spec, q_spec, v_spec, v_spec, row(n_vheads), row(n_vheads), cs_spec],
        out_shape=[q_shape, q_shape, v_shape, v_shape, gate_shape, gate_shape, jax.ShapeDtypeStruct(cs_shape, F32)],
        scratch_shapes=[pltpu.VMEM((tm + 8, CONV_COLS), F32), pltpu.VMEM((8, conv_dim), F32)],
        compiler_params=_cparams("arbitrary", "arbitrary"),
        name="gdn_in_decode" if decode else "gdn_in_prompt",
    )(x, nw, w_in[0], cw, alog, dtb, buf)


def _bdot(a, b):
    return lax.dot_general(a, b, (((2,), (1,)), ((0,), (0,))), preferred_element_type=F32)


def _bdot_nt(a, b):
    return lax.dot_general(a, b, (((2,), (2,)), ((0,), (0,))), preferred_element_type=F32)


def _unit_lower_inverse(n, row, col):
    eye = jnp.where(row == col, 1.0, 0.0)
    inv = eye - jnp.where((row >> 1) == (col >> 1), n, 0.0)
    size = n.shape[-1]
    bits = 2
    while (1 << bits) <= size:
        off_diag = jnp.where((row >> bits) == (col >> bits),
                             jnp.where((row >> (bits - 1)) == (col >> (bits - 1)), 0.0, n), 0.0)
        inv_b = inv.astype(BF16)
        inv = inv - _bdot(inv_b, _bdot(off_diag.astype(BF16), inv_b).astype(BF16))
        bits += 1
    n_hi = n.astype(BF16).astype(F32)
    x_hi = inv.astype(BF16).astype(F32)
    lhs = jnp.concatenate([n_hi, n - n_hi, n_hi], axis=-1).astype(BF16)
    rhs = jnp.concatenate([x_hi, x_hi, inv - x_hi], axis=1).astype(BF16)
    resid = eye - inv - _bdot(lhs, rhs)
    return inv + _bdot(x_hi.astype(BF16), resid.astype(BF16))


def _gdn_core_kernel(q_ref, k_ref, v_ref, z_ref, beta_ref, gcum_ref, nw_ref, o_ref, s_out_ref,
                     s_scr, u_scr, w_scr, att_scr, qg_scr, kd_scr, egl_scr, gb_scr, *, lt, hb, rep, nb, hs):
    c = GDN_CHUNK
    groups = lt // c // nb
    t = pl.program_id(2)
    head0 = pl.program_id(1) * hb
    q_scale = HEAD_DIM ** -0.5

    @pl.when(t == 0)
    def _():
        s_scr[...] = jnp.zeros_like(s_scr)

    nbh = nb * hs
    row = lax.broadcasted_iota(jnp.int32, (nbh, c, c), 1)
    col = lax.broadcasted_iota(jnp.int32, (nbh, c, c), 2)
    lower = row >= col
    strict = row > col
    lane = _iota2((nb * c, HEAD_DIM), 1)
    gate_lane = _iota2((nb * c, beta_ref.shape[-1]), 1)
    pick = jnp.where(_iota2((nbh * c, HEAD_DIM), 1) < 3, 1.0, 0.0).astype(BF16).reshape(nbh, c, HEAD_DIM)

    def solve(idx, _):
        hset = idx // groups
        chunk0 = pl.multiple_of((idx - hset * groups) * nb, nb)
        rows = pl.ds(pl.multiple_of(chunk0 * c, nb * c), nb * c)
        heads = [hset * hs + m for m in range(hs)]
        per_head = []
        for hh in heads:
            own = gate_lane == head0 + hh
            bcol = jnp.sum(jnp.where(own, beta_ref[rows, :], 0.0), axis=-1, keepdims=True)
            gcol = jnp.sum(jnp.where(own, gcum_ref[rows, :], 0.0), axis=-1, keepdims=True)
            g_hi = gcol.astype(BF16).astype(F32)
            g_mid = (gcol - g_hi).astype(BF16).astype(F32)
            g_lo = gcol - g_hi - g_mid
            pieces = jnp.where(lane == 0, g_hi, jnp.where(lane == 1, g_mid, jnp.where(lane == 2, g_lo, 0.0)))
            per_head.append((gcol, bcol, pieces.astype(BF16), q_ref[hh // rep, rows, :] * q_scale,
                             k_ref[hh // rep, rows, :], v_ref[hh, rows, :]))
        stack = lambda i, width: jnp.concatenate([p[i].reshape(nb, c, width) for p in per_head], axis=0)
        g3, b3 = stack(0, 1), stack(1, 1)
        gcum_t = _bdot_nt(pick, stack(2, HEAD_DIM))
        decay = jnp.where(lower, jnp.exp(jnp.minimum(g3 - gcum_t, 0.0)), 0.0)
        q3, k3, v3 = stack(3, HEAD_DIM), stack(4, HEAD_DIM), stack(5, HEAD_DIM)
        kb = k3 * b3
        k_bf = k3.astype(BF16)
        n = jnp.where(strict, _bdot_nt(kb.astype(BF16), k_bf) * decay, 0.0)
        inv = _unit_lower_inverse(n, row, col)
        eg = jnp.exp(g3)
        rhs = jnp.concatenate([v3 * b3, kb * eg], axis=-1).astype(BF16)
        sol = _bdot(inv.astype(BF16), rhs)
        attn = jnp.where(lower, _bdot_nt(q3.astype(BF16), k_bf) * decay, 0.0)
        g_last = g3[:, c - 1:c, :]
        qg = q3 * eg
        kd = k3 * jnp.exp(g_last - g3)
        for m, hh in enumerate(heads):
            mine = lambda a: a[m * nb:(m + 1) * nb].reshape(nb * c, a.shape[-1])
            u_scr[hh, rows, :] = mine(sol[:, :, :HEAD_DIM])
            w_scr[hh, rows, :] = mine(sol[:, :, HEAD_DIM:]).astype(BF16)
            att_scr[hh, rows, :] = mine(attn).astype(BF16)
            qg_scr[hh, rows, :] = mine(qg).astype(BF16)
            kd_scr[hh, rows, :] = mine(kd).astype(BF16)
            gb_scr[m] = jnp.broadcast_to(per_head[m][0], gb_scr.shape[1:])
            egl_scr[hh, pl.ds(chunk0, nb), :] = jnp.exp(gb_scr[m, pl.ds(c - 1, nb, stride=c), :])
        return 0

    lax.fori_loop(0, hb // hs * groups, solve, 0)

    def step(i, _):
        rows = pl.ds(pl.multiple_of(i * c, c), c)
        heads = range(hb)
        state = [s_scr[hh] for hh in heads]
        res = [_dot(jnp.concatenate([w_scr[hh, rows, :], qg_scr[hh, rows, :]], axis=0), state[hh].astype(BF16))
               for hh in heads]
        v_new = [(u_scr[hh, rows, :] - res[hh][:c]).astype(BF16) for hh in heads]
        o = [res[hh][c:] + _dot(att_scr[hh, rows, :], v_new[hh]) for hh in heads]
        grown = [_dot_tn(kd_scr[hh, rows, :], v_new[hh]) for hh in heads]
        for hh in heads:
            s_scr[hh] = state[hh] * egl_scr[hh, pl.ds(i, 1), :] + grown[hh]
            gated = _rmsnorm_rows(o[hh], nw_ref[...]) * _silu(z_ref[hh, rows, :])
            o_ref[rows, hh * HEAD_DIM:(hh + 1) * HEAD_DIM] = gated.astype(o_ref.dtype)
        return 0

    lax.fori_loop(0, lt // c, step, 0)

    @pl.when(t == pl.num_programs(2) - 1)
    def _():
        s_out_ref[...] = s_scr[...]


def _gdn_core_prompt(q, k, v, z, beta, gcum, nw):
    bsz, n_vheads, seq, _ = v.shape
    rep = n_vheads // q.shape[1]
    hb = min(GDN_HEADS_PER_STEP, n_vheads)
    lt = min(GDN_SEQ_TILE, seq)
    nb = min(GDN_SOLVE_BATCH, lt // GDN_CHUNK)
    hs = min(GDN_SOLVE_HEADS, hb)
    heads = lambda n: pl.BlockSpec((None, n, lt, HEAD_DIM), lambda b, h, t: (b, h, t, 0))
    gates = pl.BlockSpec((None, lt, n_vheads), lambda b, h, t: (b, t, 0))
    per_head = lambda width, dtype: pltpu.VMEM((hb, lt, width), dtype)
    return pl.pallas_call(
        functools.partial(_gdn_core_kernel, lt=lt, hb=hb, rep=rep, nb=nb, hs=hs),
        grid=(bsz, n_vheads // hb, seq // lt),
        in_specs=[heads(hb // rep), heads(hb // rep), heads(hb), heads(hb), gates, gates, _resident((1, HEAD_DIM))],
        out_specs=[pl.BlockSpec((None, lt, hb * HEAD_DIM), lambda b, h, t: (b, t, h)),
                   pl.BlockSpec((None, hb, HEAD_DIM, HEAD_DIM), lambda b, h, t: (b, h, 0, 0))],
        out_shape=[jax.ShapeDtypeStruct((bsz, seq, n_vheads * HEAD_DIM), BF16),
                   jax.ShapeDtypeStruct((bsz, n_vheads, HEAD_DIM, HEAD_DIM), F32)],
        scratch_shapes=[pltpu.VMEM((hb, HEAD_DIM, HEAD_DIM), F32), per_head(HEAD_DIM, F32), per_head(HEAD_DIM, BF16),
                        per_head(GDN_CHUNK, BF16), per_head(HEAD_DIM, BF16), per_head(HEAD_DIM, BF16),
                        pltpu.VMEM((hb, lt // GDN_CHUNK, HEAD_DIM), F32),
                        pltpu.VMEM((hs, nb * GDN_CHUNK, HEAD_DIM), F32)],
        compiler_params=_cparams("arbitrary", "arbitrary", "arbitrary"),
        name="gdn_core_prompt",
    )(q, k, v, z, beta, gcum, nw)


def _gdn_core_decode_kernel(q_ref, k_ref, v_ref, z_ref, beta_ref, g_ref, nw_ref, s_ref, o_ref, so_ref,
                            *, n_vheads, rep):
    q_scale = HEAD_DIM ** -0.5
    row = _iota2((8, HEAD_DIM), 0)

    def rows3(a, b, c):
        return jnp.where(row == 0, a, jnp.where(row == 1, b, jnp.where(row == 2, c, 0.0)))

    pairs = [(s, h) for s in range(q_ref.shape[0]) for h in range(n_vheads)]
    idx = range(len(pairs))
    lanes = [slice(h * HEAD_DIM, (h + 1) * HEAD_DIM) for _, h in pairs]
    klanes = [slice(h // rep * HEAD_DIM, (h // rep + 1) * HEAD_DIM) for _, h in pairs]
    qh = [q_ref[s, :, klanes[i]] * q_scale for i, (s, _) in enumerate(pairs)]
    kh = [k_ref[s, :, klanes[i]] for i, (s, _) in enumerate(pairs)]
    beta = [beta_ref[s, :, h:h + 1] for s, h in pairs]
    eg = [jnp.exp(g_ref[s, :, h:h + 1]) for s, h in pairs]
    res = [_dot(rows3(kh[i] * beta[i] * eg[i], qh[i] * eg[i], 0.0).astype(BF16), s_ref[pairs[i]].astype(BF16))
           for i in idx]
    v_new = [v_ref[pairs[i][0], :, lanes[i]] * beta[i] - res[i][0:1, :] for i in idx]
    k_hi = [kh[i].astype(BF16).astype(F32) for i in idx]
    v_hi = [v_new[i].astype(BF16).astype(F32) for i in idx]
    grown = [_dot_tn(rows3(k_hi[i], kh[i] - k_hi[i], k_hi[i]).astype(BF16),
                     rows3(v_hi[i], v_hi[i], v_new[i] - v_hi[i]).astype(BF16)) for i in idx]
    for i in idx:
        s = pairs[i][0]
        so_ref[pairs[i]] = s_ref[pairs[i]] * eg[i] + grown[i]
        o = res[i][1:2, :] + jnp.sum(qh[i] * kh[i], axis=-1, keepdims=True) * v_new[i]
        gated = _rmsnorm_rows(o, nw_ref[...]) * _silu(z_ref[s, :, lanes[i]])
        o_ref[s, :, lanes[i]] = gated.astype(o_ref.dtype)


def _gdn_core_decode(q, k, v, z, beta, g, nw, state):
    nseq, _, vw = v.shape
    n_vheads = vw // HEAD_DIM
    rep = vw // q.shape[-1]
    sb = GDN_DECODE_SEQS if nseq % GDN_DECODE_SEQS == 0 else 1
    row = lambda width: pl.BlockSpec((sb, 1, width), lambda s: (s, 0, 0))
    st = pl.BlockSpec((sb, n_vheads, HEAD_DIM, HEAD_DIM), lambda s: (s, 0, 0, 0))
    return pl.pallas_call(
        functools.partial(_gdn_core_decode_kernel, n_vheads=n_vheads, rep=rep),
        grid=(nseq // sb,),
        in_specs=[row(q.shape[-1]), row(q.shape[-1]), row(vw), row(vw), row(n_vheads), row(n_vheads),
                  _resident((1, HEAD_DIM)), st],
        out_specs=[row(vw), st],
        out_shape=[jax.ShapeDtypeStruct((nseq, 1, vw), BF16), jax.ShapeDtypeStruct(state.shape, F32)],
        compiler_params=_cparams("arbitrary"),
        name="gdn_core_decode",
    )(q, k, v, z, beta, g, nw, state)


def _out_ffn_kernel(x_ref, o_ref, wo_ref, nw_ref, w1_ref, cw_ref, cb_ref, w2_ref, buf_ref,
                    y_ref, cs_ref, gbuf, ubuf, carry, act, *, decode, tm, d_ff):
    x1 = x_ref[...] + _dot(o_ref[...], wo_ref[...])
    hb = _rmsnorm_rows(x1, nw_ref[...]).astype(BF16)

    if not decode:
        @pl.when(pl.program_id(1) == 0)
        def _():
            carry[...] = jnp.zeros_like(carry)
            carry[6:8, :] = buf_ref[...]

    def conv(u, scratch, c0, wc):
        cols = slice(c0, c0 + wc)
        if decode:
            b0 = buf_ref[:, 0, cols]
            b1 = buf_ref[:, 1, cols]
            cs_ref[:, 0, cols] = b1
            cs_ref[:, 1, cols] = u
            y = cw_ref[0:1, cols] * b0 + cw_ref[1:2, cols] * b1 + cw_ref[2:3, cols] * u
        else:
            scratch[0:8, :] = carry[:, cols]
            scratch[8:8 + tm, :] = u
            y = (cw_ref[2:3, cols] * u + cw_ref[1:2, cols] * scratch[7:7 + tm, :]
                 + cw_ref[0:1, cols] * scratch[6:6 + tm, :])
            carry[:, cols] = scratch[tm:tm + 8, :]
        return y + cb_ref[:, cols]

    wc = FFN_COLS

    def project(j):
        c0 = j * wc
        return _dot(hb, w1_ref[:, c0:c0 + wc]), _dot(hb, w1_ref[:, d_ff + c0:d_ff + c0 + wc])

    ahead = project(0)
    for j in range(d_ff // wc):
        c0 = j * wc
        pre_gate, pre_up = ahead
        if j + 1 < d_ff // wc:
            ahead = project(j + 1)
        gate = conv(pre_gate, gbuf, c0, wc)
        up = conv(pre_up, ubuf, d_ff + c0, wc)
        act[:, c0:c0 + wc] = (_silu(gate) * up).astype(BF16)
    if not decode:
        cs_ref[...] = carry[6:8, :]
    y_ref[...] = x1 + _dot(act[...], w2_ref[...])


def _out_ffn(x, o, wo, nw, w1, cw, cb, w2, buf, *, decode):
    bsz, seq, dm = x.shape
    d_ff = w2[0].shape[1]
    tm = min(ROW_TILE, seq)
    row = lambda width: pl.BlockSpec((None, tm, width), lambda b, i: (b, i, 0))
    if decode:
        buf, layer = buf
        buf_spec = pl.BlockSpec((None, tm, 2, 2 * d_ff), lambda b, i: (layer, i, 0, 0))
        cs_shape, cs_spec = (seq, 2, 2 * d_ff), pl.BlockSpec((tm, 2, 2 * d_ff), lambda b, i: (i, 0, 0))
    else:
        buf_spec = pl.BlockSpec((None, 2, 2 * d_ff), lambda b, i: (b, 0, 0))
        cs_shape, cs_spec = (bsz, 2, 2 * d_ff), pl.BlockSpec((None, 2, 2 * d_ff), lambda b, i: (b, 0, 0))
    return pl.pallas_call(
        functools.partial(_out_ffn_kernel, decode=decode, tm=tm, d_ff=d_ff),
        grid=(bsz, seq // tm),
        in_specs=[row(dm), row(o.shape[-1]), _layer_of(wo), _resident((1, dm)), _layer_of(w1),
                  _resident(cw.shape), _resident((1, 2 * d_ff)), _layer_of(w2), buf_spec],
        out_specs=[row(dm), cs_spec],
        out_shape=[jax.ShapeDtypeStruct((bsz, seq, dm), F32), jax.ShapeDtypeStruct(cs_shape, F32)],
        scratch_shapes=[pltpu.VMEM((tm + 8, FFN_COLS), F32), pltpu.VMEM((tm + 8, FFN_COLS), F32),
                        pltpu.VMEM((8, 2 * d_ff), F32), pltpu.VMEM((tm, d_ff), BF16)],
        compiler_params=_cparams("arbitrary", "arbitrary"),
        name="out_ffn_decode" if decode else "out_ffn_prompt",
    )(x, o, wo[0], nw, w1[0], cw, cb, w2[0], buf)


def _fox_in_kernel(x_ref, nw_ref, w_ref, bf_ref, bft_ref, qn_ref, kn_ref,
                   q_ref, k_ref, v_ref, og_ref, lf_ref, cum_ref, carry, *, tm, n_heads):
    width = n_heads * HEAD_DIM
    hb = _rmsnorm_rows(x_ref[...], nw_ref[...]).astype(BF16)
    q = _dot(hb, w_ref[:, 0:width])
    k = _dot(hb, w_ref[:, width:2 * width])
    for h in range(n_heads):
        lanes = slice(h * HEAD_DIM, (h + 1) * HEAD_DIM)
        q_ref[:, lanes] = _rmsnorm_rows(q[:, lanes], qn_ref[...])
    v_ref[...] = _dot(hb, w_ref[:, 2 * width:3 * width])
    for h in range(n_heads):
        lanes = slice(h * HEAD_DIM, (h + 1) * HEAD_DIM)
        k_ref[:, lanes] = _rmsnorm_rows(k[:, lanes], kn_ref[...])
    og_ref[...] = _dot(hb, w_ref[:, 3 * width:4 * width])
    lf_ref[...] = -_softplus(-(_dot(hb, w_ref[:, 4 * width:4 * width + n_heads]) + bf_ref[...]))

    @pl.when(pl.program_id(1) == 0)
    def _():
        carry[...] = jnp.zeros_like(carry)

    logits_t = lax.dot_general(w_ref[:, 4 * width:4 * width + n_heads], hb, (((0,), (1,)), ((), ())),
                               preferred_element_type=F32)
    lf_t = -_softplus(-(logits_t + bft_ref[...]))
    upper = jnp.where(_iota2((tm, tm), 0) <= _iota2((tm, tm), 1), 1.0, 0.0).astype(BF16)
    cum = _dot_sel_right(lf_t, upper) + carry[:, 0:1]
    cum_ref[...] = cum
    carry[...] = jnp.broadcast_to(cum[:, tm - 1:tm], carry.shape)


def _fox_in(x, nw, w_in, bf, bft, qn, kn):
    bsz, seq, dm = x.shape
    n_heads = bf.shape[1]
    width = n_heads * HEAD_DIM
    tm = min(ROW_TILE, seq)
    row = lambda w: pl.BlockSpec((None, tm, w), lambda b, i: (b, i, 0))
    wide = jax.ShapeDtypeStruct((bsz, seq, width), F32)
    return pl.pallas_call(
        functools.partial(_fox_in_kernel, tm=tm, n_heads=n_heads),
        grid=(bsz, seq // tm),
        in_specs=[row(dm), _resident((1, dm)), _layer_of(w_in), _resident((1, n_heads)),
                  _resident((n_heads, 1)), _resident((1, HEAD_DIM)), _resident((1, HEAD_DIM))],
        out_specs=[row(width), row(width), row(width), row(width), row(n_heads),
                   pl.BlockSpec((None, n_heads, tm), lambda b, i: (b, 0, i))],
        out_shape=[wide, wide, wide, wide, jax.ShapeDtypeStruct((bsz, seq, n_heads), F32),
                   jax.ShapeDtypeStruct((bsz, n_heads, seq), F32)],
        scratch_shapes=[pltpu.VMEM((n_heads, HEAD_DIM), F32)],
        compiler_params=_cparams("arbitrary", "arbitrary"),
        name="fox_in",
    )(x, nw, w_in[0], bf, bft, qn, kn)


def _fox_attn_kernel(q_ref, k_ref, v_ref, cum_ref, og_ref, o_ref, kb_scr, vb_scr, *, tile):
    h = pl.program_id(1)
    qi = pl.program_id(2)
    scale = HEAD_DIM ** -0.5
    exp2_scale = scale * 1.4426950408889634

    @pl.when(qi == 0)
    def _():
        kb_scr[...] = k_ref[...].astype(BF16)
        vb_scr[...] = v_ref[...].astype(BF16)

    qb = q_ref[...].astype(BF16)

    def lane_tiles(x):
        return [x[:, t * HEAD_DIM:(t + 1) * HEAD_DIM] for t in range(tile // HEAD_DIM)]

    def key_block(j, carry, diagonal):
        m_prev, l_prev, acc = carry
        cols = pl.ds(pl.multiple_of(j * tile, tile), tile)
        s = _dot_nt(qb, kb_scr[cols, :]) - cum_ref[pl.ds(h, 1), cols] * (1.0 / scale)
        if diagonal:
            s = jnp.where(_iota2((tile, tile), 1) <= _iota2((tile, tile), 0), s, -jnp.inf)
        m_new = jnp.maximum(m_prev, jnp.max(functools.reduce(jnp.maximum, lane_tiles(s)), axis=-1, keepdims=True))
        alpha = jnp.exp2((m_prev - m_new) * exp2_scale)
        p = jnp.exp2((s - m_new) * exp2_scale)
        l_new = alpha * l_prev + jnp.sum(functools.reduce(jnp.add, lane_tiles(p)), axis=-1, keepdims=True)
        acc = alpha * acc + _dot(p.astype(BF16), vb_scr[cols, :])
        return m_new, l_new, acc

    init = (jnp.full((tile, 1), -jnp.inf, F32), jnp.zeros((tile, 1), F32), jnp.zeros((tile, HEAD_DIM), F32))
    carry = lax.fori_loop(0, qi, lambda j, c: key_block(j, c, False), init)
    _, l_fin, acc = key_block(qi, carry, True)
    o_ref[...] = (acc / l_fin * _sigmoid(og_ref[...])).astype(o_ref.dtype)


def _fox_attn_prompt(q, k, v, cum, og):
    bsz, seq, width = q.shape
    n_heads = width // HEAD_DIM
    tile = min(ATTN_TILE, seq)
    q_spec = pl.BlockSpec((None, tile, HEAD_DIM), lambda b, h, i: (b, i, h))
    kv_spec = pl.BlockSpec((None, seq, HEAD_DIM), lambda b, h, i: (b, 0, h))
    cum_spec = pl.BlockSpec((None, n_heads, seq), lambda b, h, i: (b, 0, 0))
    return pl.pallas_call(
        functools.partial(_fox_attn_kernel, tile=tile),
        grid=(bsz, n_heads, seq // tile),
        in_specs=[q_spec, kv_spec, kv_spec, cum_spec, q_spec],
        out_specs=q_spec,
        out_shape=jax.ShapeDtypeStruct((bsz, seq, width), BF16),
        scratch_shapes=[pltpu.VMEM((seq, HEAD_DIM), BF16), pltpu.VMEM((seq, HEAD_DIM), BF16)],
        compiler_params=_cparams("arbitrary", "arbitrary", "arbitrary"),
        name="fox_attn_prompt",
    )(q, k, v, cum, og)


def _fox_attn_decode_kernel(pt_ref, q_ref, kn_ref, vn_ref, lfn_ref, og_ref, *rest, n_heads, page, n_pages):
    del pt_ref
    k_pages, v_pages, lf_pages = rest[:n_pages], rest[n_pages:2 * n_pages], rest[2 * n_pages:3 * n_pages]
    o_ref, lf_scr = rest[3 * n_pages:]
    flat = n_heads * page
    scale = HEAD_DIM ** -0.5

    for p in range(n_pages):
        lf_scr[p:p + 1, :] = lf_pages[p][...]
    lf = lf_scr[...]
    lane = _iota2((n_pages, flat), 1)
    cum, page_total = lf, lf
    shift = n_heads
    while shift < flat:
        cum = cum + jnp.where(lane >= shift, pltpu.roll(cum, shift, axis=1), 0.0)
        page_total = page_total + pltpu.roll(page_total, shift, axis=1)
        shift *= 2
    earlier = jnp.where(_iota2((n_pages, n_pages), 1) < _iota2((n_pages, n_pages), 0), 1.0, 0.0).astype(BF16)
    before = _dot_sel_left(earlier, page_total)
    cum = cum + before
    past_total = before[n_pages - 1:n_pages, :] + page_total[n_pages - 1:n_pages, :]

    own_head = (_iota2((n_heads, flat), 1) & (n_heads - 1)) == _iota2((n_heads, flat), 0)
    eye = _iota2((n_heads, n_heads), 0) == _iota2((n_heads, n_heads), 1)
    to_rows = lambda r: jnp.sum(jnp.where(eye, jnp.broadcast_to(r, (n_heads, n_heads)), 0.0), axis=-1, keepdims=True)

    q = q_ref[...]
    qb = q.astype(BF16)
    scores = []
    for p in range(n_pages):
        s = _dot_nt(qb, k_pages[p][...].astype(BF16)) * scale - cum[p:p + 1, :]
        scores.append(jnp.where(own_head, s, -jnp.inf))
    s_new = (jnp.sum(q * kn_ref[...], axis=-1, keepdims=True) * scale
             - (to_rows(past_total[:, :n_heads]) + to_rows(lfn_ref[...])))
    m = s_new
    for s in scores:
        m = jnp.maximum(m, jnp.max(s, axis=-1, keepdims=True))
    denom = jnp.exp(s_new - m)
    acc = denom * vn_ref[...]
    for p in range(n_pages):
        w = jnp.exp(scores[p] - m)
        denom = denom + jnp.sum(w, axis=-1, keepdims=True)
        acc = acc + _dot(w.astype(BF16), v_pages[p][...].astype(BF16))
    o_ref[...] = (acc / denom * _sigmoid(og_ref[...])).astype(o_ref.dtype)


def _fox_attn_decode(page_table, q, k_new, v_new, lf_new, og, k_pool, v_pool, lf_pool):
    nseq, n_heads, _ = q.shape
    n_pages = page_table.shape[1]
    flat = k_pool.shape[1]
    page = flat // n_heads
    tok = pl.BlockSpec((None, n_heads, HEAD_DIM), lambda s, pt: (s, 0, 0))
    pool = [pl.BlockSpec((None, flat, HEAD_DIM), lambda s, pt, p=p: (pt[s, p], 0, 0)) for p in range(n_pages)]
    lf_specs = [pl.BlockSpec((None, 1, flat), lambda s, pt, p=p: (pt[s, p], 0, 0)) for p in range(n_pages)]
    grid_spec = pltpu.PrefetchScalarGridSpec(
        num_scalar_prefetch=1,
        grid=(nseq,),
        in_specs=[tok, tok, tok, pl.BlockSpec((None, 1, n_heads), lambda s, pt: (s, 0, 0)), tok] + pool + pool + lf_specs,
        out_specs=tok,
        scratch_shapes=[pltpu.VMEM((n_pages, flat), F32)],
    )
    return pl.pallas_call(
        functools.partial(_fox_attn_decode_kernel, n_heads=n_heads, page=page, n_pages=n_pages),
        grid_spec=grid_spec,
        out_shape=jax.ShapeDtypeStruct((nseq, n_heads, HEAD_DIM), BF16),
        compiler_params=_cparams("arbitrary"),
        name="fox_attn_decode",
    )(page_table, q, k_new, v_new, lf_new, og, *([k_pool] * n_pages), *([v_pool] * n_pages), *([lf_pool] * n_pages))


def _prepare_weights(norm_mix, norm_ffn, gdn_w_in, gdn_conv_w, gdn_a_log, gdn_dt_bias, gdn_norm, gdn_w_out,
                     fox_w_in, fox_b_f, fox_q_norm, fox_k_norm, fox_w_out, ffn_w_in, ffn_conv_w, ffn_conv_b,
                     ffn_w_out):
    fw = fox_w_out.shape[1]
    gdn_in_b, gdn_out_b = gdn_w_in.astype(BF16), gdn_w_out.astype(BF16)
    fox_in_b, fox_out_b = fox_w_in.astype(BF16), fox_w_out.astype(BF16)
    ffn_in_b, ffn_out_b = ffn_w_in.astype(BF16), ffn_w_out.astype(BF16)
    layers = []
    for i in range(norm_mix.shape[0]):
        j = i // 2
        ffn = dict(nw=norm_ffn[i][None], w1=(ffn_in_b, i), cw=ffn_conv_w[i], cb=ffn_conv_b[i][None], w2=(ffn_out_b, i))
        if i % 2 == 0:
            mixer = dict(nw=norm_mix[i][None], w_in=(gdn_in_b, j), cw=gdn_conv_w[j], alog=gdn_a_log[j][None],
                         dtb=gdn_dt_bias[j][None], onw=gdn_norm[j][None], wo=(gdn_out_b, j))
        else:
            mixer = dict(nw=norm_mix[i][None], w_in=(fox_in_b, j),
                         bf=fox_b_f[j][None], bft=fox_b_f[j][:, None], qn=fox_q_norm[j][None],
                         kn=fox_k_norm[j][None], wo=(fox_out_b, j))
        layers.append((mixer, ffn))
    return layers


def _trunk_prompt(x, layers):
    bsz, seq, _ = x.shape
    gdn_s, gdn_cb, ks, vs, lfs, ffn_cb = [], [], [], [], [], []
    for i, (m, f) in enumerate(layers):
        if i % 2 == 0:
            conv_dim = m["cw"].shape[1]
            q, k, v, z, beta, g, cb = _gdn_in(x, jnp.zeros((bsz, 3, conv_dim), F32), m["nw"], m["w_in"], m["cw"],
                                              m["alog"], m["dtb"], decode=False)
            o, s = _gdn_core_prompt(q, k, v, z, beta, g, m["onw"])
            gdn_cb.append(cb)
            gdn_s.append(s)
        else:
            q, k, v, og, lf, cum = _fox_in(x, m["nw"], m["w_in"], m["bf"], m["bft"], m["qn"], m["kn"])
            o = _fox_attn_prompt(q, k, v, cum, og)
            n_heads = lf.shape[-1]
            ks.append(k.reshape(bsz, seq, n_heads, HEAD_DIM))
            vs.append(v.reshape(bsz, seq, n_heads, HEAD_DIM))
            lfs.append(lf)
        x, fcb = _out_ffn(x, o, m["wo"], f["nw"], f["w1"], f["cw"], f["cb"], f["w2"],
                          jnp.zeros((bsz, 2, f["cw"].shape[1]), F32), decode=False)
        ffn_cb.append(fcb)
    return x, jnp.stack(gdn_s), jnp.stack(gdn_cb), jnp.stack(ks), jnp.stack(vs), jnp.stack(lfs), jnp.stack(ffn_cb)


def _trunk_decode(x, state_gdn, state_gdn_conv, cache_k, cache_v, cache_logf, state_ffn_conv, page_table, layers):
    nseq = x.shape[0]
    x = x.reshape(1, nseq, x.shape[-1])
    gdn_s, gdn_cb, ks, vs, lfs, ffn_cb = [], [], [], [], [], []
    for i, (m, f) in enumerate(layers):
        j = i // 2
        if i % 2 == 0:
            q, k, v, z, beta, g, cb = _gdn_in(x, (state_gdn_conv, j), m["nw"], m["w_in"], m["cw"], m["alog"],
                                              m["dtb"], decode=True)
            per_seq = lambda a: a.reshape(nseq, 1, a.shape[-1])
            o, s = _gdn_core_decode(per_seq(q), per_seq(k), per_seq(v), per_seq(z), per_seq(beta), per_seq(g),
                                    m["onw"], state_gdn[j])
            o = o.reshape(1, nseq, -1)
            gdn_cb.append(cb)
            gdn_s.append(s)
        else:
            q, k, v, og, lf, _ = _fox_in(x, m["nw"], m["w_in"], m["bf"], m["bft"], m["qn"], m["kn"])
            n_heads = lf.shape[-1]
            n_pool, page = cache_k.shape[1], cache_k.shape[2]
            heads = lambda a: a.reshape(nseq, n_heads, HEAD_DIM)
            o = _fox_attn_decode(page_table, heads(q), heads(k), heads(v), lf.reshape(nseq, 1, n_heads), heads(og),
                                 cache_k[j].reshape(n_pool, page * n_heads, HEAD_DIM),
                                 cache_v[j].reshape(n_pool, page * n_heads, HEAD_DIM),
                                 cache_logf[j].reshape(n_pool, 1, page * n_heads))
            o = o.reshape(1, nseq, -1)
            ks.append(k.reshape(nseq, 1, n_heads, HEAD_DIM))
            vs.append(v.reshape(nseq, 1, n_heads, HEAD_DIM))
            lfs.append(lf.reshape(nseq, 1, n_heads))
        x, fcb = _out_ffn(x, o, m["wo"], f["nw"], f["w1"], f["cw"], f["cb"], f["w2"],
                          (state_ffn_conv, i), decode=True)
        ffn_cb.append(fcb)
    return (x.reshape(nseq, 1, -1), jnp.stack(gdn_s), jnp.stack(gdn_cb), jnp.stack(ks), jnp.stack(vs),
            jnp.stack(lfs), jnp.stack(ffn_cb))


def kernel(x_prompt, x_sample, state_gdn, state_gdn_conv, cache_k, cache_v, cache_logf, state_ffn_conv, page_table,
           norm_mix, norm_ffn, gdn_w_in, gdn_conv_w, gdn_a_log, gdn_dt_bias, gdn_norm, gdn_w_out, fox_w_in, fox_b_f,
           fox_q_norm, fox_k_norm, fox_w_out, ffn_w_in, ffn_conv_w, ffn_conv_b, ffn_w_out):
    layers = _prepare_weights(norm_mix, norm_ffn, gdn_w_in, gdn_conv_w, gdn_a_log, gdn_dt_bias, gdn_norm, gdn_w_out,
                              fox_w_in, fox_b_f, fox_q_norm, fox_k_norm, fox_w_out, ffn_w_in, ffn_conv_w, ffn_conv_b,
                              ffn_w_out)
    prompt = _trunk_prompt(x_prompt, layers)
    sample = _trunk_decode(x_sample, state_gdn, state_gdn_conv, cache_k, cache_v, cache_logf, state_ffn_conv,
                           page_table, layers)
    return (prompt[0], sample[0]) + prompt[1:] + sample[1:]
```

```python
import functools

import jax
import jax.numpy as jnp
from jax import lax
from jax.experimental import pallas as pl
from jax.experimental.pallas import tpu as pltpu

F32 = jnp.float32
BF16 = jnp.bfloat16

HEAD_DIM = 128
GDN_CHUNK = 64
EPS = 1e-6
V7X_VMEM_LIMIT_BYTES = 56 * 1024 * 1024
ROW_TILE = 512
ATTN_TILE = 512
CONV_COLS = 512
FFN_COLS = 256
GDN_HEADS_PER_STEP = 16
GDN_SEQ_TILE = 512
GDN_SOLVE_BATCH = 16
GDN_DECODE_SEQS = 4
GDN_SOLVE_HEADS = 4


def _cparams(*semantics):
    return pltpu.CompilerParams(dimension_semantics=semantics, vmem_limit_bytes=V7X_VMEM_LIMIT_BYTES)


def _resident(shape):
    return pl.BlockSpec(shape, lambda *_: (0,) * len(shape), pipeline_mode=pl.Buffered(1))


def _layer_of(w):
    stacked, layer = w
    shape = stacked.shape[1:]
    return pl.BlockSpec((None,) + shape, lambda *_: (layer,) + (0,) * len(shape), pipeline_mode=pl.Buffered(1))


def _sigmoid(x):
    return 1.0 / (1.0 + jnp.exp(-x))


def _silu(x):
    return x * _sigmoid(x)


def _softplus(x):
    return jnp.maximum(x, 0.0) + jnp.log(1.0 + jnp.exp(-jnp.abs(x)))


def _rmsnorm_rows(x, w):
    return x * lax.rsqrt(jnp.mean(x * x, axis=-1, keepdims=True) + EPS) * w


def _dot(a, b):
    return jnp.dot(a, b, preferred_element_type=F32)


def _dot_nt(a, b):
    return lax.dot_general(a, b, (((1,), (1,)), ((), ())), preferred_element_type=F32)


def _dot_tn(a, b):
    return lax.dot_general(a, b, (((0,), (0,)), ((), ())), preferred_element_type=F32)


def _split3(x):
    hi = x.astype(BF16)
    r = x - hi.astype(F32)
    mid = r.astype(BF16)
    lo = (r - mid.astype(F32)).astype(BF16)
    return hi, mid, lo


def _dot_sel_left(sel, x):
    hi, mid, lo = _split3(x)
    return _dot(sel, hi) + _dot(sel, mid) + _dot(sel, lo)


def _dot_sel_right(x, sel):
    hi, mid, lo = _split3(x)
    return _dot(hi, sel) + _dot(mid, sel) + _dot(lo, sel)


def _iota2(shape, dim):
    return lax.broadcasted_iota(jnp.int32, shape, dim)


def _gdn_in_kernel(x_ref, nw_ref, w_ref, cw_ref, alog_ref, dtb_ref, buf_ref,
                   q_ref, k_ref, v_ref, z_ref, beta_ref, g_ref, cs_ref, ubuf, carry,
                   *, decode, tm, qk_width, conv_dim, n_vheads):
    z_cols = slice(conv_dim, conv_dim + n_vheads * HEAD_DIM)
    gate_cols = slice(conv_dim + n_vheads * HEAD_DIM, conv_dim + n_vheads * HEAD_DIM + 2 * n_vheads)
    hb = _rmsnorm_rows(x_ref[...], nw_ref[...]).astype(BF16)

    if not decode:
        @pl.when(pl.program_id(1) == 0)
        def _():
            carry[...] = jnp.zeros_like(carry)
            carry[5:8, :] = buf_ref[...]

    wc = CONV_COLS
    ahead = _dot(hb, w_ref[:, 0:wc])
    for j in range(conv_dim // wc):
        c0 = j * wc
        cols = slice(c0, c0 + wc)
        u = ahead
        if j + 1 < conv_dim // wc:
            ahead = _dot(hb, w_ref[:, c0 + wc:c0 + 2 * wc])
        else:
            ahead = _dot(hb, w_ref[:, z_cols])
        if decode:
            b0, b1, b2 = buf_ref[:, 0, cols], buf_ref[:, 1, cols], buf_ref[:, 2, cols]
            y = cw_ref[0:1, cols] * b0 + cw_ref[1:2, cols] * b1 + cw_ref[2:3, cols] * b2 + cw_ref[3:4, cols] * u
            cs_ref[:, 0, cols] = b1
            cs_ref[:, 1, cols] = b2
            cs_ref[:, 2, cols] = u
        else:
            ubuf[0:8, :] = carry[:, cols]
            ubuf[8:8 + tm, :] = u
            y = (cw_ref[3:4, cols] * u + cw_ref[2:3, cols] * ubuf[7:7 + tm, :]
                 + cw_ref[1:2, cols] * ubuf[6:6 + tm, :] + cw_ref[0:1, cols] * ubuf[5:5 + tm, :])
            carry[:, cols] = ubuf[tm:tm + 8, :]
        y = _silu(y)
        for hh in range(wc // HEAD_DIM):
            yh = y[:, hh * HEAD_DIM:(hh + 1) * HEAD_DIM]
            col = c0 + hh * HEAD_DIM
            if col < 2 * qk_width:
                yh = yh * lax.rsqrt(jnp.sum(yh * yh, axis=-1, keepdims=True) + EPS)
            dst, col = ((q_ref, col) if col < qk_width else
                        (k_ref, col - qk_width) if col < 2 * qk_width else (v_ref, col - 2 * qk_width))
            if decode:
                dst[:, col:col + HEAD_DIM] = yh
            else:
                dst[col // HEAD_DIM] = yh
    if not decode:
        cs_ref[...] = carry[5:8, :]

    z = ahead
    if decode:
        z_ref[...] = z
    else:
        for h in range(n_vheads):
            z_ref[h] = z[:, h * HEAD_DIM:(h + 1) * HEAD_DIM]
    ba = _dot(hb, w_ref[:, gate_cols])
    beta_ref[...] = _sigmoid(ba[:, :n_vheads])
    g = -jnp.exp(alog_ref[...]) * _softplus(ba[:, n_vheads:] + dtb_ref[...])
    if decode:
        g_ref[...] = g
    else:
        r, c = _iota2((tm, tm), 0), _iota2((tm, tm), 1)
        same_chunk_lower = jnp.where((r // GDN_CHUNK == c // GDN_CHUNK) & (c <= r), 1.0, 0.0).astype(BF16)
        g_ref[...] = _dot_sel_left(same_chunk_lower, g)


def _gdn_in(x, buf, nw, w_in, cw, alog, dtb, *, decode):
    bsz, seq, dm = x.shape
    conv_dim = cw.shape[1]
    vw = (w_in[0].shape[-1] - conv_dim) * HEAD_DIM // (HEAD_DIM + 2)
    qk_width = (conv_dim - vw) // 2
    n_vheads = vw // HEAD_DIM
    tm = min(ROW_TILE, seq)
    grid = (bsz, seq // tm)
    row = lambda width: pl.BlockSpec((None, tm, width), lambda b, i: (b, i, 0))
    if decode:
        buf, layer = buf
        buf_spec = pl.BlockSpec((None, tm, 3, conv_dim), lambda b, i: (layer, i, 0, 0))
        cs_shape, cs_spec = (seq, 3, conv_dim), pl.BlockSpec((tm, 3, conv_dim), lambda b, i: (i, 0, 0))
    else:
        buf_spec = pl.BlockSpec((None, 3, conv_dim), lambda b, i: (b, 0, 0))
        cs_shape, cs_spec = (bsz, 3, conv_dim), pl.BlockSpec((None, 3, conv_dim), lambda b, i: (b, 0, 0))
    kern = functools.partial(_gdn_in_kernel, decode=decode, tm=tm, qk_width=qk_width, conv_dim=conv_dim,
                             n_vheads=n_vheads)
    if decode:
        wide = lambda width: (row(width), jax.ShapeDtypeStruct((bsz, seq, width), F32))
    else:
        wide = lambda width: (pl.BlockSpec((None, width // HEAD_DIM, tm, HEAD_DIM), lambda b, i: (b, 0, i, 0)),
                              jax.ShapeDtypeStruct((bsz, width // HEAD_DIM, seq, HEAD_DIM), F32))
    (q_spec, q_shape), (v_spec, v_shape) = wide(qk_width), wide(vw)
    gate_shape = jax.ShapeDtypeStruct((bsz, seq, n_vheads), F32)
    return pl.pallas_call(
        kern,
        grid=grid,
        in_specs=[row(dm), _resident((1, dm)), _layer_of(w_in), _resident(cw.shape), _resident((1, n_vheads)),
                  _resident((1, n_vheads)), buf_spec],
        out_specs=[q_spec, q_spec, v_spec, v_spec, row(n_vheads), row(n_vheads), cs_spec],
        out_shape=[q_shape, q_shape, v_shape, v_shape, gate_shape, gate_shape, jax.ShapeDtypeStruct(cs_shape, F32)],
        scratch_shapes=[pltpu.VMEM((tm + 8, CONV_COLS), F32), pltpu.VMEM((8, conv_dim), F32)],
        compiler_params=_cparams("arbitrary", "arbitrary"),
        name="gdn_in_decode" if decode else "gdn_in_prompt",
    )(x, nw, w_in[0], cw, alog, dtb, buf)


def _bdot(a, b):
    return lax.dot_general(a, b, (((2,), (1,)), ((0,), (0,))), preferred_element_type=F32)


def _bdot_nt(a, b):
    return lax.dot_general(a, b, (((2,), (2,)), ((0,), (0,))), preferred_element_type=F32)


def _unit_lower_inverse(n, row, col):
    eye = jnp.where(row == col, 1.0, 0.0)
    inv = eye - jnp.where((row >> 1) == (col >> 1), n, 0.0)
    size = n.shape[-1]
    bits = 2
    while (1 << bits) <= size:
        off_diag = jnp.where((row >> bits) == (col >> bits),
                             jnp.where((row >> (bits - 1)) == (col >> (bits - 1)), 0.0, n), 0.0)
        inv_b = inv.astype(BF16)
        inv = inv - _bdot(inv_b, _bdot(off_diag.astype(BF16), inv_b).astype(BF16))
        bits += 1
    n_hi = n.astype(BF16).astype(F32)
    x_hi = inv.astype(BF16).astype(F32)
    lhs = jnp.concatenate([n_hi, n - n_hi, n_hi], axis=-1).astype(BF16)
    rhs = jnp.concatenate([x_hi, x_hi, inv - x_hi], axis=1).astype(BF16)
    resid = eye - inv - _bdot(lhs, rhs)
    return inv + _bdot(x_hi.astype(BF16), resid.astype(BF16))


def _gdn_core_kernel(q_ref, k_ref, v_ref, z_ref, beta_ref, gcum_ref, nw_ref, o_ref, s_out_ref,
                     s_scr, u_scr, w_scr, att_scr, qg_scr, kd_scr, egl_scr, gb_scr, *, lt, hb, rep, nb, hs):
    c = GDN_CHUNK
    groups = lt // c // nb
    t = pl.program_id(2)
    head0 = pl.program_id(1) * hb
    q_scale = HEAD_DIM ** -0.5

    @pl.when(t == 0)
    def _():
        s_scr[...] = jnp.zeros_like(s_scr)

    nbh = nb * hs
    row = lax.broadcasted_iota(jnp.int32, (nbh, c, c), 1)
    col = lax.broadcasted_iota(jnp.int32, (nbh, c, c), 2)
    lower = row >= col
    strict = row > col
    lane = _iota2((nb * c, HEAD_DIM), 1)
    gate_lane = _iota2((nb * c, beta_ref.shape[-1]), 1)
    pick = jnp.where(_iota2((nbh * c, HEAD_DIM), 1) < 3, 1.0, 0.0).astype(BF16).reshape(nbh, c, HEAD_DIM)

    def solve(idx, _):
        hset = idx // groups
        chunk0 = pl.multiple_of((idx - hset * groups) * nb, nb)
        rows = pl.ds(pl.multiple_of(chunk0 * c, nb * c), nb * c)
        heads = [hset * hs + m for m in range(hs)]
        per_head = []
        for hh in heads:
            own = gate_lane == head0 + hh
            bcol = jnp.sum(jnp.where(own, beta_ref[rows, :], 0.0), axis=-1, keepdims=True)
            gcol = jnp.sum(jnp.where(own, gcum_ref[rows, :], 0.0), axis=-1, keepdims=True)
            g_hi = gcol.astype(BF16).astype(F32)
            g_mid = (gcol - g_hi).astype(BF16).astype(F32)
            g_lo = gcol - g_hi - g_mid
            pieces = jnp.where(lane == 0, g_hi, jnp.where(lane == 1, g_mid, jnp.where(lane == 2, g_lo, 0.0)))
            per_head.append((gcol, bcol, pieces.astype(BF16), q_ref[hh // rep, rows, :] * q_scale,
                             k_ref[hh // rep, rows, :], v_ref[hh, rows, :]))
        stack = lambda i, width: jnp.concatenate([p[i].reshape(nb, c, width) for p in per_head], axis=0)
        g3, b3 = stack(0, 1), stack(1, 1)
        gcum_t = _bdot_nt(pick, stack(2, HEAD_DIM))
        decay = jnp.where(lower, jnp.exp(jnp.minimum(g3 - gcum_t, 0.0)), 0.0)
        q3, k3, v3 = stack(3, HEAD_DIM), stack(4, HEAD_DIM), stack(5, HEAD_DIM)
        kb = k3 * b3
        k_bf = k3.astype(BF16)
        n = jnp.where(strict, _bdot_nt(kb.astype(BF16), k_bf) * decay, 0.0)
        inv = _unit_lower_inverse(n, row, col)
        eg = jnp.exp(g3)
        rhs = jnp.concatenate([v3 * b3, kb * eg], axis=-1).astype(BF16)
        sol = _bdot(inv.astype(BF16), rhs)
        attn = jnp.where(lower, _bdot_nt(q3.astype(BF16), k_bf) * decay, 0.0)
        g_last = g3[:, c - 1:c, :]
        qg = q3 * eg
        kd = k3 * jnp.exp(g_last - g3)
        for m, hh in enumerate(heads):
            mine = lambda a: a[m * nb:(m + 1) * nb].reshape(nb * c, a.shape[-1])
            u_scr[hh, rows, :] = mine(sol[:, :, :HEAD_DIM])
            w_scr[hh, rows, :] = mine(sol[:, :, HEAD_DIM:]).astype(BF16)
            att_scr[hh, rows, :] = mine(attn).astype(BF16)
            qg_scr[hh, rows, :] = mine(qg).astype(BF16)
            kd_scr[hh, rows, :] = mine(kd).astype(BF16)
            gb_scr[m] = jnp.broadcast_to(per_head[m][0], gb_scr.shape[1:])
            egl_scr[hh, pl.ds(chunk0, nb), :] = jnp.exp(gb_scr[m, pl.ds(c - 1, nb, stride=c), :])
        return 0

    lax.fori_loop(0, hb // hs * groups, solve, 0)

    def step(i, _):
        rows = pl.ds(pl.multiple_of(i * c, c), c)
        heads = range(hb)
        state = [s_scr[hh] for hh in heads]
        res = [_dot(jnp.concatenate([w_scr[hh, rows, :], qg_scr[hh, rows, :]], axis=0), state[hh].astype(BF16))
               for hh in heads]
        v_new = [(u_scr[hh, rows, :] - res[hh][:c]).astype(BF16) for hh in heads]
        o = [res[hh][c:] + _dot(att_scr[hh, rows, :], v_new[hh]) for hh in heads]
        grown = [_dot_tn(kd_scr[hh, rows, :], v_new[hh]) for hh in heads]
        for hh in heads:
            s_scr[hh] = state[hh] * egl_scr[hh, pl.ds(i, 1), :] + grown[hh]
            gated = _rmsnorm_rows(o[hh], nw_ref[...]) * _silu(z_ref[hh, rows, :])
            o_ref[rows, hh * HEAD_DIM:(hh + 1) * HEAD_DIM] = gated.astype(o_ref.dtype)
        return 0

    lax.fori_loop(0, lt // c, step, 0)

    @pl.when(t == pl.num_programs(2) - 1)
    def _():
        s_out_ref[...] = s_scr[...]


def _gdn_core_prompt(q, k, v, z, beta, gcum, nw):
    bsz, n_vheads, seq, _ = v.shape
    rep = n_vheads // q.shape[1]
    hb = min(GDN_HEADS_PER_STEP, n_vheads)
    lt = min(GDN_SEQ_TILE, seq)
    nb = min(GDN_SOLVE_BATCH, lt // GDN_CHUNK)
    hs = min(GDN_SOLVE_HEADS, hb)
    heads = lambda n: pl.BlockSpec((None, n, lt, HEAD_DIM), lambda b, h, t: (b, h, t, 0))
    gates = pl.BlockSpec((None, lt, n_vheads), lambda b, h, t: (b, t, 0))
    per_head = lambda width, dtype: pltpu.VMEM((hb, lt, width), dtype)
    return pl.pallas_call(
        functools.partial(_gdn_core_kernel, lt=lt, hb=hb, rep=rep, nb=nb, hs=hs),
        grid=(bsz, n_vheads // hb, seq // lt),
        in_specs=[heads(hb // rep), heads(hb // rep), heads(hb), heads(hb), gates, gates, _resident((1, HEAD_DIM))],
        out_specs=[pl.BlockSpec((None, lt, hb * HEAD_DIM), lambda b, h, t: (b, t, h)),
                   pl.BlockSpec((None, hb, HEAD_DIM, HEAD_DIM), lambda b, h, t: (b, h, 0, 0))],
        out_shape=[jax.ShapeDtypeStruct((bsz, seq, n_vheads * HEAD_DIM), BF16),
                   jax.ShapeDtypeStruct((bsz, n_vheads, HEAD_DIM, HEAD_DIM), F32)],
        scratch_shapes=[pltpu.VMEM((hb, HEAD_DIM, HEAD_DIM), F32), per_head(HEAD_DIM, F32), per_head(HEAD_DIM, BF16),
                        per_head(GDN_CHUNK, BF16), per_head(HEAD_DIM, BF16), per_head(HEAD_DIM, BF16),
                        pltpu.VMEM((hb, lt // GDN_CHUNK, HEAD_DIM), F32),
                        pltpu.VMEM((hs, nb * GDN_CHUNK, HEAD_DIM), F32)],
        compiler_params=_cparams("arbitrary", "arbitrary", "arbitrary"),
        name="gdn_core_prompt",
    )(q, k, v, z, beta, gcum, nw)


def _gdn_core_decode_kernel(q_ref, k_ref, v_ref, z_ref, beta_ref, g_ref, nw_ref, s_ref, o_ref, so_ref,
                            *, n_vheads, rep):
    q_scale = HEAD_DIM ** -0.5
    row = _iota2((8, HEAD_DIM), 0)

    def rows3(a, b, c):
        return jnp.where(row == 0, a, jnp.where(row == 1, b, jnp.where(row == 2, c, 0.0)))

    pairs = [(s, h) for s in range(q_ref.shape[0]) for h in range(n_vheads)]
    idx = range(len(pairs))
    lanes = [slice(h * HEAD_DIM, (h + 1) * HEAD_DIM) for _, h in pairs]
    klanes = [slice(h // rep * HEAD_DIM, (h // rep + 1) * HEAD_DIM) for _, h in pairs]
    qh = [q_ref[s, :, klanes[i]] * q_scale for i, (s, _) in enumerate(pairs)]
    kh = [k_ref[s, :, klanes[i]] for i, (s, _) in enumerate(pairs)]
    beta = [beta_ref[s, :, h:h + 1] for s, h in pairs]
    eg = [jnp.exp(g_ref[s, :, h:h + 1]) for s, h in pairs]
    res = [_dot(rows3(kh[i] * beta[i] * eg[i], qh[i] * eg[i], 0.0).astype(BF16), s_ref[pairs[i]].astype(BF16))
           for i in idx]
    v_new = [v_ref[pairs[i][0], :, lanes[i]] * beta[i] - res[i][0:1, :] for i in idx]
    k_hi = [kh[i].astype(BF16).astype(F32) for i in idx]
    v_hi = [v_new[i].astype(BF16).astype(F32) for i in idx]
    grown = [_dot_tn(rows3(k_hi[i], kh[i] - k_hi[i], k_hi[i]).astype(BF16),
                     rows3(v_hi[i], v_hi[i], v_new[i] - v_hi[i]).astype(BF16)) for i in idx]
    for i in idx:
        s = pairs[i][0]
        so_ref[pairs[i]] = s_ref[pairs[i]] * eg[i] + grown[i]
        o = res[i][1:2, :] + jnp.sum(qh[i] * kh[i], axis=-1, keepdims=True) * v_new[i]
        gated = _rmsnorm_rows(o, nw_ref[...]) * _silu(z_ref[s, :, lanes[i]])
        o_ref[s, :, lanes[i]] = gated.astype(o_ref.dtype)


def _gdn_core_decode(q, k, v, z, beta, g, nw, state):
    nseq, _, vw = v.shape
    n_vheads = vw // HEAD_DIM
    rep = vw // q.shape[-1]
    sb = GDN_DECODE_SEQS if nseq % GDN_DECODE_SEQS == 0 else 1
    row = lambda width: pl.BlockSpec((sb, 1, width), lambda s: (s, 0, 0))
    st = pl.BlockSpec((sb, n_vheads, HEAD_DIM, HEAD_DIM), lambda s: (s, 0, 0, 0))
    return pl.pallas_call(
        functools.partial(_gdn_core_decode_kernel, n_vheads=n_vheads, rep=rep),
        grid=(nseq // sb,),
        in_specs=[row(q.shape[-1]), row(q.shape[-1]), row(vw), row(vw), row(n_vheads), row(n_vheads),
                  _resident((1, HEAD_DIM)), st],
        out_specs=[row(vw), st],
        out_shape=[jax.ShapeDtypeStruct((nseq, 1, vw), BF16), jax.ShapeDtypeStruct(state.shape, F32)],
        compiler_params=_cparams("arbitrary"),
        name="gdn_core_decode",
    )(q, k, v, z, beta, g, nw, state)


def _out_ffn_kernel(x_ref, o_ref, wo_ref, nw_ref, w1_ref, cw_ref, cb_ref, w2_ref, buf_ref,
                    y_ref, cs_ref, gbuf, ubuf, carry, act, *, decode, tm, d_ff):
    x1 = x_ref[...] + _dot(o_ref[...], wo_ref[...])
    hb = _rmsnorm_rows(x1, nw_ref[...]).astype(BF16)

    if not decode:
        @pl.when(pl.program_id(1) == 0)
        def _():
            carry[...] = jnp.zeros_like(carry)
            carry[6:8, :] = buf_ref[...]

    def conv(u, scratch, c0, wc):
        cols = slice(c0, c0 + wc)
        if decode:
            b0 = buf_ref[:, 0, cols]
            b1 = buf_ref[:, 1, cols]
            cs_ref[:, 0, cols] = b1
            cs_ref[:, 1, cols] = u
            y = cw_ref[0:1, cols] * b0 + cw_ref[1:2, cols] * b1 + cw_ref[2:3, cols] * u
        else:
            scratch[0:8, :] = carry[:, cols]
            scratch[8:8 + tm, :] = u
            y = (cw_ref[2:3, cols] * u + cw_ref[1:2, cols] * scratch[7:7 + tm, :]
                 + cw_ref[0:1, cols] * scratch[6:6 + tm, :])
            carry[:, cols] = scratch[tm:tm + 8, :]
        return y + cb_ref[:, cols]

    wc = FFN_COLS

    def project(j):
        c0 = j * wc
        return _dot(hb, w1_ref[:, c0:c0 + wc]), _dot(hb, w1_ref[:, d_ff + c0:d_ff + c0 + wc])

    ahead = project(0)
    for j in range(d_ff // wc):
        c0 = j * wc
        pre_gate, pre_up = ahead
        if j + 1 < d_ff // wc:
            ahead = project(j + 1)
        gate = conv(pre_gate, gbuf, c0, wc)
        up = conv(pre_up, ubuf, d_ff + c0, wc)
        act[:, c0:c0 + wc] = (_silu(gate) * up).astype(BF16)
    if not decode:
        cs_ref[...] = carry[6:8, :]
    y_ref[...] = x1 + _dot(act[...], w2_ref[...])


def _out_ffn(x, o, wo, nw, w1, cw, cb, w2, buf, *, decode):
    bsz, seq, dm = x.shape
    d_ff = w2[0].shape[1]
    tm = min(ROW_TILE, seq)
    row = lambda width: pl.BlockSpec((None, tm, width), lambda b, i: (b, i, 0))
    if decode:
        buf, layer = buf
        buf_spec = pl.BlockSpec((None, tm, 2, 2 * d_ff), lambda b, i: (layer, i, 0, 0))
        cs_shape, cs_spec = (seq, 2, 2 * d_ff), pl.BlockSpec((tm, 2, 2 * d_ff), lambda b, i: (i, 0, 0))
    else:
        buf_spec = pl.BlockSpec((None, 2, 2 * d_ff), lambda b, i: (b, 0, 0))
        cs_shape, cs_spec = (bsz, 2, 2 * d_ff), pl.BlockSpec((None, 2, 2 * d_ff), lambda b, i: (b, 0, 0))
    return pl.pallas_call(
        functools.partial(_out_ffn_kernel, decode=decode, tm=tm, d_ff=d_ff),
        grid=(bsz, seq // tm),
        in_specs=[row(dm), row(o.shape[-1]), _layer_of(wo), _resident((1, dm)), _layer_of(w1),
                  _resident(cw.shape), _resident((1, 2 * d_ff)), _layer_of(w2), buf_spec],
        out_specs=[row(dm), cs_spec],
        out_shape=[jax.ShapeDtypeStruct((bsz, seq, dm), F32), jax.ShapeDtypeStruct(cs_shape, F32)],
        scratch_shapes=[pltpu.VMEM((tm + 8, FFN_COLS), F32), pltpu.VMEM((tm + 8, FFN_COLS), F32),
                        pltpu.VMEM((8, 2 * d_ff), F32), pltpu.VMEM((tm, d_ff), BF16)],
        compiler_params=_cparams("arbitrary", "arbitrary"),
        name="out_ffn_decode" if decode else "out_ffn_prompt",
    )(x, o, wo[0], nw, w1[0], cw, cb, w2[0], buf)


def _fox_in_kernel(x_ref, nw_ref, w_ref, bf_ref, bft_ref, qn_ref, kn_ref,
                   q_ref, k_ref, v_ref, og_ref, lf_ref, cum_ref, carry, *, tm, n_heads):
    width = n_heads * HEAD_DIM
    hb = _rmsnorm_rows(x_ref[...], nw_ref[...]).astype(BF16)
    q = _dot(hb, w_ref[:, 0:width])
    k = _dot(hb, w_ref[:, width:2 * width])
    for h in range(n_heads):
        lanes = slice(h * HEAD_DIM, (h + 1) * HEAD_DIM)
        q_ref[:, lanes] = _rmsnorm_rows(q[:, lanes], qn_ref[...])
    v_ref[...] = _dot(hb, w_ref[:, 2 * width:3 * width])
    for h in range(n_heads):
        lanes = slice(h * HEAD_DIM, (h + 1) * HEAD_DIM)
        k_ref[:, lanes] = _rmsnorm_rows(k[:, lanes], kn_ref[...])
    og_ref[...] = _dot(hb, w_ref[:, 3 * width:4 * width])
    lf_ref[...] = -_softplus(-(_dot(hb, w_ref[:, 4 * width:4 * width + n_heads]) + bf_ref[...]))

    @pl.when(pl.program_id(1) == 0)
    def _():
        carry[...] = jnp.zeros_like(carry)

    logits_t = lax.dot_general(w_ref[:, 4 * width:4 * width + n_heads], hb, (((0,), (1,)), ((), ())),
                               preferred_element_type=F32)
    lf_t = -_softplus(-(logits_t + bft_ref[...]))
    upper = jnp.where(_iota2((tm, tm), 0) <= _iota2((tm, tm), 1), 1.0, 0.0).astype(BF16)
    cum = _dot_sel_right(lf_t, upper) + carry[:, 0:1]
    cum_ref[...] = cum
    carry[...] = jnp.broadcast_to(cum[:, tm - 1:tm], carry.shape)


def _fox_in(x, nw, w_in, bf, bft, qn, kn):
    bsz, seq, dm = x.shape
    n_heads = bf.shape[1]
    width = n_heads * HEAD_DIM
    tm = min(ROW_TILE, seq)
    row = lambda w: pl.BlockSpec((None, tm, w), lambda b, i: (b, i, 0))
    wide = jax.ShapeDtypeStruct((bsz, seq, width), F32)
    return pl.pallas_call(
        functools.partial(_fox_in_kernel, tm=tm, n_heads=n_heads),
        grid=(bsz, seq // tm),
        in_specs=[row(dm), _resident((1, dm)), _layer_of(w_in), _resident((1, n_heads)),
                  _resident((n_heads, 1)), _resident((1, HEAD_DIM)), _resident((1, HEAD_DIM))],
        out_specs=[row(width), row(width), row(width), row(width), row(n_heads),
                   pl.BlockSpec((None, n_heads, tm), lambda b, i: (b, 0, i))],
        out_shape=[wide, wide, wide, wide, jax.ShapeDtypeStruct((bsz, seq, n_heads), F32),
                   jax.ShapeDtypeStruct((bsz, n_heads, seq), F32)],
        scratch_shapes=[pltpu.VMEM((n_heads, HEAD_DIM), F32)],
        compiler_params=_cparams("arbitrary", "arbitrary"),
        name="fox_in",
    )(x, nw, w_in[0], bf, bft, qn, kn)


def _fox_attn_kernel(q_ref, k_ref, v_ref, cum_ref, og_ref, o_ref, kb_scr, vb_scr, *, tile):
    h = pl.program_id(1)
    qi = pl.program_id(2)
    scale = HEAD_DIM ** -0.5
    exp2_scale = scale * 1.4426950408889634

    @pl.when(qi == 0)
    def _():
        kb_scr[...] = k_ref[...].astype(BF16)
        vb_scr[...] = v_ref[...].astype(BF16)

    qb = q_ref[...].astype(BF16)

    def lane_tiles(x):
        return [x[:, t * HEAD_DIM:(t + 1) * HEAD_DIM] for t in range(tile // HEAD_DIM)]

    def key_block(j, carry, diagonal):
        m_prev, l_prev, acc = carry
        cols = pl.ds(pl.multiple_of(j * tile, tile), tile)
        s = _dot_nt(qb, kb_scr[cols, :]) - cum_ref[pl.ds(h, 1), cols] * (1.0 / scale)
        if diagonal:
            s = jnp.where(_iota2((tile, tile), 1) <= _iota2((tile, tile), 0), s, -jnp.inf)
        m_new = jnp.maximum(m_prev, jnp.max(functools.reduce(jnp.maximum, lane_tiles(s)), axis=-1, keepdims=True))
        alpha = jnp.exp2((m_prev - m_new) * exp2_scale)
        p = jnp.exp2((s - m_new) * exp2_scale)
        l_new = alpha * l_prev + jnp.sum(functools.reduce(jnp.add, lane_tiles(p)), axis=-1, keepdims=True)
        acc = alpha * acc + _dot(p.astype(BF16), vb_scr[cols, :])
        return m_new, l_new, acc

    init = (jnp.full((tile, 1), -jnp.inf, F32), jnp.zeros((tile, 1), F32), jnp.zeros((tile, HEAD_DIM), F32))
    carry = lax.fori_loop(0, qi, lambda j, c: key_block(j, c, False), init)
    _, l_fin, acc = key_block(qi, carry, True)
    o_ref[...] = (acc / l_fin * _sigmoid(og_ref[...])).astype(o_ref.dtype)


def _fox_attn_prompt(q, k, v, cum, og):
    bsz, seq, width = q.shape
    n_heads = width // HEAD_DIM
    tile = min(ATTN_TILE, seq)
    q_spec = pl.BlockSpec((None, tile, HEAD_DIM), lambda b, h, i: (b, i, h))
    kv_spec = pl.BlockSpec((None, seq, HEAD_DIM), lambda b, h, i: (b, 0, h))
    cum_spec = pl.BlockSpec((None, n_heads, seq), lambda b, h, i: (b, 0, 0))
    return pl.pallas_call(
        functools.partial(_fox_attn_kernel, tile=tile),
        grid=(bsz, n_heads, seq // tile),
        in_specs=[q_spec, kv_spec, kv_spec, cum_spec, q_spec],
        out_specs=q_spec,
        out_shape=jax.ShapeDtypeStruct((bsz, seq, width), BF16),
        scratch_shapes=[pltpu.VMEM((seq, HEAD_DIM), BF16), pltpu.VMEM((seq, HEAD_DIM), BF16)],
        compiler_params=_cparams("arbitrary", "arbitrary", "arbitrary"),
        name="fox_attn_prompt",
    )(q, k, v, cum, og)


def _fox_attn_decode_kernel(pt_ref, q_ref, kn_ref, vn_ref, lfn_ref, og_ref, *rest, n_heads, page, n_pages):
    del pt_ref
    k_pages, v_pages, lf_pages = rest[:n_pages], rest[n_pages:2 * n_pages], rest[2 * n_pages:3 * n_pages]
    o_ref, lf_scr = rest[3 * n_pages:]
    flat = n_heads * page
    scale = HEAD_DIM ** -0.5

    for p in range(n_pages):
        lf_scr[p:p + 1, :] = lf_pages[p][...]
    lf = lf_scr[...]
    lane = _iota2((n_pages, flat), 1)
    cum, page_total = lf, lf
    shift = n_heads
    while shift < flat:
        cum = cum + jnp.where(lane >= shift, pltpu.roll(cum, shift, axis=1), 0.0)
        page_total = page_total + pltpu.roll(page_total, shift, axis=1)
        shift *= 2
    earlier = jnp.where(_iota2((n_pages, n_pages), 1) < _iota2((n_pages, n_pages), 0), 1.0, 0.0).astype(BF16)
    before = _dot_sel_left(earlier, page_total)
    cum = cum + before
    past_total = before[n_pages - 1:n_pages, :] + page_total[n_pages - 1:n_pages, :]

    own_head = (_iota2((n_heads, flat), 1) & (n_heads - 1)) == _iota2((n_heads, flat), 0)
    eye = _iota2((n_heads, n_heads), 0) == _iota2((n_heads, n_heads), 1)
    to_rows = lambda r: jnp.sum(jnp.where(eye, jnp.broadcast_to(r, (n_heads, n_heads)), 0.0), axis=-1, keepdims=True)

    q = q_ref[...]
    qb = q.astype(BF16)
    scores = []
    for p in range(n_pages):
        s = _dot_nt(qb, k_pages[p][...].astype(BF16)) * scale - cum[p:p + 1, :]
        scores.append(jnp.where(own_head, s, -jnp.inf))
    s_new = (jnp.sum(q * kn_ref[...], axis=-1, keepdims=True) * scale
             - (to_rows(past_total[:, :n_heads]) + to_rows(lfn_ref[...])))
    m = s_new
    for s in scores:
        m = jnp.maximum(m, jnp.max(s, axis=-1, keepdims=True))
    denom = jnp.exp(s_new - m)
    acc = denom * vn_ref[...]
    for p in range(n_pages):
        w = jnp.exp(scores[p] - m)
        denom = denom + jnp.sum(w, axis=-1, keepdims=True)
        acc = acc + _dot(w.astype(BF16), v_pages[p][...].astype(BF16))
    o_ref[...] = (acc / denom * _sigmoid(og_ref[...])).astype(o_ref.dtype)


def _fox_attn_decode(page_table, q, k_new, v_new, lf_new, og, k_pool, v_pool, lf_pool):
    nseq, n_heads, _ = q.shape
    n_pages = page_table.shape[1]
    flat = k_pool.shape[1]
    page = flat // n_heads
    tok = pl.BlockSpec((None, n_heads, HEAD_DIM), lambda s, pt: (s, 0, 0))
    pool = [pl.BlockSpec((None, flat, HEAD_DIM), lambda s, pt, p=p: (pt[s, p], 0, 0)) for p in range(n_pages)]
    lf_specs = [pl.BlockSpec((None, 1, flat), lambda s, pt, p=p: (pt[s, p], 0, 0)) for p in range(n_pages)]
    grid_spec = pltpu.PrefetchScalarGridSpec(
        num_scalar_prefetch=1,
        grid=(nseq,),
        in_specs=[tok, tok, tok, pl.BlockSpec((None, 1, n_heads), lambda s, pt: (s, 0, 0)), tok] + pool + pool + lf_specs,
        out_specs=tok,
        scratch_shapes=[pltpu.VMEM((n_pages, flat), F32)],
    )
    return pl.pallas_call(
        functools.partial(_fox_attn_decode_kernel, n_heads=n_heads, page=page, n_pages=n_pages),
        grid_spec=grid_spec,
        out_shape=jax.ShapeDtypeStruct((nseq, n_heads, HEAD_DIM), BF16),
        compiler_params=_cparams("arbitrary"),
        name="fox_attn_decode",
    )(page_table, q, k_new, v_new, lf_new, og, *([k_pool] * n_pages), *([v_pool] * n_pages), *([lf_pool] * n_pages))


def _prepare_weights(norm_mix, norm_ffn, gdn_w_in, gdn_conv_w, gdn_a_log, gdn_dt_bias, gdn_norm, gdn_w_out,
                     fox_w_in, fox_b_f, fox_q_norm, fox_k_norm, fox_w_out, ffn_w_in, ffn_conv_w, ffn_conv_b,
                     ffn_w_out):
    fw = fox_w_out.shape[1]
    gdn_in_b, gdn_out_b = gdn_w_in.astype(BF16), gdn_w_out.astype(BF16)
    fox_in_b, fox_out_b = fox_w_in.astype(BF16), fox_w_out.astype(BF16)
    ffn_in_b, ffn_out_b = ffn_w_in.astype(BF16), ffn_w_out.astype(BF16)
    layers = []
    for i in range(norm_mix.shape[0]):
        j = i // 2
        ffn = dict(nw=norm_ffn[i][None], w1=(ffn_in_b, i), cw=ffn_conv_w[i], cb=ffn_conv_b[i][None], w2=(ffn_out_b, i))
        if i % 2 == 0:
            mixer = dict(nw=norm_mix[i][None], w_in=(gdn_in_b, j), cw=gdn_conv_w[j], alog=gdn_a_log[j][None],
                         dtb=gdn_dt_bias[j][None], onw=gdn_norm[j][None], wo=(gdn_out_b, j))
        else:
            mixer = dict(nw=norm_mix[i][None], w_in=(fox_in_b, j),
                         bf=fox_b_f[j][None], bft=fox_b_f[j][:, None], qn=fox_q_norm[j][None],
                         kn=fox_k_norm[j][None], wo=(fox_out_b, j))
        layers.append((mixer, ffn))
    return layers


def _trunk_prompt(x, layers):
    bsz, seq, _ = x.shape
    gdn_s, gdn_cb, ks, vs, lfs, ffn_cb = [], [], [], [], [], []
    for i, (m, f) in enumerate(layers):
        if i % 2 == 0:
            conv_dim = m["cw"].shape[1]
            q, k, v, z, beta, g, cb = _gdn_in(x, jnp.zeros((bsz, 3, conv_dim), F32), m["nw"], m["w_in"], m["cw"],
                                              m["alog"], m["dtb"], decode=False)
            o, s = _gdn_core_prompt(q, k, v, z, beta, g, m["onw"])
            gdn_cb.append(cb)
            gdn_s.append(s)
        else:
            q, k, v, og, lf, cum = _fox_in(x, m["nw"], m["w_in"], m["bf"], m["bft"], m["qn"], m["kn"])
            o = _fox_attn_prompt(q, k, v, cum, og)
            n_heads = lf.shape[-1]
            ks.append(k.reshape(bsz, seq, n_heads, HEAD_DIM))
            vs.append(v.reshape(bsz, seq, n_heads, HEAD_DIM))
            lfs.append(lf)
        x, fcb = _out_ffn(x, o, m["wo"], f["nw"], f["w1"], f["cw"], f["cb"], f["w2"],
                          jnp.zeros((bsz, 2, f["cw"].shape[1]), F32), decode=False)
        ffn_cb.append(fcb)
    return x, jnp.stack(gdn_s), jnp.stack(gdn_cb), jnp.stack(ks), jnp.stack(vs), jnp.stack(lfs), jnp.stack(ffn_cb)


def _trunk_decode(x, state_gdn, state_gdn_conv, cache_k, cache_v, cache_logf, state_ffn_conv, page_table, layers):
    nseq = x.shape[0]
    x = x.reshape(1, nseq, x.shape[-1])
    gdn_s, gdn_cb, ks, vs, lfs, ffn_cb = [], [], [], [], [], []
    for i, (m, f) in enumerate(layers):
        j = i // 2
        if i % 2 == 0:
            q, k, v, z, beta, g, cb = _gdn_in(x, (state_gdn_conv, j), m["nw"], m["w_in"], m["cw"], m["alog"],
                                              m["dtb"], decode=True)
            per_seq = lambda a: a.reshape(nseq, 1, a.shape[-1])
            o, s = _gdn_core_decode(per_seq(q), per_seq(k), per_seq(v), per_seq(z), per_seq(beta), per_seq(g),
                                    m["onw"], state_gdn[j])
            o = o.reshape(1, nseq, -1)
            gdn_cb.append(cb)
            gdn_s.append(s)
        else:
            q, k, v, og, lf, _ = _fox_in(x, m["nw"], m["w_in"], m["bf"], m["bft"], m["qn"], m["kn"])
            n_heads = lf.shape[-1]
            n_pool, page = cache_k.shape[1], cache_k.shape[2]
            heads = lambda a: a.reshape(nseq, n_heads, HEAD_DIM)
            o = _fox_attn_decode(page_table, heads(q), heads(k), heads(v), lf.reshape(nseq, 1, n_heads), heads(og),
                                 cache_k[j].reshape(n_pool, page * n_heads, HEAD_DIM),
                                 cache_v[j].reshape(n_pool, page * n_heads, HEAD_DIM),
                                 cache_logf[j].reshape(n_pool, 1, page * n_heads))
            o = o.reshape(1, nseq, -1)
            ks.append(k.reshape(nseq, 1, n_heads, HEAD_DIM))
            vs.append(v.reshape(nseq, 1, n_heads, HEAD_DIM))
            lfs.append(lf.reshape(nseq, 1, n_heads))
        x, fcb = _out_ffn(x, o, m["wo"], f["nw"], f["w1"], f["cw"], f["cb"], f["w2"],
                          (state_ffn_conv, i), decode=True)
        ffn_cb.append(fcb)
    return (x.reshape(nseq, 1, -1), jnp.stack(gdn_s), jnp.stack(gdn_cb), jnp.stack(ks), jnp.stack(vs),
            jnp.stack(lfs), jnp.stack(ffn_cb))


def kernel(x_prompt, x_sample, state_gdn, state_gdn_conv, cache_k, cache_v, cache_logf, state_ffn_conv, page_table,
           norm_mix, norm_ffn, gdn_w_in, gdn_conv_w, gdn_a_log, gdn_dt_bias, gdn_norm, gdn_w_out, fox_w_in, fox_b_f,
           fox_q_norm, fox_k_norm, fox_w_out, ffn_w_in, ffn_conv_w, ffn_conv_b, ffn_w_out):
    layers = _prepare_weights(norm_mix, norm_ffn, gdn_w_in, gdn_conv_w, gdn_a_log, gdn_dt_bias, gdn_norm, gdn_w_out,
                              fox_w_in, fox_b_f, fox_q_norm, fox_k_norm, fox_w_out, ffn_w_in, ffn_conv_w, ffn_conv_b,
                              ffn_w_out)
    prompt = _trunk_prompt(x_prompt, layers)
    sample = _trunk_decode(x_sample, state_gdn, state_gdn_conv, cache_k, cache_v, cache_logf, state_ffn_conv,
                           page_table, layers)
    return (prompt[0], sample[0]) + prompt[1:] + sample[1:]
```
